```python
import math
import jax, jax.numpy as jnp
from jax import lax
import numpy as np

D_MODEL = 1024
BATCH = 8
SEQ = 2048
DEPTH = 4
DEC_BATCH = 128
DEC_SEQ = 4
PAST_LEN = 16384
PAGE_SIZE = 128

N_META = 16
D_RNN = D_MODEL
N_RNN_HEADS = 16
RNN_HEAD_DIM = D_RNN // N_RNN_HEADS
RNN_CONV_W = 4
LRU_C = 8.0
D_CONV = D_MODEL
N_CONV_GROUPS = 16
SC_CONV_W = 3
D_FF = int(math.ceil((8 * D_MODEL / 3) / 256) * 256)
EPS = 1e-6
SPLITS = (D_RNN, D_RNN, D_CONV, D_CONV, D_CONV, D_MODEL, D_MODEL)
D_IN = sum(SPLITS)

kernel_name = "hawk_shortconv_parallel_meta_decoder_step"


def rmsnorm(x, g):
    xf = x.astype(jnp.float32)
    y = xf * lax.rsqrt(jnp.mean(xf * xf, axis=-1, keepdims=True) + EPS)
    return (y * g.astype(jnp.float32)).astype(x.dtype)


def causal_dwconv(x, buf, w, b=None):
    width = w.shape[0]
    t = x.shape[1]
    xp = jnp.concatenate([buf.astype(x.dtype), x], axis=1)
    out = xp[:, 0:t] * w[0]
    for k in range(1, width):
        out = out + xp[:, k:k + t] * w[k]
    if b is not None:
        out = out + b
    return out, xp[:, xp.shape[1] - (width - 1):]


def rglru(x, h0, wa, ba, wx, bx, lam):
    bsz, t, _ = x.shape
    xh = x.reshape(bsz, t, N_RNN_HEADS, RNN_HEAD_DIM)
    r = jax.nn.sigmoid(jnp.einsum('bthi,hij->bthj', xh, wa).reshape(bsz, t, D_RNN).astype(jnp.float32) + ba.astype(jnp.float32))
    i = jax.nn.sigmoid(jnp.einsum('bthi,hij->bthj', xh, wx).reshape(bsz, t, D_RNN).astype(jnp.float32) + bx.astype(jnp.float32))
    log_a = -LRU_C * r * jax.nn.softplus(-lam.astype(jnp.float32))
    a = jnp.exp(log_a)
    mult = jnp.sqrt(jnp.maximum(-jnp.expm1(2.0 * log_a), 1e-12))
    u = mult * i * x.astype(jnp.float32)
    u = jnp.concatenate([u[:, :1] + a[:, :1] * h0.astype(jnp.float32)[:, None], u[:, 1:]], axis=1)

    def combine(left, right):
        a1, b1 = left
        a2, b2 = right
        return a1 * a2, a2 * b1 + b2

    _, h = lax.associative_scan(combine, (a, u), axis=1)
    return h, h[:, -1]


def trunk(x, h_st, rconv_st, sconv_st,
          norm1_g, w_in, rnn_conv_w, rnn_conv_b, gate_a_w, gate_a_b, gate_x_w, gate_x_b, lru_lambda,
          w_branch_a, sc_conv_w, w_branch_b, w_out, norm2_g, w_ff_gate, w_ff_up, w_ff_down, final_norm_g):
    dt = x.dtype
    offs = list(np.cumsum(SPLITS)[:-1])
    hs, rcs, scs = [], [], []
    for l in range(DEPTH):
        u = rmsnorm(x, norm1_g[l])
        p = u @ w_in[l]
        xr, gr, bc, cc, hc, ga, gb = jnp.split(p, offs, axis=-1)
        xr, new_rc = causal_dwconv(xr, rconv_st[l], rnn_conv_w[l], rnn_conv_b[l])
        hseq, h_last = rglru(xr, h_st[l], gate_a_w[l], gate_a_b[l], gate_x_w[l], gate_x_b[l], lru_lambda[l])
        ya = hseq.astype(dt) * jax.nn.gelu(gr)
        vc, new_sc = causal_dwconv(cc * hc, sconv_st[l], sc_conv_w[l])
        yb = bc * vc
        m = jax.nn.sigmoid(ga) * (ya @ w_branch_a[l]) + jax.nn.sigmoid(gb) * (yb @ w_branch_b[l])
        x = x + m @ w_out[l]
        v = rmsnorm(x, norm2_g[l])
        x = x + (jax.nn.silu(v @ w_ff_gate[l]) * (v @ w_ff_up[l])) @ w_ff_down[l]
        hs.append(h_last.astype(dt))
        rcs.append(new_rc)
        scs.append(new_sc)
    return rmsnorm(x, final_norm_g), jnp.stack(hs), jnp.stack(rcs), jnp.stack(scs)


def setup_inputs(seed: int = 0) -> dict:
    key = jax.random.key(seed)
    ks = jax.random.split(key, 32)
    f = jnp.float32
    nrm = lambda k, shape, s: jax.random.normal(k, shape, f) * s
    a8 = jax.random.uniform(ks[10], (DEPTH, D_RNN), f, 0.9, 0.999)
    a_base = a8 ** (1.0 / LRU_C)
    lru_lambda = jnp.log(a_base) - jnp.log1p(-a_base)
    return {
        "x_prompt": nrm(ks[0], (BATCH, SEQ, D_MODEL), 1.0),
        "x_sample": nrm(ks[1], (DEC_BATCH, DEC_SEQ, D_MODEL), 1.0),
        "state_rnn_h": nrm(ks[2], (DEPTH, DEC_BATCH, D_RNN), 0.5),
        "state_rnn_conv": nrm(ks[3], (DEPTH, DEC_BATCH, RNN_CONV_W - 1, D_RNN), 1.0),
        "state_sc_conv": nrm(ks[4], (DEPTH, DEC_BATCH, SC_CONV_W - 1, D_CONV), 1.0),
        "meta_tokens": nrm(ks[5], (N_META, D_MODEL), 1.0),
        "norm1_g": 1.0 + nrm(ks[6], (DEPTH, D_MODEL), 0.02),
        "w_in": nrm(ks[7], (DEPTH, D_MODEL, D_IN), D_MODEL ** -0.5),
        "rnn_conv_w": nrm(ks[8], (DEPTH, RNN_CONV_W, D_RNN), RNN_CONV_W ** -0.5),
        "rnn_conv_b": nrm(ks[9], (DEPTH, D_RNN), 0.02),
        "gate_a_w": nrm(ks[11], (DEPTH, N_RNN_HEADS, RNN_HEAD_DIM, RNN_HEAD_DIM), RNN_HEAD_DIM ** -0.5),
        "gate_a_b": nrm(ks[12], (DEPTH, D_RNN), 0.02),
        "gate_x_w": nrm(ks[13], (DEPTH, N_RNN_HEADS, RNN_HEAD_DIM, RNN_HEAD_DIM), RNN_HEAD_DIM ** -0.5),
        "gate_x_b": nrm(ks[14], (DEPTH, D_RNN), 0.02),
        "lru_lambda": lru_lambda,
        "w_branch_a": nrm(ks[15], (DEPTH, D_RNN, D_MODEL), D_RNN ** -0.5),
        "sc_conv_w": nrm(ks[16], (DEPTH, SC_CONV_W, D_CONV), SC_CONV_W ** -0.5),
        "w_branch_b": nrm(ks[17], (DEPTH, D_CONV, D_MODEL), D_CONV ** -0.5),
        "w_out": nrm(ks[18], (DEPTH, D_MODEL, D_MODEL), D_MODEL ** -0.5),
        "norm2_g": 1.0 + nrm(ks[19], (DEPTH, D_MODEL), 0.02),
        "w_ff_gate": nrm(ks[20], (DEPTH, D_MODEL, D_FF), D_MODEL ** -0.5),
        "w_ff_up": nrm(ks[21], (DEPTH, D_MODEL, D_FF), D_MODEL ** -0.5),
        "w_ff_down": nrm(ks[22], (DEPTH, D_FF, D_MODEL), D_FF ** -0.5),
        "final_norm_g": 1.0 + nrm(ks[23], (D_MODEL,), 0.02),
    }


def reference(x_prompt, x_sample, state_rnn_h, state_rnn_conv, state_sc_conv, meta_tokens,
              norm1_g, w_in, rnn_conv_w, rnn_conv_b, gate_a_w, gate_a_b, gate_x_w, gate_x_b, lru_lambda,
              w_branch_a, sc_conv_w, w_branch_b, w_out, norm2_g, w_ff_gate, w_ff_up, w_ff_down, final_norm_g):
    weights = (norm1_g, w_in, rnn_conv_w, rnn_conv_b, gate_a_w, gate_a_b, gate_x_w, gate_x_b, lru_lambda,
               w_branch_a, sc_conv_w, w_branch_b, w_out, norm2_g, w_ff_gate, w_ff_up, w_ff_down, final_norm_g)
    bp = x_prompt.shape[0]
    dt = x_prompt.dtype
    meta = jnp.broadcast_to(meta_tokens.astype(dt)[None], (bp, N_META, D_MODEL))
    xp = jnp.concatenate([meta, x_prompt], axis=1)
    h0 = jnp.zeros((DEPTH, bp, D_RNN), dt)
    rc0 = jnp.zeros((DEPTH, bp, RNN_CONV_W - 1, D_RNN), dt)
    sc0 = jnp.zeros((DEPTH, bp, SC_CONV_W - 1, D_CONV), dt)
    yp, rnn_h_prompt, rnn_conv_prompt, sc_conv_prompt = trunk(xp, h0, rc0, sc0, *weights)
    y_prompt = yp[:, N_META:]
    y_sample, rnn_h_sample, rnn_conv_sample, sc_conv_sample = trunk(
        x_sample, state_rnn_h, state_rnn_conv, state_sc_conv, *weights)
    return (y_prompt, y_sample, rnn_h_prompt, rnn_conv_prompt, sc_conv_prompt,
            rnn_h_sample, rnn_conv_sample, sc_conv_sample)
```

```python
import functools
import math

import jax
import jax.numpy as jnp
from jax import lax
from jax.experimental import pallas as pl
from jax.experimental.pallas import tpu as pltpu

D_MODEL = 1024
DEPTH = 4
N_META = 16
N_RNN_HEADS = 16
RNN_HEAD_DIM = D_MODEL // N_RNN_HEADS
RNN_CONV_W = 4
SC_CONV_W = 3
LRU_C = 8.0
D_FF = 2816
EPS = 1e-6
N_SPLITS = 7
GATE_CHUNK = 256
N_GATE_CHUNKS = D_MODEL // GATE_CHUNK

V7X_VMEM_BYTES = 64 * 1024 * 1024
VMEM_LIMIT_BYTES = V7X_VMEM_BYTES - 8 * 1024 * 1024

F32 = jnp.float32
BF16 = jnp.bfloat16


def _sigmoid(x):
    return 0.5 * jnp.tanh(0.5 * x) + 0.5


def _gelu_tanh(x):
    c = math.sqrt(2.0 / math.pi)
    return x * (0.5 * (1.0 + jnp.tanh(c * (x + 0.044715 * (x * x * x)))))


def _softplus(x):
    return jnp.maximum(x, 0.0) + jnp.log1p(jnp.exp(-jnp.abs(x)))


def _rmsnorm(x, g):
    ms = jnp.mean(x * x, axis=-1, keepdims=True)
    return x * lax.rsqrt(ms + EPS) * g


def _dot(a, b):
    return jnp.dot(a, b, preferred_element_type=F32)


def _mixer_kernel(x_ref, h0_ref, rc0_ref, sc0_ref, g1_ref, win_ref, cw_ref, cb_ref, wgate_ref,
                  ba_ref, bx_ref, lam_ref, wba_ref, scw_ref, wbb_ref, wout_ref,
                  xo_ref, h_ref, rc_ref, sc_ref,
                  xrbuf, chbuf, a_buf, u_buf, hseq, *, n_seq, n_t):
    step = pl.program_id(0)
    rows = n_seq * n_t
    rc_rows = (RNN_CONV_W - 1) * n_seq
    sc_rows = (SC_CONV_W - 1) * n_seq

    @pl.when(step == 0)
    def _():
        h_ref[...] = h0_ref[...]
        xrbuf[0:rc_rows, :] = rc0_ref[...]
        chbuf[0:sc_rows, :] = sc0_ref[...]

    x = x_ref[...]
    u = _rmsnorm(x, g1_ref[...]).astype(BF16)

    def proj(k):
        return _dot(u, win_ref[:, k * D_MODEL:(k + 1) * D_MODEL])

    xrbuf[rc_rows:rc_rows + rows, :] = proj(0)
    xc = cb_ref[...] + xrbuf[0:rows, :] * cw_ref[0:1, :]
    for k in range(1, RNN_CONV_W):
        xc = xc + xrbuf[k * n_seq:k * n_seq + rows, :] * cw_ref[k:k + 1, :]
    xc_b = xc.astype(BF16)
    neg_c_sp = -LRU_C * _softplus(-lam_ref[...])
    for j in range(N_GATE_CHUNKS):
        cols = slice(j * GATE_CHUNK, (j + 1) * GATE_CHUNK)
        gates = _dot(xc_b[:, cols], wgate_ref[j])
        r = _sigmoid(gates[:, :GATE_CHUNK] + ba_ref[:, cols])
        i = _sigmoid(gates[:, GATE_CHUNK:] + bx_ref[:, cols])
        log_a = r * neg_c_sp[:, cols]
        a = jnp.exp(log_a)
        one_minus_a2 = -jnp.tanh(log_a) * (a * a + 1.0)
        mult = jnp.sqrt(jnp.maximum(one_minus_a2, 1e-12))
        a_buf[:, cols] = a
        u_buf[:, cols] = mult * i * xc[:, cols]

    if n_seq == 8:
        def scan_step(t, h):
            sl = pl.ds(pl.multiple_of(t * n_seq, n_seq), n_seq)
            h = a_buf[sl, :] * h + u_buf[sl, :]
            hseq[sl, :] = h
            return h
        h_ref[...] = lax.fori_loop(0, n_t, scan_step, h_ref[...], unroll=True)
    else:
        for t in range(n_t):
            sl = slice(t * n_seq, (t + 1) * n_seq)
            prev = h_ref[...] if t == 0 else hseq[(t - 1) * n_seq:t * n_seq, :]
            hseq[sl, :] = a_buf[sl, :] * prev + u_buf[sl, :]
        h_ref[...] = hseq[(n_t - 1) * n_seq:n_t * n_seq, :]

    ya = (hseq[...] * _gelu_tanh(proj(1))).astype(BF16)
    m = _sigmoid(proj(5)) * _dot(ya, wba_ref[...])

    chbuf[sc_rows:sc_rows + rows, :] = proj(3) * proj(4)
    vc = chbuf[0:rows, :] * scw_ref[0:1, :]
    for k in range(1, SC_CONV_W):
        vc = vc + chbuf[k * n_seq:k * n_seq + rows, :] * scw_ref[k:k + 1, :]
    yb = (proj(2) * vc).astype(BF16)
    m = m + _sigmoid(proj(6)) * _dot(yb, wbb_ref[...])

    xo_ref[...] = x + _dot(m.astype(BF16), wout_ref[...])

    new_rc = xrbuf[rows:rows + rc_rows, :]
    new_sc = chbuf[rows:rows + sc_rows, :]
    xrbuf[0:rc_rows, :] = new_rc
    chbuf[0:sc_rows, :] = new_sc

    @pl.when(step == pl.num_programs(0) - 1)
    def _():
        rc_ref[...] = new_rc
        sc_ref[...] = new_sc


def _ffn_kernel(x_ref, g2_ref, wg_ref, wu_ref, wd_ref, gf_ref, o_ref, *, final_norm):
    x = x_ref[...]
    v = _rmsnorm(x, g2_ref[...]).astype(BF16)
    gate = _dot(v, wg_ref[...])
    half = 0.5 * gate
    hid = ((half * jnp.tanh(half) + half) * _dot(v, wu_ref[...])).astype(BF16)
    y = x + _dot(hid, wd_ref[...])
    if final_norm:
        y = _rmsnorm(y, gf_ref[...])
    o_ref[...] = y


def _resident(shape, index):
    return pl.BlockSpec(shape, index, pipeline_mode=pl.Buffered(1))


def _layer_weight_spec(arr, layer):
    zeros = (0,) * (arr.ndim - 1)
    return _resident((None,) + arr.shape[1:], lambda i: (layer,) + zeros)


def _mixer_call(x, h0, rc0, sc0, w, layer, *, n_seq, n_t):
    rows = n_seq * n_t
    n_rows = x.shape[0]
    assert n_rows % rows == 0
    rc_rows = (RNN_CONV_W - 1) * n_seq
    sc_rows = (SC_CONV_W - 1) * n_seq
    row_spec = pl.BlockSpec((rows, D_MODEL), lambda i: (i, 0))
    state_spec = lambda r: _resident((r, D_MODEL), lambda i: (0, 0))
    names = ("norm1_g", "w_in", "rnn_conv_w", "rnn_conv_b", "w_gate", "gate_a_b", "gate_x_b",
             "lru_lambda", "w_branch_a", "sc_conv_w", "w_branch_b", "w_out")
    return pl.pallas_call(
        functools.partial(_mixer_kernel, n_seq=n_seq, n_t=n_t),
        grid=(n_rows // rows,),
        in_specs=[row_spec, state_spec(n_seq), state_spec(rc_rows), state_spec(sc_rows)]
        + [_layer_weight_spec(w[n], layer) for n in names],
        out_specs=[row_spec,
                   pl.BlockSpec((n_seq, D_MODEL), lambda i: (0, 0)),
                   pl.BlockSpec((rc_rows, D_MODEL), lambda i: (0, 0)),
                   pl.BlockSpec((sc_rows, D_MODEL), lambda i: (0, 0))],
        out_shape=[jax.ShapeDtypeStruct((n_rows, D_MODEL), F32),
                   jax.ShapeDtypeStruct((n_seq, D_MODEL), F32),
                   jax.ShapeDtypeStruct((rc_rows, D_MODEL), F32),
                   jax.ShapeDtypeStruct((sc_rows, D_MODEL), F32)],
        scratch_shapes=[pltpu.VMEM((rc_rows + rows, D_MODEL), F32),
                        pltpu.VMEM((sc_rows + rows, D_MODEL), F32),
                        pltpu.VMEM((rows, D_MODEL), F32),
                        pltpu.VMEM((rows, D_MODEL), F32),
                        pltpu.VMEM((rows, D_MODEL), F32)],
        compiler_params=pltpu.CompilerParams(dimension_semantics=("arbitrary",),
                                             vmem_limit_bytes=VMEM_LIMIT_BYTES),
        name=f"mixer_s{n_seq}",
    )(x, h0, rc0, sc0, *[w[n] for n in names])


def _ffn_call(x, w, layer, *, rows, final_norm):
    n_rows = x.shape[0]
    assert n_rows % rows == 0
    row_spec = pl.BlockSpec((rows, D_MODEL), lambda i: (i, 0))
    names = ("norm2_g", "w_ff_gate", "w_ff_up", "w_ff_down")
    gf = w["final_norm_g"]
    return pl.pallas_call(
        functools.partial(_ffn_kernel, final_norm=final_norm),
        grid=(n_rows // rows,),
        in_specs=[row_spec] + [_layer_weight_spec(w[n], layer) for n in names]
        + [_resident(gf.shape, lambda i: (0, 0))],
        out_specs=row_spec,
        out_shape=jax.ShapeDtypeStruct((n_rows, D_MODEL), F32),
        compiler_params=pltpu.CompilerParams(dimension_semantics=("arbitrary",),
                                             vmem_limit_bytes=VMEM_LIMIT_BYTES),
        name=f"ffn_r{rows}",
    )(x, *[w[n] for n in names], gf)


def _block_diag_gates(gate_a_w, gate_x_w):
    hpc = GATE_CHUNK // RNN_HEAD_DIM
    eye = jnp.eye(hpc, dtype=gate_a_w.dtype)

    def bd(wt):
        wt = wt.reshape(DEPTH, N_GATE_CHUNKS, hpc, RNN_HEAD_DIM, RNN_HEAD_DIM)
        full = wt[:, :, :, :, None, :] * eye[None, None, :, None, :, None]
        return full.reshape(DEPTH, N_GATE_CHUNKS, GATE_CHUNK, GATE_CHUNK)

    return jnp.concatenate([bd(gate_a_w), bd(gate_x_w)], axis=-1).astype(BF16)


def _time_major(state):
    return jnp.swapaxes(state, 0, 1).reshape(-1, D_MODEL)


def _seq_major(state, n_seq):
    return jnp.swapaxes(state.reshape(-1, n_seq, D_MODEL), 0, 1)


def _trunk(x, h_st, rc_st, sc_st, w, *, n_seq, n_t, ffn_rows):
    hs, rcs, scs = [], [], []
    for layer in range(DEPTH):
        x, h, rc, sc = _mixer_call(x, h_st[layer], _time_major(rc_st[layer]), _time_major(sc_st[layer]),
                                   w, layer, n_seq=n_seq, n_t=n_t)
        x = _ffn_call(x, w, layer, rows=ffn_rows, final_norm=(layer == DEPTH - 1))
        hs.append(h)
        rcs.append(_seq_major(rc, n_seq))
        scs.append(_seq_major(sc, n_seq))
    return x, jnp.stack(hs), jnp.stack(rcs), jnp.stack(scs)


PROMPT_N_T = 48
PROMPT_FFN_ROWS = 384


def kernel(x_prompt, x_sample, state_rnn_h, state_rnn_conv, state_sc_conv, meta_tokens, norm1_g, w_in, rnn_conv_w, rnn_conv_b, gate_a_w, gate_a_b, gate_x_w, gate_x_b, lru_lambda, w_branch_a, sc_conv_w, w_branch_b, w_out, norm2_g, w_ff_gate, w_ff_up, w_ff_down, final_norm_g):
    row = lambda p: p.reshape(DEPTH, 1, D_MODEL)
    w = {
        "norm1_g": row(norm1_g), "w_in": w_in.astype(BF16),
        "rnn_conv_w": rnn_conv_w, "rnn_conv_b": row(rnn_conv_b),
        "w_gate": _block_diag_gates(gate_a_w, gate_x_w),
        "gate_a_b": row(gate_a_b), "gate_x_b": row(gate_x_b), "lru_lambda": row(lru_lambda),
        "w_branch_a": w_branch_a.astype(BF16), "sc_conv_w": sc_conv_w,
        "w_branch_b": w_branch_b.astype(BF16), "w_out": w_out.astype(BF16),
        "norm2_g": row(norm2_g), "w_ff_gate": w_ff_gate.astype(BF16),
        "w_ff_up": w_ff_up.astype(BF16), "w_ff_down": w_ff_down.astype(BF16),
        "final_norm_g": final_norm_g.reshape(1, D_MODEL),
    }
    dt = x_prompt.dtype

    bp, seq, _ = x_prompt.shape
    t_p = N_META + seq
    meta = jnp.broadcast_to(meta_tokens.astype(dt)[:, None, :], (N_META, bp, D_MODEL))
    xp = jnp.concatenate([meta, jnp.swapaxes(x_prompt, 0, 1)], axis=0).reshape(t_p * bp, D_MODEL)
    yp, rnn_h_p, rnn_conv_p, sc_conv_p = _trunk(
        xp, jnp.zeros((DEPTH, bp, D_MODEL), dt), jnp.zeros((DEPTH, bp, RNN_CONV_W - 1, D_MODEL), dt),
        jnp.zeros((DEPTH, bp, SC_CONV_W - 1, D_MODEL), dt), w,
        n_seq=bp, n_t=PROMPT_N_T, ffn_rows=PROMPT_FFN_ROWS)
    y_prompt = jnp.swapaxes(yp.reshape(t_p, bp, D_MODEL)[N_META:], 0, 1)

    bs, t_s, _ = x_sample.shape
    xs = jnp.swapaxes(x_sample, 0, 1).reshape(t_s * bs, D_MODEL)
    ys, rnn_h_s, rnn_conv_s, sc_conv_s = _trunk(
        xs, state_rnn_h, state_rnn_conv, state_sc_conv, w, n_seq=bs, n_t=t_s, ffn_rows=t_s * bs)
    y_sample = jnp.swapaxes(ys.reshape(t_s, bs, D_MODEL), 0, 1)

    return (y_prompt, y_sample, rnn_h_p, rnn_conv_p, sc_conv_p, rnn_h_s, rnn_conv_s, sc_conv_s)
```

```python
import functools
import math

import jax
import jax.numpy as jnp
from jax import lax
from jax.experimental import pallas as pl
from jax.experimental.pallas import tpu as pltpu

D_MODEL = 1024
DEPTH = 4
N_META = 16
N_RNN_HEADS = 16
RNN_HEAD_DIM = D_MODEL // N_RNN_HEADS
RNN_CONV_W = 4
SC_CONV_W = 3
LRU_C = 8.0
D_FF = 2816
EPS = 1e-6
N_SPLITS = 7
GATE_CHUNK = 256
N_GATE_CHUNKS = D_MODEL // GATE_CHUNK

V7X_VMEM_BYTES = 64 * 1024 * 1024
VMEM_LIMIT_BYTES = V7X_VMEM_BYTES - 8 * 1024 * 1024

F32 = jnp.float32
BF16 = jnp.bfloat16


def _sigmoid(x):
    return 0.5 * jnp.tanh(0.5 * x) + 0.5


def _gelu_tanh(x):
    c = math.sqrt(2.0 / math.pi)
    return x * (0.5 * (1.0 + jnp.tanh(c * (x + 0.044715 * (x * x * x)))))


def _softplus(x):
    return jnp.maximum(x, 0.0) + jnp.log1p(jnp.exp(-jnp.abs(x)))


def _rmsnorm(x, g):
    ms = jnp.mean(x * x, axis=-1, keepdims=True)
    return x * lax.rsqrt(ms + EPS) * g


def _dot(a, b):
    return jnp.dot(a, b, preferred_element_type=F32)


def _mixer_kernel(x_ref, h0_ref, rc0_ref, sc0_ref, g1_ref, win_ref, cw_ref, cb_ref, wgate_ref,
                  ba_ref, bx_ref, lam_ref, wba_ref, scw_ref, wbb_ref, wout_ref,
                  xo_ref, h_ref, rc_ref, sc_ref,
                  xrbuf, chbuf, a_buf, u_buf, hseq, *, n_seq, n_t):
    step = pl.program_id(0)
    rows = n_seq * n_t
    rc_rows = (RNN_CONV_W - 1) * n_seq
    sc_rows = (SC_CONV_W - 1) * n_seq

    @pl.when(step == 0)
    def _():
        h_ref[...] = h0_ref[...]
        xrbuf[0:rc_rows, :] = rc0_ref[...]
        chbuf[0:sc_rows, :] = sc0_ref[...]

    x = x_ref[...]
    u = _rmsnorm(x, g1_ref[...]).astype(BF16)

    def proj(k):
        return _dot(u, win_ref[:, k * D_MODEL:(k + 1) * D_MODEL])

    xrbuf[rc_rows:rc_rows + rows, :] = proj(0)
    xc = cb_ref[...] + xrbuf[0:rows, :] * cw_ref[0:1, :]
    for k in range(1, RNN_CONV_W):
        xc = xc + xrbuf[k * n_seq:k * n_seq + rows, :] * cw_ref[k:k + 1, :]
    xc_b = xc.astype(BF16)
    neg_c_sp = -LRU_C * _softplus(-lam_ref[...])
    for j in range(N_GATE_CHUNKS):
        cols = slice(j * GATE_CHUNK, (j + 1) * GATE_CHUNK)
        gates = _dot(xc_b[:, cols], wgate_ref[j])
        r = _sigmoid(gates[:, :GATE_CHUNK] + ba_ref[:, cols])
        i = _sigmoid(gates[:, GATE_CHUNK:] + bx_ref[:, cols])
        log_a = r * neg_c_sp[:, cols]
        a = jnp.exp(log_a)
        one_minus_a2 = -jnp.tanh(log_a) * (a * a + 1.0)
        mult = jnp.sqrt(jnp.maximum(one_minus_a2, 1e-12))
        a_buf[:, cols] = a
        u_buf[:, cols] = mult * i * xc[:, cols]

    if n_seq == 8:
        def scan_step(t, h):
            sl = pl.ds(pl.multiple_of(t * n_seq, n_seq), n_seq)
            h = a_buf[sl, :] * h + u_buf[sl, :]
            hseq[sl, :] = h
            return h
        h_ref[...] = lax.fori_loop(0, n_t, scan_step, h_ref[...], unroll=True)
    else:
        for t in range(n_t):
            sl = slice(t * n_seq, (t + 1) * n_seq)
            prev = h_ref[...] if t == 0 else hseq[(t - 1) * n_seq:t * n_seq, :]
            hseq[sl, :] = a_buf[sl, :] * prev + u_buf[sl, :]
        h_ref[...] = hseq[(n_t - 1) * n_seq:n_t * n_seq, :]

    ya = (hseq[...] * _gelu_tanh(proj(1))).astype(BF16)
    m = _sigmoid(proj(5)) * _dot(ya, wba_ref[...])

    chbuf[sc_rows:sc_rows + rows, :] = proj(3) * proj(4)
    vc = chbuf[0:rows, :] * scw_ref[0:1, :]
    for k in range(1, SC_CONV_W):
        vc = vc + chbuf[k * n_seq:k * n_seq + rows, :] * scw_ref[k:k + 1, :]
    yb = (proj(2) * vc).astype(BF16)
    m = m + _sigmoid(proj(6)) * _dot(yb, wbb_ref[...])

    xo_ref[...] = x + _dot(m.astype(BF16), wout_ref[...])

    new_rc = xrbuf[rows:rows + rc_rows, :]
    new_sc = chbuf[rows:rows + sc_rows, :]
    xrbuf[0:rc_rows, :] = new_rc
    chbuf[0:sc_rows, :] = new_sc

    @pl.when(step == pl.num_programs(0) - 1)
    def _():
        rc_ref[...] = new_rc
        sc_ref[...] = new_sc


def _ffn_kernel(x_ref, g2_ref, wg_ref, wu_ref, wd_ref, gf_ref, o_ref, *, final_norm):
    x = x_ref[...]
    v = _rmsnorm(x, g2_ref[...]).astype(BF16)
    gate = _dot(v, wg_ref[...])
    half = 0.5 * gate
    hid = ((half * jnp.tanh(half) + half) * _dot(v, wu_ref[...])).astype(BF16)
    y = x + _dot(hid, wd_ref[...])
    if final_norm:
        y = _rmsnorm(y, gf_ref[...])
    o_ref[...] = y


def _resident(shape, index):
    return pl.BlockSpec(shape, index, pipeline_mode=pl.Buffered(1))


def _layer_weight_spec(arr, layer):
    zeros = (0,) * (arr.ndim - 1)
    return _resident((None,) + arr.shape[1:], lambda i: (layer,) + zeros)


def _mixer_call(x, h0, rc0, sc0, w, layer, *, n_seq, n_t):
    rows = n_seq * n_t
    n_rows = x.shape[0]
    assert n_rows % rows == 0
    rc_rows = (RNN_CONV_W - 1) * n_seq
    sc_rows = (SC_CONV_W - 1) * n_seq
    row_spec = pl.BlockSpec((rows, D_MODEL), lambda i: (i, 0))
    state_spec = lambda r: _resident((r, D_MODEL), lambda i: (0, 0))
    names = ("norm1_g", "w_in", "rnn_conv_w", "rnn_conv_b", "w_gate", "gate_a_b", "gate_x_b",
             "lru_lambda", "w_branch_a", "sc_conv_w", "w_branch_b", "w_out")
    return pl.pallas_call(
        functools.partial(_mixer_kernel, n_seq=n_seq, n_t=n_t),
        grid=(n_rows // rows,),
        in_specs=[row_spec, state_spec(n_seq), state_spec(rc_rows), state_spec(sc_rows)]
        + [_layer_weight_spec(w[n], layer) for n in names],
        out_specs=[row_spec,
                   pl.BlockSpec((n_seq, D_MODEL), lambda i: (0, 0)),
                   pl.BlockSpec((rc_rows, D_MODEL), lambda i: (0, 0)),
                   pl.BlockSpec((sc_rows, D_MODEL), lambda i: (0, 0))],
        out_shape=[jax.ShapeDtypeStruct((n_rows, D_MODEL), F32),
                   jax.ShapeDtypeStruct((n_seq, D_MODEL), F32),
                   jax.ShapeDtypeStruct((rc_rows, D_MODEL), F32),
                   jax.ShapeDtypeStruct((sc_rows, D_MODEL), F32)],
        scratch_shapes=[pltpu.VMEM((rc_rows + rows, D_MODEL), F32),
                        pltpu.VMEM((sc_rows + rows, D_MODEL), F32),
                        pltpu.VMEM((rows, D_MODEL), F32),
                        pltpu.VMEM((rows, D_MODEL), F32),
                        pltpu.VMEM((rows, D_MODEL), F32)],
        compiler_params=pltpu.CompilerParams(dimension_semantics=("arbitrary",),
                                             vmem_limit_bytes=VMEM_LIMIT_BYTES),
        name=f"mixer_s{n_seq}",
    )(x, h0, rc0, sc0, *[w[n] for n in names])


def _ffn_call(x, w, layer, *, rows, final_norm):
    n_rows = x.shape[0]
    assert n_rows % rows == 0
    row_spec = pl.BlockSpec((rows, D_MODEL), lambda i: (i, 0))
    names = ("norm2_g", "w_ff_gate", "w_ff_up", "w_ff_down")
    gf = w["final_norm_g"]
    return pl.pallas_call(
        functools.partial(_ffn_kernel, final_norm=final_norm),
        grid=(n_rows // rows,),
        in_specs=[row_spec] + [_layer_weight_spec(w[n], layer) for n in names]
        + [_resident(gf.shape, lambda i: (0, 0))],
        out_specs=row_spec,
        out_shape=jax.ShapeDtypeStruct((n_rows, D_MODEL), F32),
        compiler_params=pltpu.CompilerParams(dimension_semantics=("arbitrary",),
                                             vmem_limit_bytes=VMEM_LIMIT_BYTES),
        name=f"ffn_r{rows}",
    )(x, *[w[n] for n in names], gf)


def _block_diag_gates(gate_a_w, gate_x_w):
    hpc = GATE_CHUNK // RNN_HEAD_DIM
    eye = jnp.eye(hpc, dtype=gate_a_w.dtype)

    def bd(wt):
        wt = wt.reshape(DEPTH, N_GATE_CHUNKS, hpc, RNN_HEAD_DIM, RNN_HEAD_DIM)
        full = wt[:, :, :, :, None, :] * eye[None, None, :, None, :, None]
        return full.reshape(DEPTH, N_GATE_CHUNKS, GATE_CHUNK, GATE_CHUNK)

    return jnp.concatenate([bd(gate_a_w), bd(gate_x_w)], axis=-1).astype(BF16)


def _time_major(state):
    return jnp.swapaxes(state, 0, 1).reshape(-1, D_MODEL)


def _seq_major(state, n_seq):
    return jnp.swapaxes(state.reshape(-1, n_seq, D_MODEL), 0, 1)


def _trunk(x, h_st, rc_st, sc_st, w, *, n_seq, n_t, ffn_rows):
    hs, rcs, scs = [], [], []
    for layer in range(DEPTH):
        x, h, rc, sc = _mixer_call(x, h_st[layer], _time_major(rc_st[layer]), _time_major(sc_st[layer]),
                                   w, layer, n_seq=n_seq, n_t=n_t)
        x = _ffn_call(x, w, layer, rows=ffn_rows, final_norm=(layer == DEPTH - 1))
        hs.append(h)
        rcs.append(_seq_major(rc, n_seq))
        scs.append(_seq_major(sc, n_seq))
    return x, jnp.stack(hs), jnp.stack(rcs), jnp.stack(scs)


PROMPT_N_T = 86
PROMPT_FFN_ROWS = 688


def kernel(x_prompt, x_sample, state_rnn_h, state_rnn_conv, state_sc_conv, meta_tokens, norm1_g, w_in, rnn_conv_w, rnn_conv_b, gate_a_w, gate_a_b, gate_x_w, gate_x_b, lru_lambda, w_branch_a, sc_conv_w, w_branch_b, w_out, norm2_g, w_ff_gate, w_ff_up, w_ff_down, final_norm_g):
    row = lambda p: p.reshape(DEPTH, 1, D_MODEL)
    w = {
        "norm1_g": row(norm1_g), "w_in": w_in.astype(BF16),
        "rnn_conv_w": rnn_conv_w, "rnn_conv_b": row(rnn_conv_b),
        "w_gate": _block_diag_gates(gate_a_w, gate_x_w),
        "gate_a_b": row(gate_a_b), "gate_x_b": row(gate_x_b), "lru_lambda": row(lru_lambda),
        "w_branch_a": w_branch_a.astype(BF16), "sc_conv_w": sc_conv_w,
        "w_branch_b": w_branch_b.astype(BF16), "w_out": w_out.astype(BF16),
        "norm2_g": row(norm2_g), "w_ff_gate": w_ff_gate.astype(BF16),
        "w_ff_up": w_ff_up.astype(BF16), "w_ff_down": w_ff_down.astype(BF16),
        "final_norm_g": final_norm_g.reshape(1, D_MODEL),
    }
    dt = x_prompt.dtype

    bp, seq, _ = x_prompt.shape
    t_p = N_META + seq
    meta = jnp.broadcast_to(meta_tokens.astype(dt)[:, None, :], (N_META, bp, D_MODEL))
    xp = jnp.concatenate([meta, jnp.swapaxes(x_prompt, 0, 1)], axis=0).reshape(t_p * bp, D_MODEL)
    yp, rnn_h_p, rnn_conv_p, sc_conv_p = _trunk(
        xp, jnp.zeros((DEPTH, bp, D_MODEL), dt), jnp.zeros((DEPTH, bp, RNN_CONV_W - 1, D_MODEL), dt),
        jnp.zeros((DEPTH, bp, SC_CONV_W - 1, D_MODEL), dt), w,
        n_seq=bp, n_t=PROMPT_N_T, ffn_rows=PROMPT_FFN_ROWS)
    y_prompt = jnp.swapaxes(yp.reshape(t_p, bp, D_MODEL), 0, 1)[:, N_META:]

    bs, t_s, _ = x_sample.shape
    xs = jnp.swapaxes(x_sample, 0, 1).reshape(t_s * bs, D_MODEL)
    ys, rnn_h_s, rnn_conv_s, sc_conv_s = _trunk(
        xs, state_rnn_h, state_rnn_conv, state_sc_conv, w, n_seq=bs, n_t=t_s, ffn_rows=t_s * bs)
    y_sample = jnp.swapaxes(ys.reshape(t_s, bs, D_MODEL), 0, 1)

    return (y_prompt, y_sample, rnn_h_p, rnn_conv_p, sc_conv_p, rnn_h_s, rnn_conv_s, sc_conv_s)
```

```python
import functools
import math

import jax
import jax.numpy as jnp
from jax import lax
from jax.experimental import pallas as pl
from jax.experimental.pallas import tpu as pltpu

D_MODEL = 1024
DEPTH = 4
N_META = 16
N_RNN_HEADS = 16
RNN_HEAD_DIM = D_MODEL // N_RNN_HEADS
RNN_CONV_W = 4
SC_CONV_W = 3
LRU_C = 8.0
D_FF = 2816
EPS = 1e-6
N_SPLITS = 7
GATE_CHUNK = 256
N_GATE_CHUNKS = D_MODEL // GATE_CHUNK
SUBLANES = 8

V7X_VMEM_BYTES = 64 * 1024 * 1024
VMEM_LIMIT_BYTES = V7X_VMEM_BYTES - 8 * 1024 * 1024

F32 = jnp.float32
BF16 = jnp.bfloat16


def _sigmoid(x):
    return 0.5 * jnp.tanh(0.5 * x) + 0.5


def _gelu_tanh(x):
    c = math.sqrt(2.0 / math.pi)
    return x * (0.5 * (1.0 + jnp.tanh(c * (x + 0.044715 * (x * x * x)))))


def _softplus(x):
    return jnp.maximum(x, 0.0) + jnp.log1p(jnp.exp(-jnp.abs(x)))


def _rmsnorm(x, g):
    ms = jnp.mean(x * x, axis=-1, keepdims=True)
    return x * lax.rsqrt(ms + EPS) * g


def _dot(a, b):
    return jnp.dot(a, b, preferred_element_type=F32)


def _mixer_body(x, step, h0_ref, rc0_ref, sc0_ref, g1_ref, win_ref, cw_ref, cb_ref, wgate_ref,
                ba_ref, bx_ref, lam_ref, wba_ref, scw_ref, wbb_ref, wout_ref,
                xo_ref, h_ref, rc_ref, sc_ref,
                xrbuf, chbuf, a_buf, u_buf, hseq, *, n_seq, n_t):
    rows = n_seq * n_t
    rc_rows = (RNN_CONV_W - 1) * n_seq
    sc_rows = (SC_CONV_W - 1) * n_seq

    @pl.when(step == 0)
    def _():
        h_ref[...] = h0_ref[...]
        xrbuf[0:rc_rows, :] = rc0_ref[...]
        chbuf[0:sc_rows, :] = sc0_ref[...]

    u = _rmsnorm(x, g1_ref[...]).astype(BF16)

    def proj(k):
        return _dot(u, win_ref[:, k * D_MODEL:(k + 1) * D_MODEL])

    xrbuf[rc_rows:rc_rows + rows, :] = proj(0)
    xc = cb_ref[...] + xrbuf[0:rows, :] * cw_ref[0:1, :]
    for k in range(1, RNN_CONV_W):
        xc = xc + xrbuf[k * n_seq:k * n_seq + rows, :] * cw_ref[k:k + 1, :]
    xc_b = xc.astype(BF16)
    neg_c_sp = -LRU_C * _softplus(-lam_ref[...])
    for j in range(N_GATE_CHUNKS):
        cols = slice(j * GATE_CHUNK, (j + 1) * GATE_CHUNK)
        gates = _dot(xc_b[:, cols], wgate_ref[j])
        r = _sigmoid(gates[:, :GATE_CHUNK] + ba_ref[:, cols])
        i = _sigmoid(gates[:, GATE_CHUNK:] + bx_ref[:, cols])
        log_a = r * neg_c_sp[:, cols]
        a = jnp.exp(log_a)
        one_minus_a2 = -jnp.tanh(log_a) * (a * a + 1.0)
        mult = jnp.sqrt(jnp.maximum(one_minus_a2, 1e-12))
        a_buf[:, cols] = a
        u_buf[:, cols] = mult * i * xc[:, cols]

    if n_seq == SUBLANES:
        def scan_step(t, h):
            sl = pl.ds(pl.multiple_of(t * n_seq, n_seq), n_seq)
            h = a_buf[sl, :] * h + u_buf[sl, :]
            hseq[sl, :] = h
            return h
        h_ref[...] = lax.fori_loop(0, n_t, scan_step, h_ref[...], unroll=True)
    else:
        for t in range(n_t):
            sl = slice(t * n_seq, (t + 1) * n_seq)
            prev = h_ref[...] if t == 0 else hseq[(t - 1) * n_seq:t * n_seq, :]
            hseq[sl, :] = a_buf[sl, :] * prev + u_buf[sl, :]
        h_ref[...] = hseq[(n_t - 1) * n_seq:n_t * n_seq, :]

    ya = (hseq[...] * _gelu_tanh(proj(1))).astype(BF16)
    m = _sigmoid(proj(5)) * _dot(ya, wba_ref[...])

    chbuf[sc_rows:sc_rows + rows, :] = proj(3) * proj(4)
    vc = chbuf[0:rows, :] * scw_ref[0:1, :]
    for k in range(1, SC_CONV_W):
        vc = vc + chbuf[k * n_seq:k * n_seq + rows, :] * scw_ref[k:k + 1, :]
    yb = (proj(2) * vc).astype(BF16)
    m = m + _sigmoid(proj(6)) * _dot(yb, wbb_ref[...])

    xo_ref[...] = x + _dot(m.astype(BF16), wout_ref[...])

    new_rc = xrbuf[rows:rows + rc_rows, :]
    new_sc = chbuf[rows:rows + sc_rows, :]
    xrbuf[0:rc_rows, :] = new_rc
    chbuf[0:sc_rows, :] = new_sc

    @pl.when(step == pl.num_programs(0) - 1)
    def _():
        rc_ref[...] = new_rc
        sc_ref[...] = new_sc


def _mixer_kernel(x_ref, *refs, n_seq, n_t):
    _mixer_body(x_ref[...], pl.program_id(0), *refs, n_seq=n_seq, n_t=n_t)


def _seq_major_tile_copies(seq_hbm, tile_buf, sem, tile, slot, *, n_seq, n_t, to_hbm):
    copies = []
    for b in range(n_seq):
        if isinstance(tile, int):
            assert tile == 0
            hbm = seq_hbm.at[b, pl.ds(0, n_t - N_META), :]
            vmem = tile_buf.at[slot, pl.ds(N_META, n_t - N_META), b, :]
        else:
            t0 = pl.multiple_of(tile * n_t - N_META, SUBLANES)
            hbm = seq_hbm.at[b, pl.ds(t0, n_t), :]
            vmem = tile_buf.at[slot, :, b, :]
        src, dst = (vmem, hbm) if to_hbm else (hbm, vmem)
        copies.append(pltpu.make_async_copy(src, dst, sem.at[slot]))
    return copies


def _mixer_from_seq_major_kernel(xs_hbm, meta_ref, *refs, n_seq, n_t):
    *refs, xin, sem = refs
    step = pl.program_id(0)
    n_steps = pl.num_programs(0)
    slot = step % 2
    copies = functools.partial(_seq_major_tile_copies, xs_hbm, xin, sem, n_seq=n_seq, n_t=n_t, to_hbm=False)

    @pl.when(step == 0)
    def _():
        for c in copies(0, 0):
            c.start()
        xin[0, 0:N_META, :, :] = jnp.broadcast_to(meta_ref[...][:, None, :], (N_META, n_seq, D_MODEL))

        @pl.when(n_steps > 1)
        def _():
            for c in copies(step + 1, 1):
                c.start()
        for c in copies(0, 0):
            c.wait()

    @pl.when(step > 0)
    def _():
        @pl.when(step + 1 < n_steps)
        def _():
            for c in copies(step + 1, 1 - slot):
                c.start()
        for c in copies(step, slot):
            c.wait()

    x = xin[slot].reshape(n_t * n_seq, D_MODEL)
    _mixer_body(x, step, *refs, n_seq=n_seq, n_t=n_t)


def _ffn_body(x, g2_ref, wg_ref, wu_ref, wd_ref, gf_ref, *, final_norm):
    v = _rmsnorm(x, g2_ref[...]).astype(BF16)
    gate = _dot(v, wg_ref[...])
    half = 0.5 * gate
    hid = ((half * jnp.tanh(half) + half) * _dot(v, wu_ref[...])).astype(BF16)
    y = x + _dot(hid, wd_ref[...])
    if final_norm:
        y = _rmsnorm(y, gf_ref[...])
    return y


def _ffn_kernel(x_ref, g2_ref, wg_ref, wu_ref, wd_ref, gf_ref, o_ref, *, final_norm):
    o_ref[...] = _ffn_body(x_ref[...], g2_ref, wg_ref, wu_ref, wd_ref, gf_ref, final_norm=final_norm)


def _ffn_to_seq_major_kernel(x_ref, g2_ref, wg_ref, wu_ref, wd_ref, gf_ref, ys_hbm, yout, sem, *, n_seq, n_t):
    step = pl.program_id(0)
    n_steps = pl.num_programs(0)
    slot = step % 2
    copies = functools.partial(_seq_major_tile_copies, ys_hbm, yout, sem, n_seq=n_seq, n_t=n_t, to_hbm=True)
    y = _ffn_body(x_ref[...], g2_ref, wg_ref, wu_ref, wd_ref, gf_ref, final_norm=True)

    @pl.when(step == 1)
    def _():
        for c in copies(0, 0):
            c.wait()

    @pl.when(step > 1)
    def _():
        for c in copies(step - 1, 1 - slot):
            c.wait()

    yout[slot] = y.reshape(n_t, n_seq, D_MODEL)

    @pl.when(step == 0)
    def _():
        for c in copies(0, 0):
            c.start()

        @pl.when(n_steps == 1)
        def _():
            for c in copies(0, 0):
                c.wait()

    @pl.when(step > 0)
    def _():
        for c in copies(step, slot):
            c.start()

        @pl.when(step == n_steps - 1)
        def _():
            for c in copies(step, slot):
                c.wait()


def _resident(shape, index):
    return pl.BlockSpec(shape, index, pipeline_mode=pl.Buffered(1))


def _layer_weight_spec(arr, layer):
    zeros = (0,) * (arr.ndim - 1)
    return _resident((None,) + arr.shape[1:], lambda i: (layer,) + zeros)


_MIXER_WEIGHTS = ("norm1_g", "w_in", "rnn_conv_w", "rnn_conv_b", "w_gate", "gate_a_b", "gate_x_b",
                  "lru_lambda", "w_branch_a", "sc_conv_w", "w_branch_b", "w_out")
_FFN_WEIGHTS = ("norm2_g", "w_ff_gate", "w_ff_up", "w_ff_down")
_COMPILER_PARAMS = pltpu.CompilerParams(dimension_semantics=("arbitrary",), vmem_limit_bytes=VMEM_LIMIT_BYTES)


def _mixer_call(x, h0, rc0, sc0, w, layer, *, n_seq, n_t, meta=None):
    rows = n_seq * n_t
    rc_rows = (RNN_CONV_W - 1) * n_seq
    sc_rows = (SC_CONV_W - 1) * n_seq
    row_spec = pl.BlockSpec((rows, D_MODEL), lambda i: (i, 0))
    state_spec = lambda r: _resident((r, D_MODEL), lambda i: (0, 0))
    scratch = [pltpu.VMEM((rc_rows + rows, D_MODEL), F32),
               pltpu.VMEM((sc_rows + rows, D_MODEL), F32),
               pltpu.VMEM((rows, D_MODEL), F32),
               pltpu.VMEM((rows, D_MODEL), F32),
               pltpu.VMEM((rows, D_MODEL), F32)]
    if meta is None:
        body, x_args, n_rows = _mixer_kernel, (x,), x.shape[0]
        x_specs = [row_spec]
    else:
        assert n_t % SUBLANES == 0 and n_t > N_META and n_seq == SUBLANES
        body, x_args, n_rows = _mixer_from_seq_major_kernel, (x, meta), (x.shape[1] + N_META) * n_seq
        x_specs = [pl.BlockSpec(memory_space=pl.ANY), _resident(meta.shape, lambda i: (0, 0))]
        scratch += [pltpu.VMEM((2, n_t, n_seq, D_MODEL), F32), pltpu.SemaphoreType.DMA((2,))]
    assert n_rows % rows == 0
    return pl.pallas_call(
        functools.partial(body, n_seq=n_seq, n_t=n_t),
        grid=(n_rows // rows,),
        in_specs=x_specs + [state_spec(n_seq), state_spec(rc_rows), state_spec(sc_rows)]
        + [_layer_weight_spec(w[n], layer) for n in _MIXER_WEIGHTS],
        out_specs=[row_spec,
                   pl.BlockSpec((n_seq, D_MODEL), lambda i: (0, 0)),
                   pl.BlockSpec((rc_rows, D_MODEL), lambda i: (0, 0)),
                   pl.BlockSpec((sc_rows, D_MODEL), lambda i: (0, 0))],
        out_shape=[jax.ShapeDtypeStruct((n_rows, D_MODEL), F32),
                   jax.ShapeDtypeStruct((n_seq, D_MODEL), F32),
                   jax.ShapeDtypeStruct((rc_rows, D_MODEL), F32),
                   jax.ShapeDtypeStruct((sc_rows, D_MODEL), F32)],
        scratch_shapes=scratch,
        compiler_params=_COMPILER_PARAMS,
        name=f"mixer_s{n_seq}" + ("_in" if meta is not None else ""),
    )(*x_args, h0, rc0, sc0, *[w[n] for n in _MIXER_WEIGHTS])


def _ffn_call(x, w, layer, *, rows, final_norm, seq_major_out=None):
    n_rows = x.shape[0]
    assert n_rows % rows == 0
    row_spec = pl.BlockSpec((rows, D_MODEL), lambda i: (i, 0))
    gf = w["final_norm_g"]
    in_specs = ([row_spec] + [_layer_weight_spec(w[n], layer) for n in _FFN_WEIGHTS]
                + [_resident(gf.shape, lambda i: (0, 0))])
    args = (x, *[w[n] for n in _FFN_WEIGHTS], gf)
    if seq_major_out is None:
        return pl.pallas_call(
            functools.partial(_ffn_kernel, final_norm=final_norm),
            grid=(n_rows // rows,), in_specs=in_specs, out_specs=row_spec,
            out_shape=jax.ShapeDtypeStruct((n_rows, D_MODEL), F32),
            compiler_params=_COMPILER_PARAMS, name=f"ffn_r{rows}",
        )(*args)
    n_seq, n_t = seq_major_out
    assert final_norm and rows == n_seq * n_t and n_t % SUBLANES == 0 and n_t > N_META
    return pl.pallas_call(
        functools.partial(_ffn_to_seq_major_kernel, n_seq=n_seq, n_t=n_t),
        grid=(n_rows // rows,), in_specs=in_specs, out_specs=pl.BlockSpec(memory_space=pl.ANY),
        out_shape=jax.ShapeDtypeStruct((n_seq, n_rows // n_seq - N_META, D_MODEL), F32),
        scratch_shapes=[pltpu.VMEM((2, n_t, n_seq, D_MODEL), F32), pltpu.SemaphoreType.DMA((2,))],
        compiler_params=_COMPILER_PARAMS, name=f"ffn_r{rows}_out",
    )(*args)


def _block_diag_gates(gate_a_w, gate_x_w):
    hpc = GATE_CHUNK // RNN_HEAD_DIM
    eye = jnp.eye(hpc, dtype=gate_a_w.dtype)

    def bd(wt):
        wt = wt.reshape(DEPTH, N_GATE_CHUNKS, hpc, RNN_HEAD_DIM, RNN_HEAD_DIM)
        full = wt[:, :, :, :, None, :] * eye[None, None, :, None, :, None]
        return full.reshape(DEPTH, N_GATE_CHUNKS, GATE_CHUNK, GATE_CHUNK)

    return jnp.concatenate([bd(gate_a_w), bd(gate_x_w)], axis=-1).astype(BF16)


def _time_major(state):
    return jnp.swapaxes(state, 0, 1).reshape(-1, D_MODEL)


def _seq_major(state, n_seq):
    return jnp.swapaxes(state.reshape(-1, n_seq, D_MODEL), 0, 1)


PROMPT_N_T = 86
PROMPT_FFN_ROWS = 688
PROMPT_IO_N_T = 48


def kernel(x_prompt, x_sample, state_rnn_h, state_rnn_conv, state_sc_conv, meta_tokens, norm1_g, w_in, rnn_conv_w, rnn_conv_b, gate_a_w, gate_a_b, gate_x_w, gate_x_b, lru_lambda, w_branch_a, sc_conv_w, w_branch_b, w_out, norm2_g, w_ff_gate, w_ff_up, w_ff_down, final_norm_g):
    row = lambda p: p.reshape(DEPTH, 1, D_MODEL)
    w = {
        "norm1_g": row(norm1_g), "w_in": w_in.astype(BF16),
        "rnn_conv_w": rnn_conv_w, "rnn_conv_b": row(rnn_conv_b),
        "w_gate": _block_diag_gates(gate_a_w, gate_x_w),
        "gate_a_b": row(gate_a_b), "gate_x_b": row(gate_x_b), "lru_lambda": row(lru_lambda),
        "w_branch_a": w_branch_a.astype(BF16), "sc_conv_w": sc_conv_w,
        "w_branch_b": w_branch_b.astype(BF16), "w_out": w_out.astype(BF16),
        "norm2_g": row(norm2_g), "w_ff_gate": w_ff_gate.astype(BF16),
        "w_ff_up": w_ff_up.astype(BF16), "w_ff_down": w_ff_down.astype(BF16),
        "final_norm_g": final_norm_g.reshape(1, D_MODEL),
    }
    dt = x_prompt.dtype

    def trunk(x, h_st, rc_st, sc_st, *, n_seq, n_t, ffn_rows, meta=None):
        hs, rcs, scs = [], [], []
        for layer in range(DEPTH):
            first, last = layer == 0, layer == DEPTH - 1
            seq_major_io = meta is not None
            x, h, rc, sc = _mixer_call(
                x, h_st[layer], _time_major(rc_st[layer]), _time_major(sc_st[layer]), w, layer, n_seq=n_seq,
                n_t=PROMPT_IO_N_T if (seq_major_io and first) else n_t, meta=meta if first else None)
            if seq_major_io and last:
                x = _ffn_call(x, w, layer, rows=n_seq * PROMPT_IO_N_T, final_norm=True,
                              seq_major_out=(n_seq, PROMPT_IO_N_T))
            else:
                x = _ffn_call(x, w, layer, rows=ffn_rows, final_norm=last)
            hs.append(h)
            rcs.append(_seq_major(rc, n_seq))
            scs.append(_seq_major(sc, n_seq))
        return x, jnp.stack(hs), jnp.stack(rcs), jnp.stack(scs)

    bp = x_prompt.shape[0]
    y_prompt, rnn_h_p, rnn_conv_p, sc_conv_p = trunk(
        x_prompt, jnp.zeros((DEPTH, bp, D_MODEL), dt), jnp.zeros((DEPTH, bp, RNN_CONV_W - 1, D_MODEL), dt),
        jnp.zeros((DEPTH, bp, SC_CONV_W - 1, D_MODEL), dt),
        n_seq=bp, n_t=PROMPT_N_T, ffn_rows=PROMPT_FFN_ROWS, meta=meta_tokens.astype(dt))

    bs, t_s, _ = x_sample.shape
    xs = jnp.swapaxes(x_sample, 0, 1).reshape(t_s * bs, D_MODEL)
    ys, rnn_h_s, rnn_conv_s, sc_conv_s = trunk(
        xs, state_rnn_h, state_rnn_conv, state_sc_conv, n_seq=bs, n_t=t_s, ffn_rows=t_s * bs)
    y_sample = jnp.swapaxes(ys.reshape(t_s, bs, D_MODEL), 0, 1)

    return (y_prompt, y_sample, rnn_h_p, rnn_conv_p, sc_conv_p, rnn_h_s, rnn_conv_s, sc_conv_s)
```

```python
import functools
import math

import jax
import jax.numpy as jnp
from jax import lax
from jax.experimental import pallas as pl
from jax.experimental.pallas import tpu as pltpu

D_MODEL = 1024
DEPTH = 4
N_META = 16
N_RNN_HEADS = 16
RNN_HEAD_DIM = D_MODEL // N_RNN_HEADS
RNN_CONV_W = 4
SC_CONV_W = 3
LRU_C = 8.0
D_FF = 2816
EPS = 1e-6
N_SPLITS = 7
GATE_CHUNK = 256
N_GATE_CHUNKS = D_MODEL // GATE_CHUNK
SUBLANES = 8

V7X_VMEM_BYTES = 64 * 1024 * 1024
VMEM_LIMIT_BYTES = V7X_VMEM_BYTES - 2 * 1024 * 1024

F32 = jnp.float32
BF16 = jnp.bfloat16


def _sigmoid_of_half(half_x):
    return 0.5 * jnp.tanh(half_x) + 0.5


def _gelu_tanh(x):
    c = math.sqrt(2.0 / math.pi)
    half_x = 0.5 * x
    return half_x * jnp.tanh(x * (c + (c * 0.044715) * (x * x))) + half_x


def _softplus(x):
    return jnp.maximum(x, 0.0) + jnp.log1p(jnp.exp(-jnp.abs(x)))


def _rmsnorm(x, g):
    ms = jnp.mean(x * x, axis=-1, keepdims=True)
    return x * lax.rsqrt(ms + EPS) * g


def _dot(a, b):
    return jnp.dot(a, b, preferred_element_type=F32)


def _mixer_body(read_x, step, h0_ref, rc0_ref, sc0_ref, g1_ref, win_ref, cw_ref, cb_ref, wgate_ref,
                ba_ref, bx_ref, lam_ref, wba_ref, scw_ref, wbb_ref, wout_ref,
                xo_ref, h_ref, rc_ref, sc_ref,
                u_bf, xrbuf, chbuf, a_buf, h_buf, ya_buf, yb_buf, ga_buf, gb_buf, *, n_seq, n_t):
    rows = n_seq * n_t
    rc_rows = (RNN_CONV_W - 1) * n_seq
    sc_rows = (SC_CONV_W - 1) * n_seq

    def proj(k):
        return _dot(u_bf[...], win_ref[:, k * D_MODEL:(k + 1) * D_MODEL])

    @pl.when(step == 0)
    def _():
        h_ref[...] = h0_ref[...]
        xrbuf[0:rc_rows, :] = rc0_ref[...]
        chbuf[0:sc_rows, :] = sc0_ref[...]

    u_bf[...] = _rmsnorm(read_x(), g1_ref[...]).astype(BF16)
    xrbuf[rc_rows:rc_rows + rows, :] = proj(0)
    xc = cb_ref[...] + xrbuf[0:rows, :] * cw_ref[0:1, :]
    for k in range(1, RNN_CONV_W):
        xc = xc + xrbuf[k * n_seq:k * n_seq + rows, :] * cw_ref[k:k + 1, :]
    h_buf[...] = xc
    xc_b = xc.astype(BF16)
    for j in range(N_GATE_CHUNKS):
        cols = slice(j * GATE_CHUNK, (j + 1) * GATE_CHUNK)
        gates = _dot(xc_b[:, cols], wgate_ref[j])
        ga_buf[:, cols] = gates[:, :GATE_CHUNK]
        gb_buf[:, cols] = gates[:, GATE_CHUNK:]
    chbuf[sc_rows:sc_rows + rows, :] = proj(3) * proj(4)

    @pl.when(step >= 0)
    def _():
        half_log_a_max = (-0.5 * LRU_C) * _softplus(-lam_ref[...])
        t_r = jnp.tanh(ga_buf[...] + 0.5 * ba_ref[...])
        log_a = t_r * half_log_a_max + half_log_a_max
        a = jnp.exp(log_a)
        one_minus_a2 = jnp.maximum(jnp.tanh(log_a) * (-1.0 - a * a), 1e-12)
        mult = one_minus_a2 * lax.rsqrt(one_minus_a2)
        gate_i = _sigmoid_of_half(gb_buf[...] + 0.5 * bx_ref[...])
        a_buf[...] = a
        h_buf[...] = mult * gate_i * h_buf[...]

        if n_seq == SUBLANES:
            def scan_step(t, h):
                sl = pl.ds(pl.multiple_of(t * n_seq, n_seq), n_seq)
                h = a_buf[sl, :] * h + h_buf[sl, :]
                h_buf[sl, :] = h
                return h
            h_ref[...] = lax.fori_loop(0, n_t, scan_step, h_ref[...], unroll=True)
        else:
            for t in range(n_t):
                sl = slice(t * n_seq, (t + 1) * n_seq)
                prev = h_ref[...] if t == 0 else h_buf[(t - 1) * n_seq:t * n_seq, :]
                h_buf[sl, :] = a_buf[sl, :] * prev + h_buf[sl, :]
            h_ref[...] = h_buf[(n_t - 1) * n_seq:n_t * n_seq, :]

        ya_buf[...] = (h_buf[...] * _gelu_tanh(proj(1))).astype(BF16)

        vc = chbuf[0:rows, :] * scw_ref[0:1, :]
        for k in range(1, SC_CONV_W):
            vc = vc + chbuf[k * n_seq:k * n_seq + rows, :] * scw_ref[k:k + 1, :]
        yb_buf[...] = (proj(2) * vc).astype(BF16)

        ga_buf[...] = _sigmoid_of_half(proj(5))
        gb_buf[...] = _sigmoid_of_half(proj(6))

        new_rc = xrbuf[rows:rows + rc_rows, :]
        new_sc = chbuf[rows:rows + sc_rows, :]
        xrbuf[0:rc_rows, :] = new_rc
        chbuf[0:sc_rows, :] = new_sc

        @pl.when(step == pl.num_programs(0) - 1)
        def _():
            rc_ref[...] = new_rc
            sc_ref[...] = new_sc

    @pl.when(step >= 0)
    def _():
        m = ga_buf[...] * _dot(ya_buf[...], wba_ref[...])
        m = m + gb_buf[...] * _dot(yb_buf[...], wbb_ref[...])
        xo_ref[...] = read_x() + _dot(m.astype(BF16), wout_ref[...])


def _mixer_kernel(x_ref, *refs, n_seq, n_t):
    _mixer_body(lambda: x_ref[...], pl.program_id(0), *refs, n_seq=n_seq, n_t=n_t)


def _seq_major_tile_copies(seq_hbm, tile_buf, sem, tile, slot, *, n_seq, n_t, to_hbm):
    copies = []
    for b in range(n_seq):
        if isinstance(tile, int):
            assert tile == 0
            hbm = seq_hbm.at[b, pl.ds(0, n_t - N_META), :]
            vmem = tile_buf.at[slot, pl.ds(N_META, n_t - N_META), b, :]
        else:
            t0 = pl.multiple_of(tile * n_t - N_META, SUBLANES)
            hbm = seq_hbm.at[b, pl.ds(t0, n_t), :]
            vmem = tile_buf.at[slot, :, b, :]
        src, dst = (vmem, hbm) if to_hbm else (hbm, vmem)
        copies.append(pltpu.make_async_copy(src, dst, sem.at[slot]))
    return copies


def _mixer_from_seq_major_kernel(xs_hbm, meta_ref, *refs, n_seq, n_t):
    *refs, xin, sem = refs
    step = pl.program_id(0)
    n_steps = pl.num_programs(0)
    slot = step % 2
    copies = functools.partial(_seq_major_tile_copies, xs_hbm, xin, sem, n_seq=n_seq, n_t=n_t, to_hbm=False)

    @pl.when(step == 0)
    def _():
        for c in copies(0, 0):
            c.start()
        xin[0, 0:N_META, :, :] = jnp.broadcast_to(meta_ref[...][:, None, :], (N_META, n_seq, D_MODEL))

        @pl.when(n_steps > 1)
        def _():
            for c in copies(step + 1, 1):
                c.start()
        for c in copies(0, 0):
            c.wait()

    @pl.when(step > 0)
    def _():
        @pl.when(step + 1 < n_steps)
        def _():
            for c in copies(step + 1, 1 - slot):
                c.start()
        for c in copies(step, slot):
            c.wait()

    _mixer_body(lambda: xin[slot].reshape(n_t * n_seq, D_MODEL), step, *refs, n_seq=n_seq, n_t=n_t)


def _ffn_body(x, g2_ref, wg_ref, wu_ref, wd_ref, gf_ref, *, final_norm):
    v = _rmsnorm(x, g2_ref[...]).astype(BF16)
    half = _dot(v, wg_ref[...])
    hid = ((half * jnp.tanh(half) + half) * _dot(v, wu_ref[...])).astype(BF16)
    y = x + _dot(hid, wd_ref[...])
    if final_norm:
        y = _rmsnorm(y, gf_ref[...])
    return y


def _ffn_kernel(x_ref, g2_ref, wg_ref, wu_ref, wd_ref, gf_ref, o_ref, *, final_norm):
    o_ref[...] = _ffn_body(x_ref[...], g2_ref, wg_ref, wu_ref, wd_ref, gf_ref, final_norm=final_norm)


def _ffn_to_seq_major_kernel(x_ref, g2_ref, wg_ref, wu_ref, wd_ref, gf_ref, ys_hbm, yout, sem, *, n_seq, n_t):
    step = pl.program_id(0)
    n_steps = pl.num_programs(0)
    slot = step % 2
    copies = functools.partial(_seq_major_tile_copies, ys_hbm, yout, sem, n_seq=n_seq, n_t=n_t, to_hbm=True)
    y = _ffn_body(x_ref[...], g2_ref, wg_ref, wu_ref, wd_ref, gf_ref, final_norm=True)

    @pl.when(step == 1)
    def _():
        for c in copies(0, 0):
            c.wait()

    @pl.when(step > 1)
    def _():
        for c in copies(step - 1, 1 - slot):
            c.wait()

    yout[slot] = y.reshape(n_t, n_seq, D_MODEL)

    @pl.when(step == 0)
    def _():
        for c in copies(0, 0):
            c.start()

        @pl.when(n_steps == 1)
        def _():
            for c in copies(0, 0):
                c.wait()

    @pl.when(step > 0)
    def _():
        for c in copies(step, slot):
            c.start()

        @pl.when(step == n_steps - 1)
        def _():
            for c in copies(step, slot):
                c.wait()


def _resident(shape, index):
    return pl.BlockSpec(shape, index, pipeline_mode=pl.Buffered(1))


def _layer_weight_spec(arr, layer):
    zeros = (0,) * (arr.ndim - 1)
    return _resident((None,) + arr.shape[1:], lambda i: (layer,) + zeros)


_MIXER_WEIGHTS = ("norm1_g", "w_in", "rnn_conv_w", "rnn_conv_b", "w_gate", "gate_a_b", "gate_x_b",
                  "lru_lambda", "w_branch_a", "sc_conv_w", "w_branch_b", "w_out")
_FFN_WEIGHTS = ("norm2_g", "w_ff_gate", "w_ff_up", "w_ff_down")
_COMPILER_PARAMS = pltpu.CompilerParams(dimension_semantics=("arbitrary",), vmem_limit_bytes=VMEM_LIMIT_BYTES)


def _mixer_call(x, h0, rc0, sc0, w, layer, *, n_seq, n_t, meta=None):
    rows = n_seq * n_t
    rc_rows = (RNN_CONV_W - 1) * n_seq
    sc_rows = (SC_CONV_W - 1) * n_seq
    row_spec = pl.BlockSpec((rows, D_MODEL), lambda i: (i, 0))
    state_spec = lambda r: _resident((r, D_MODEL), lambda i: (0, 0))
    scratch = [pltpu.VMEM((rows, D_MODEL), BF16),
               pltpu.VMEM((rc_rows + rows, D_MODEL), F32),
               pltpu.VMEM((sc_rows + rows, D_MODEL), F32),
               pltpu.VMEM((rows, D_MODEL), F32),
               pltpu.VMEM((rows, D_MODEL), F32),
               pltpu.VMEM((rows, D_MODEL), BF16),
               pltpu.VMEM((rows, D_MODEL), BF16),
               pltpu.VMEM((rows, D_MODEL), F32),
               pltpu.VMEM((rows, D_MODEL), F32)]
    if meta is None:
        body, x_args, n_rows = _mixer_kernel, (x,), x.shape[0]
        x_specs = [row_spec]
    else:
        assert n_t % SUBLANES == 0 and n_t > N_META and n_seq == SUBLANES
        body, x_args, n_rows = _mixer_from_seq_major_kernel, (x, meta), (x.shape[1] + N_META) * n_seq
        x_specs = [pl.BlockSpec(memory_space=pl.ANY), _resident(meta.shape, lambda i: (0, 0))]
        scratch += [pltpu.VMEM((2, n_t, n_seq, D_MODEL), F32), pltpu.SemaphoreType.DMA((2,))]
    assert n_rows % rows == 0
    return pl.pallas_call(
        functools.partial(body, n_seq=n_seq, n_t=n_t),
        grid=(n_rows // rows,),
        in_specs=x_specs + [state_spec(n_seq), state_spec(rc_rows), state_spec(sc_rows)]
        + [_layer_weight_spec(w[n], layer) for n in _MIXER_WEIGHTS],
        out_specs=[row_spec,
                   pl.BlockSpec((n_seq, D_MODEL), lambda i: (0, 0)),
                   pl.BlockSpec((rc_rows, D_MODEL), lambda i: (0, 0)),
                   pl.BlockSpec((sc_rows, D_MODEL), lambda i: (0, 0))],
        out_shape=[jax.ShapeDtypeStruct((n_rows, D_MODEL), F32),
                   jax.ShapeDtypeStruct((n_seq, D_MODEL), F32),
                   jax.ShapeDtypeStruct((rc_rows, D_MODEL), F32),
                   jax.ShapeDtypeStruct((sc_rows, D_MODEL), F32)],
        scratch_shapes=scratch,
        compiler_params=_COMPILER_PARAMS,
        name=f"mixer_s{n_seq}" + ("_in" if meta is not None else ""),
    )(*x_args, h0, rc0, sc0, *[w[n] for n in _MIXER_WEIGHTS])


def _ffn_call(x, w, layer, *, rows, final_norm, seq_major_out=None):
    n_rows = x.shape[0]
    assert n_rows % rows == 0
    row_spec = pl.BlockSpec((rows, D_MODEL), lambda i: (i, 0))
    gf = w["final_norm_g"]
    in_specs = ([row_spec] + [_layer_weight_spec(w[n], layer) for n in _FFN_WEIGHTS]
                + [_resident(gf.shape, lambda i: (0, 0))])
    args = (x, *[w[n] for n in _FFN_WEIGHTS], gf)
    if seq_major_out is None:
        return pl.pallas_call(
            functools.partial(_ffn_kernel, final_norm=final_norm),
            grid=(n_rows // rows,), in_specs=in_specs, out_specs=row_spec,
            out_shape=jax.ShapeDtypeStruct((n_rows, D_MODEL), F32),
            compiler_params=_COMPILER_PARAMS, name=f"ffn_r{rows}",
        )(*args)
    n_seq, n_t = seq_major_out
    assert final_norm and rows == n_seq * n_t and n_t % SUBLANES == 0 and n_t > N_META
    return pl.pallas_call(
        functools.partial(_ffn_to_seq_major_kernel, n_seq=n_seq, n_t=n_t),
        grid=(n_rows // rows,), in_specs=in_specs, out_specs=pl.BlockSpec(memory_space=pl.ANY),
        out_shape=jax.ShapeDtypeStruct((n_seq, n_rows // n_seq - N_META, D_MODEL), F32),
        scratch_shapes=[pltpu.VMEM((2, n_t, n_seq, D_MODEL), F32), pltpu.SemaphoreType.DMA((2,))],
        compiler_params=_COMPILER_PARAMS, name=f"ffn_r{rows}_out",
    )(*args)


def _block_diag_gates(gate_a_w, gate_x_w):
    hpc = GATE_CHUNK // RNN_HEAD_DIM
    eye = jnp.eye(hpc, dtype=gate_a_w.dtype)

    def bd(wt):
        wt = wt.reshape(DEPTH, N_GATE_CHUNKS, hpc, RNN_HEAD_DIM, RNN_HEAD_DIM)
        full = wt[:, :, :, :, None, :] * eye[None, None, :, None, :, None]
        return full.reshape(DEPTH, N_GATE_CHUNKS, GATE_CHUNK, GATE_CHUNK)

    return jnp.concatenate([bd(gate_a_w), bd(gate_x_w)], axis=-1).astype(BF16)


def _time_major(state):
    return jnp.swapaxes(state, 0, 1).reshape(-1, D_MODEL)


def _seq_major(state, n_seq):
    return jnp.swapaxes(state.reshape(-1, n_seq, D_MODEL), 0, 1)


PROMPT_N_T = 86
PROMPT_FFN_ROWS = 688
PROMPT_IO_N_T = 48


def kernel(x_prompt, x_sample, state_rnn_h, state_rnn_conv, state_sc_conv, meta_tokens, norm1_g, w_in, rnn_conv_w, rnn_conv_b, gate_a_w, gate_a_b, gate_x_w, gate_x_b, lru_lambda, w_branch_a, sc_conv_w, w_branch_b, w_out, norm2_g, w_ff_gate, w_ff_up, w_ff_down, final_norm_g):
    row = lambda p: p.reshape(DEPTH, 1, D_MODEL)
    merge_gate_half = jnp.concatenate([jnp.ones(((N_SPLITS - 2) * D_MODEL,), F32), jnp.full((2 * D_MODEL,), 0.5, F32)])
    w = {
        "norm1_g": row(norm1_g), "w_in": (w_in * merge_gate_half).astype(BF16),
        "rnn_conv_w": rnn_conv_w, "rnn_conv_b": row(rnn_conv_b),
        "w_gate": _block_diag_gates(0.5 * gate_a_w, 0.5 * gate_x_w),
        "gate_a_b": row(gate_a_b), "gate_x_b": row(gate_x_b), "lru_lambda": row(lru_lambda),
        "w_branch_a": w_branch_a.astype(BF16), "sc_conv_w": sc_conv_w,
        "w_branch_b": w_branch_b.astype(BF16), "w_out": w_out.astype(BF16),
        "norm2_g": row(norm2_g), "w_ff_gate": (0.5 * w_ff_gate).astype(BF16),
        "w_ff_up": w_ff_up.astype(BF16), "w_ff_down": w_ff_down.astype(BF16),
        "final_norm_g": final_norm_g.reshape(1, D_MODEL),
    }
    dt = x_prompt.dtype

    def trunk(x, h_st, rc_st, sc_st, *, n_seq, n_t, ffn_rows, meta=None):
        hs, rcs, scs = [], [], []
        for layer in range(DEPTH):
            first, last = layer == 0, layer == DEPTH - 1
            seq_major_io = meta is not None
            x, h, rc, sc = _mixer_call(
                x, h_st[layer], _time_major(rc_st[layer]), _time_major(sc_st[layer]), w, layer, n_seq=n_seq,
                n_t=PROMPT_IO_N_T if (seq_major_io and first) else n_t, meta=meta if first else None)
            if seq_major_io and last:
                x = _ffn_call(x, w, layer, rows=n_seq * PROMPT_IO_N_T, final_norm=True,
                              seq_major_out=(n_seq, PROMPT_IO_N_T))
            else:
                x = _ffn_call(x, w, layer, rows=ffn_rows, final_norm=last)
            hs.append(h)
            rcs.append(_seq_major(rc, n_seq))
            scs.append(_seq_major(sc, n_seq))
        return x, jnp.stack(hs), jnp.stack(rcs), jnp.stack(scs)

    bp = x_prompt.shape[0]
    y_prompt, rnn_h_p, rnn_conv_p, sc_conv_p = trunk(
        x_prompt, jnp.zeros((DEPTH, bp, D_MODEL), dt), jnp.zeros((DEPTH, bp, RNN_CONV_W - 1, D_MODEL), dt),
        jnp.zeros((DEPTH, bp, SC_CONV_W - 1, D_MODEL), dt),
        n_seq=bp, n_t=PROMPT_N_T, ffn_rows=PROMPT_FFN_ROWS, meta=meta_tokens.astype(dt))

    bs, t_s, _ = x_sample.shape
    xs = jnp.swapaxes(x_sample, 0, 1).reshape(t_s * bs, D_MODEL)
    ys, rnn_h_s, rnn_conv_s, sc_conv_s = trunk(
        xs, state_rnn_h, state_rnn_conv, state_sc_conv, n_seq=bs, n_t=t_s, ffn_rows=t_s * bs)
    y_sample = jnp.swapaxes(ys.reshape(t_s, bs, D_MODEL), 0, 1)

    return (y_prompt, y_sample, rnn_h_p, rnn_conv_p, sc_conv_p, rnn_h_s, rnn_conv_s, sc_conv_s)
```

```python
import functools
import math

import jax
import jax.numpy as jnp
from jax import lax
from jax.experimental import pallas as pl
from jax.experimental.pallas import tpu as pltpu

D_MODEL = 1024
DEPTH = 4
N_META = 16
N_RNN_HEADS = 16
RNN_HEAD_DIM = D_MODEL // N_RNN_HEADS
RNN_CONV_W = 4
SC_CONV_W = 3
LRU_C = 8.0
D_FF = 2816
EPS = 1e-6
N_SPLITS = 7
GATE_CHUNK = 256
N_GATE_CHUNKS = D_MODEL // GATE_CHUNK
SUBLANES = 8

V7X_VMEM_BYTES = 64 * 1024 * 1024
VMEM_LIMIT_BYTES = V7X_VMEM_BYTES - 2 * 1024 * 1024

F32 = jnp.float32
BF16 = jnp.bfloat16


def _sigmoid_of_half(half_x):
    return 0.5 * jnp.tanh(half_x) + 0.5


def _gelu_tanh(x):
    c = math.sqrt(2.0 / math.pi)
    half_x = 0.5 * x
    return half_x * jnp.tanh(x * (c + (c * 0.044715) * (x * x))) + half_x


def _softplus(x):
    return jnp.maximum(x, 0.0) + jnp.log1p(jnp.exp(-jnp.abs(x)))


def _rmsnorm(x, g):
    ms = jnp.mean(x * x, axis=-1, keepdims=True)
    return x * lax.rsqrt(ms + EPS) * g


def _dot(a, b):
    return jnp.dot(a, b, preferred_element_type=F32)


def _mixer_body(read_x, step, h0_ref, rc0_ref, sc0_ref, g1_ref, win_ref, cw_ref, cb_ref, wgate_ref,
                ba_ref, bx_ref, lam_ref, wba_ref, scw_ref, wbb_ref, wout_ref,
                xo_ref, h_ref, rc_ref, sc_ref,
                u_bf, xrbuf, chbuf, a_buf, h_buf, ya_buf, yb_buf, ga_buf, gb_buf, *, n_seq, n_t):
    rows = n_seq * n_t
    rc_rows = (RNN_CONV_W - 1) * n_seq
    sc_rows = (SC_CONV_W - 1) * n_seq

    def proj(k):
        return _dot(u_bf[...], win_ref[:, k * D_MODEL:(k + 1) * D_MODEL])

    @pl.when(step == 0)
    def _():
        h_ref[...] = h0_ref[...]
        xrbuf[0:rc_rows, :] = rc0_ref[...]
        chbuf[0:sc_rows, :] = sc0_ref[...]

    u_bf[...] = _rmsnorm(read_x(), g1_ref[...]).astype(BF16)
    xrbuf[rc_rows:rc_rows + rows, :] = proj(0)
    xc = cb_ref[...] + xrbuf[0:rows, :] * cw_ref[0:1, :]
    for k in range(1, RNN_CONV_W):
        xc = xc + xrbuf[k * n_seq:k * n_seq + rows, :] * cw_ref[k:k + 1, :]
    h_buf[...] = xc
    xc_b = xc.astype(BF16)
    for j in range(N_GATE_CHUNKS):
        cols = slice(j * GATE_CHUNK, (j + 1) * GATE_CHUNK)
        gates = _dot(xc_b[:, cols], wgate_ref[j])
        ga_buf[:, cols] = gates[:, :GATE_CHUNK]
        gb_buf[:, cols] = gates[:, GATE_CHUNK:]
    chbuf[sc_rows:sc_rows + rows, :] = proj(3) * proj(4)

    @pl.when(step >= 0)
    def _():
        half_log_a_max = (-0.5 * LRU_C) * _softplus(-lam_ref[...])
        t_r = jnp.tanh(ga_buf[...] + 0.5 * ba_ref[...])
        log_a = t_r * half_log_a_max + half_log_a_max
        a = jnp.exp(log_a)
        one_minus_a2 = jnp.maximum(jnp.tanh(log_a) * (-1.0 - a * a), 1e-12)
        mult = one_minus_a2 * lax.rsqrt(one_minus_a2)
        gate_i = _sigmoid_of_half(gb_buf[...] + 0.5 * bx_ref[...])
        a_buf[...] = a
        h_buf[...] = mult * gate_i * h_buf[...]

        if n_seq == SUBLANES:
            def scan_step(t, h):
                sl = pl.ds(pl.multiple_of(t * n_seq, n_seq), n_seq)
                h = a_buf[sl, :] * h + h_buf[sl, :]
                h_buf[sl, :] = h
                return h
            h_ref[...] = lax.fori_loop(0, n_t, scan_step, h_ref[...], unroll=True)
        else:
            for t in range(n_t):
                sl = slice(t * n_seq, (t + 1) * n_seq)
                prev = h_ref[...] if t == 0 else h_buf[(t - 1) * n_seq:t * n_seq, :]
                h_buf[sl, :] = a_buf[sl, :] * prev + h_buf[sl, :]
            h_ref[...] = h_buf[(n_t - 1) * n_seq:n_t * n_seq, :]

        ya_buf[...] = (h_buf[...] * _gelu_tanh(proj(1))).astype(BF16)

        vc = chbuf[0:rows, :] * scw_ref[0:1, :]
        for k in range(1, SC_CONV_W):
            vc = vc + chbuf[k * n_seq:k * n_seq + rows, :] * scw_ref[k:k + 1, :]
        yb_buf[...] = (proj(2) * vc).astype(BF16)

        ga_buf[...] = _sigmoid_of_half(proj(5))
        gb_buf[...] = _sigmoid_of_half(proj(6))

        new_rc = xrbuf[rows:rows + rc_rows, :]
        new_sc = chbuf[rows:rows + sc_rows, :]
        xrbuf[0:rc_rows, :] = new_rc
        chbuf[0:sc_rows, :] = new_sc

        @pl.when(step == pl.num_programs(0) - 1)
        def _():
            rc_ref[...] = new_rc
            sc_ref[...] = new_sc

    @pl.when(step >= 0)
    def _():
        m = ga_buf[...] * _dot(ya_buf[...], wba_ref[...])
        m = m + gb_buf[...] * _dot(yb_buf[...], wbb_ref[...])
        xo_ref[...] = read_x() + _dot(m.astype(BF16), wout_ref[...])


def _mixer_kernel(x_ref, *refs, n_seq, n_t):
    _mixer_body(lambda: x_ref[...], pl.program_id(0), *refs, n_seq=n_seq, n_t=n_t)


def _seq_major_tile_copies(seq_hbm, tile_buf, sem, tile, slot, *, n_seq, n_t, to_hbm):
    copies = []
    for b in range(n_seq):
        if isinstance(tile, int):
            assert tile == 0
            hbm = seq_hbm.at[b, pl.ds(0, n_t - N_META), :]
            vmem = tile_buf.at[slot, pl.ds(N_META, n_t - N_META), b, :]
        else:
            t0 = pl.multiple_of(tile * n_t - N_META, SUBLANES)
            hbm = seq_hbm.at[b, pl.ds(t0, n_t), :]
            vmem = tile_buf.at[slot, :, b, :]
        src, dst = (vmem, hbm) if to_hbm else (hbm, vmem)
        copies.append(pltpu.make_async_copy(src, dst, sem.at[slot]))
    return copies


def _mixer_from_seq_major_kernel(xs_hbm, meta_ref, *refs, n_seq, n_t):
    *refs, xin, sem = refs
    step = pl.program_id(0)
    n_steps = pl.num_programs(0)
    slot = step % 2
    copies = functools.partial(_seq_major_tile_copies, xs_hbm, xin, sem, n_seq=n_seq, n_t=n_t, to_hbm=False)

    @pl.when(step == 0)
    def _():
        for c in copies(0, 0):
            c.start()
        xin[0, 0:N_META, :, :] = jnp.broadcast_to(meta_ref[...][:, None, :], (N_META, n_seq, D_MODEL))

        @pl.when(n_steps > 1)
        def _():
            for c in copies(step + 1, 1):
                c.start()
        for c in copies(0, 0):
            c.wait()

    @pl.when(step > 0)
    def _():
        @pl.when(step + 1 < n_steps)
        def _():
            for c in copies(step + 1, 1 - slot):
                c.start()
        for c in copies(step, slot):
            c.wait()

    _mixer_body(lambda: xin[slot].reshape(n_t * n_seq, D_MODEL), step, *refs, n_seq=n_seq, n_t=n_t)


def _ffn_body(x, g2_ref, wg_ref, wu_ref, wd_ref, gf_ref, *, final_norm):
    v = _rmsnorm(x, g2_ref[...]).astype(BF16)
    half = _dot(v, wg_ref[...])
    hid = ((half * jnp.tanh(half) + half) * _dot(v, wu_ref[...])).astype(BF16)
    y = x + _dot(hid, wd_ref[...])
    if final_norm:
        y = _rmsnorm(y, gf_ref[...])
    return y


def _cast_next_layer_weights(src_refs, dst_refs):
    @pl.when(pl.program_id(0) < N_CONVERT_STEPS)
    def _():
        for name, src, dst in zip(_BIG_WEIGHTS, src_refs, dst_refs):
            if name == "w_in":
                plain = (N_SPLITS - 2) * D_MODEL
                dst[:, :plain] = src[:, :plain].astype(BF16)
                dst[:, plain:] = (0.5 * src[:, plain:]).astype(BF16)
            elif name == "w_ff_gate":
                dst[...] = (0.5 * src[...]).astype(BF16)
            else:
                dst[...] = src[...].astype(BF16)


def _ffn_kernel(x_ref, g2_ref, wg_ref, wu_ref, wd_ref, gf_ref, *rest, final_norm):
    n_big = (len(rest) - 1) // 2
    next_f32, o_ref, next_bf16 = rest[:n_big], rest[n_big], rest[n_big + 1:]
    o_ref[...] = _ffn_body(x_ref[...], g2_ref, wg_ref, wu_ref, wd_ref, gf_ref, final_norm=final_norm)
    if n_big:
        _cast_next_layer_weights(next_f32, next_bf16)


def _ffn_to_seq_major_kernel(x_ref, g2_ref, wg_ref, wu_ref, wd_ref, gf_ref, ys_hbm, yout, sem, *, n_seq, n_t):
    step = pl.program_id(0)
    n_steps = pl.num_programs(0)
    slot = step % 2
    copies = functools.partial(_seq_major_tile_copies, ys_hbm, yout, sem, n_seq=n_seq, n_t=n_t, to_hbm=True)
    y = _ffn_body(x_ref[...], g2_ref, wg_ref, wu_ref, wd_ref, gf_ref, final_norm=True)

    @pl.when(step == 1)
    def _():
        for c in copies(0, 0):
            c.wait()

    @pl.when(step > 1)
    def _():
        for c in copies(step - 1, 1 - slot):
            c.wait()

    yout[slot] = y.reshape(n_t, n_seq, D_MODEL)

    @pl.when(step == 0)
    def _():
        for c in copies(0, 0):
            c.start()

        @pl.when(n_steps == 1)
        def _():
            for c in copies(0, 0):
                c.wait()

    @pl.when(step > 0)
    def _():
        for c in copies(step, slot):
            c.start()

        @pl.when(step == n_steps - 1)
        def _():
            for c in copies(step, slot):
                c.wait()


def _resident(shape, index):
    return pl.BlockSpec(shape, index, pipeline_mode=pl.Buffered(1))


def _weight_operands(names, small, big, layer):
    specs, arrays = [], []
    for name in names:
        if name in big:
            arr = big[name]
            specs.append(_resident(arr.shape, lambda i, nd=arr.ndim: (0,) * nd))
        else:
            arr = small[name]
            specs.append(_resident((None,) + arr.shape[1:], lambda i, nd=arr.ndim: (layer,) + (0,) * (nd - 1)))
        arrays.append(arr)
    return specs, arrays


_MIXER_WEIGHTS = ("norm1_g", "w_in", "rnn_conv_w", "rnn_conv_b", "w_gate", "gate_a_b", "gate_x_b",
                  "lru_lambda", "w_branch_a", "sc_conv_w", "w_branch_b", "w_out")
_FFN_WEIGHTS = ("norm2_g", "w_ff_gate", "w_ff_up", "w_ff_down")
_BIG_WEIGHTS = ("w_in", "w_branch_a", "w_branch_b", "w_out", "w_ff_gate", "w_ff_up", "w_ff_down")
N_CONVERT_STEPS = 16
_COMPILER_PARAMS = pltpu.CompilerParams(dimension_semantics=("arbitrary",), vmem_limit_bytes=VMEM_LIMIT_BYTES)


def _mixer_call(x, h0, rc0, sc0, small, big, layer, *, n_seq, n_t, meta=None):
    rows = n_seq * n_t
    rc_rows = (RNN_CONV_W - 1) * n_seq
    sc_rows = (SC_CONV_W - 1) * n_seq
    row_spec = pl.BlockSpec((rows, D_MODEL), lambda i: (i, 0))
    state_spec = lambda r: _resident((r, D_MODEL), lambda i: (0, 0))
    scratch = [pltpu.VMEM((rows, D_MODEL), BF16),
               pltpu.VMEM((rc_rows + rows, D_MODEL), F32),
               pltpu.VMEM((sc_rows + rows, D_MODEL), F32),
               pltpu.VMEM((rows, D_MODEL), F32),
               pltpu.VMEM((rows, D_MODEL), F32),
               pltpu.VMEM((rows, D_MODEL), BF16),
               pltpu.VMEM((rows, D_MODEL), BF16),
               pltpu.VMEM((rows, D_MODEL), F32),
               pltpu.VMEM((rows, D_MODEL), F32)]
    if meta is None:
        body, x_args, n_rows = _mixer_kernel, (x,), x.shape[0]
        x_specs = [row_spec]
    else:
        assert n_t % SUBLANES == 0 and n_t > N_META and n_seq == SUBLANES
        body, x_args, n_rows = _mixer_from_seq_major_kernel, (x, meta), (x.shape[1] + N_META) * n_seq
        x_specs = [pl.BlockSpec(memory_space=pl.ANY), _resident(meta.shape, lambda i: (0, 0))]
        scratch += [pltpu.VMEM((2, n_t, n_seq, D_MODEL), F32), pltpu.SemaphoreType.DMA((2,))]
    assert n_rows % rows == 0
    w_specs, w_arrays = _weight_operands(_MIXER_WEIGHTS, small, big, layer)
    return pl.pallas_call(
        functools.partial(body, n_seq=n_seq, n_t=n_t),
        grid=(n_rows // rows,),
        in_specs=x_specs + [state_spec(n_seq), state_spec(rc_rows), state_spec(sc_rows)] + w_specs,
        out_specs=[row_spec,
                   pl.BlockSpec((n_seq, D_MODEL), lambda i: (0, 0)),
                   pl.BlockSpec((rc_rows, D_MODEL), lambda i: (0, 0)),
                   pl.BlockSpec((sc_rows, D_MODEL), lambda i: (0, 0))],
        out_shape=[jax.ShapeDtypeStruct((n_rows, D_MODEL), F32),
                   jax.ShapeDtypeStruct((n_seq, D_MODEL), F32),
                   jax.ShapeDtypeStruct((rc_rows, D_MODEL), F32),
                   jax.ShapeDtypeStruct((sc_rows, D_MODEL), F32)],
        scratch_shapes=scratch,
        compiler_params=_COMPILER_PARAMS,
        name=f"mixer_s{n_seq}" + ("_in" if meta is not None else ""),
    )(*x_args, h0, rc0, sc0, *w_arrays)


def _ffn_call(x, small, big, layer, *, rows, final_norm, seq_major_out=None, next_f32=None):
    n_rows = x.shape[0]
    assert n_rows % rows == 0
    n_steps = n_rows // rows
    row_spec = pl.BlockSpec((rows, D_MODEL), lambda i: (i, 0))
    gf = small["final_norm_g"]
    w_specs, w_arrays = _weight_operands(_FFN_WEIGHTS, small, big, layer)
    in_specs = [row_spec] + w_specs + [_resident(gf.shape, lambda i: (0, 0))]
    args = (x, *w_arrays, gf)
    if seq_major_out is not None:
        n_seq, n_t = seq_major_out
        assert final_norm and next_f32 is None and rows == n_seq * n_t and n_t % SUBLANES == 0 and n_t > N_META
        return pl.pallas_call(
            functools.partial(_ffn_to_seq_major_kernel, n_seq=n_seq, n_t=n_t),
            grid=(n_steps,), in_specs=in_specs, out_specs=pl.BlockSpec(memory_space=pl.ANY),
            out_shape=jax.ShapeDtypeStruct((n_seq, n_rows // n_seq - N_META, D_MODEL), F32),
            scratch_shapes=[pltpu.VMEM((2, n_t, n_seq, D_MODEL), F32), pltpu.SemaphoreType.DMA((2,))],
            compiler_params=_COMPILER_PARAMS, name=f"ffn_r{rows}_out",
        )(*args)
    out_specs, out_shape = [row_spec], [jax.ShapeDtypeStruct((n_rows, D_MODEL), F32)]
    if next_f32 is not None:
        assert n_steps >= N_CONVERT_STEPS
        block = lambda i: jnp.minimum(i, N_CONVERT_STEPS - 1)
        for name in _BIG_WEIGHTS:
            _, n_in, n_out = next_f32[name].shape
            blk = n_in // N_CONVERT_STEPS
            assert blk * N_CONVERT_STEPS == n_in and blk % (2 * SUBLANES) == 0
            in_specs.append(pl.BlockSpec((None, blk, n_out), lambda i: (layer + 1, block(i), 0)))
            out_specs.append(pl.BlockSpec((blk, n_out), lambda i: (block(i), 0)))
            out_shape.append(jax.ShapeDtypeStruct((n_in, n_out), BF16))
        args += tuple(next_f32[name] for name in _BIG_WEIGHTS)
    outs = pl.pallas_call(
        functools.partial(_ffn_kernel, final_norm=final_norm),
        grid=(n_steps,), in_specs=in_specs, out_specs=out_specs, out_shape=out_shape,
        compiler_params=_COMPILER_PARAMS,
        name=f"ffn_r{rows}" + ("_cast" if next_f32 is not None else ""),
    )(*args)
    if next_f32 is None:
        return outs[0]
    return outs[0], dict(zip(_BIG_WEIGHTS, outs[1:]))


def _block_diag_gates(gate_a_w, gate_x_w):
    hpc = GATE_CHUNK // RNN_HEAD_DIM
    eye = jnp.eye(hpc, dtype=gate_a_w.dtype)

    def bd(wt):
        wt = wt.reshape(DEPTH, N_GATE_CHUNKS, hpc, RNN_HEAD_DIM, RNN_HEAD_DIM)
        full = wt[:, :, :, :, None, :] * eye[None, None, :, None, :, None]
        return full.reshape(DEPTH, N_GATE_CHUNKS, GATE_CHUNK, GATE_CHUNK)

    return jnp.concatenate([bd(gate_a_w), bd(gate_x_w)], axis=-1).astype(BF16)


def _time_major(state):
    return jnp.swapaxes(state, 0, 1).reshape(-1, D_MODEL)


def _seq_major(state, n_seq):
    return jnp.swapaxes(state.reshape(-1, n_seq, D_MODEL), 0, 1)


PROMPT_N_T = 86
PROMPT_FFN_ROWS = 688
PROMPT_IO_N_T = 48


def kernel(x_prompt, x_sample, state_rnn_h, state_rnn_conv, state_sc_conv, meta_tokens, norm1_g, w_in, rnn_conv_w, rnn_conv_b, gate_a_w, gate_a_b, gate_x_w, gate_x_b, lru_lambda, w_branch_a, sc_conv_w, w_branch_b, w_out, norm2_g, w_ff_gate, w_ff_up, w_ff_down, final_norm_g):
    row = lambda p: p.reshape(DEPTH, 1, D_MODEL)
    small = {
        "norm1_g": row(norm1_g), "rnn_conv_w": rnn_conv_w, "rnn_conv_b": row(rnn_conv_b),
        "w_gate": _block_diag_gates(0.5 * gate_a_w, 0.5 * gate_x_w),
        "gate_a_b": row(gate_a_b), "gate_x_b": row(gate_x_b), "lru_lambda": row(lru_lambda),
        "sc_conv_w": sc_conv_w, "norm2_g": row(norm2_g), "final_norm_g": final_norm_g.reshape(1, D_MODEL),
    }
    big_f32 = {"w_in": w_in, "w_branch_a": w_branch_a, "w_branch_b": w_branch_b, "w_out": w_out,
               "w_ff_gate": w_ff_gate, "w_ff_up": w_ff_up, "w_ff_down": w_ff_down}
    merge_gate_half = jnp.concatenate([jnp.ones(((N_SPLITS - 2) * D_MODEL,), F32), jnp.full((2 * D_MODEL,), 0.5, F32)])
    pre_scale = {"w_in": merge_gate_half, "w_ff_gate": 0.5}
    big = [{n: (big_f32[n][0] * pre_scale.get(n, 1.0)).astype(BF16) for n in _BIG_WEIGHTS}]
    dt = x_prompt.dtype

    def trunk(x, h_st, rc_st, sc_st, *, n_seq, n_t, ffn_rows, meta=None):
        hs, rcs, scs = [], [], []
        seq_major_io = meta is not None
        for layer in range(DEPTH):
            first, last = layer == 0, layer == DEPTH - 1
            x, h, rc, sc = _mixer_call(
                x, h_st[layer], _time_major(rc_st[layer]), _time_major(sc_st[layer]), small, big[layer], layer,
                n_seq=n_seq, n_t=PROMPT_IO_N_T if (seq_major_io and first) else n_t, meta=meta if first else None)
            if seq_major_io and last:
                x = _ffn_call(x, small, big[layer], layer, rows=n_seq * PROMPT_IO_N_T, final_norm=True,
                              seq_major_out=(n_seq, PROMPT_IO_N_T))
            elif seq_major_io:
                x, next_big = _ffn_call(x, small, big[layer], layer, rows=ffn_rows, final_norm=False,
                                        next_f32=big_f32)
                big.append(next_big)
            else:
                x = _ffn_call(x, small, big[layer], layer, rows=ffn_rows, final_norm=last)
            hs.append(h)
            rcs.append(_seq_major(rc, n_seq))
            scs.append(_seq_major(sc, n_seq))
        return x, jnp.stack(hs), jnp.stack(rcs), jnp.stack(scs)

    bp = x_prompt.shape[0]
    y_prompt, rnn_h_p, rnn_conv_p, sc_conv_p = trunk(
        x_prompt, jnp.zeros((DEPTH, bp, D_MODEL), dt), jnp.zeros((DEPTH, bp, RNN_CONV_W - 1, D_MODEL), dt),
        jnp.zeros((DEPTH, bp, SC_CONV_W - 1, D_MODEL), dt),
        n_seq=bp, n_t=PROMPT_N_T, ffn_rows=PROMPT_FFN_ROWS, meta=meta_tokens.astype(dt))

    bs, t_s, _ = x_sample.shape
    xs = jnp.swapaxes(x_sample, 0, 1).reshape(t_s * bs, D_MODEL)
    ys, rnn_h_s, rnn_conv_s, sc_conv_s = trunk(
        xs, state_rnn_h, state_rnn_conv, state_sc_conv, n_seq=bs, n_t=t_s, ffn_rows=t_s * bs)
    y_sample = jnp.swapaxes(ys.reshape(t_s, bs, D_MODEL), 0, 1)

    return (y_prompt, y_sample, rnn_h_p, rnn_conv_p, sc_conv_p, rnn_h_s, rnn_conv_s, sc_conv_s)
```

```python
import functools
import math

import jax
import jax.numpy as jnp
from jax import lax
from jax.experimental import pallas as pl
from jax.experimental.pallas import tpu as pltpu

D_MODEL = 1024
DEPTH = 4
N_META = 16
N_RNN_HEADS = 16
RNN_HEAD_DIM = D_MODEL // N_RNN_HEADS
RNN_CONV_W = 4
SC_CONV_W = 3
LRU_C = 8.0
D_FF = 2816
EPS = 1e-6
N_SPLITS = 7
GATE_CHUNK = 256
N_GATE_CHUNKS = D_MODEL // GATE_CHUNK
SUBLANES = 8

V7X_VMEM_BYTES = 64 * 1024 * 1024
VMEM_LIMIT_BYTES = V7X_VMEM_BYTES - 2 * 1024 * 1024

F32 = jnp.float32
BF16 = jnp.bfloat16


def _sigmoid_of_half(half_x):
    return 0.5 * jnp.tanh(half_x) + 0.5


def _gelu_tanh(x):
    c = math.sqrt(2.0 / math.pi)
    half_x = 0.5 * x
    return half_x * jnp.tanh(x * (c + (c * 0.044715) * (x * x))) + half_x


def _softplus(x):
    return jnp.maximum(x, 0.0) + jnp.log1p(jnp.exp(-jnp.abs(x)))


def _rmsnorm(x, g):
    ms = jnp.mean(x * x, axis=-1, keepdims=True)
    return x * lax.rsqrt(ms + EPS) * g


def _dot(a, b):
    return jnp.dot(a, b, preferred_element_type=F32)


def _mixer_body(read_x, step, h0_ref, rc0_ref, sc0_ref, g1_ref, win_ref, cw_ref, cb_ref, wgate_ref,
                ba_ref, bx_ref, lam_ref, wba_ref, scw_ref, wbb_ref, wout_ref,
                xo_ref, h_ref, rc_ref, sc_ref,
                u_bf, xrbuf, chbuf, a_buf, h_buf, ya_buf, yb_buf, ga_buf, gb_buf, *, n_seq, n_t):
    rows = n_seq * n_t
    rc_rows = (RNN_CONV_W - 1) * n_seq
    sc_rows = (SC_CONV_W - 1) * n_seq

    def proj(k):
        return _dot(u_bf[...], win_ref[:, k * D_MODEL:(k + 1) * D_MODEL])

    @pl.when(step == 0)
    def _():
        h_ref[...] = h0_ref[...]
        xrbuf[0:rc_rows, :] = rc0_ref[...]
        chbuf[0:sc_rows, :] = sc0_ref[...]

    u_bf[...] = _rmsnorm(read_x(), g1_ref[...]).astype(BF16)
    xrbuf[rc_rows:rc_rows + rows, :] = proj(0)
    xc = cb_ref[...] + xrbuf[0:rows, :] * cw_ref[0:1, :]
    for k in range(1, RNN_CONV_W):
        xc = xc + xrbuf[k * n_seq:k * n_seq + rows, :] * cw_ref[k:k + 1, :]
    h_buf[...] = xc
    xc_b = xc.astype(BF16)
    for j in range(N_GATE_CHUNKS):
        cols = slice(j * GATE_CHUNK, (j + 1) * GATE_CHUNK)
        gates = _dot(xc_b[:, cols], wgate_ref[j])
        ga_buf[:, cols] = gates[:, :GATE_CHUNK]
        gb_buf[:, cols] = gates[:, GATE_CHUNK:]
    chbuf[sc_rows:sc_rows + rows, :] = proj(3) * proj(4)

    @pl.when(step >= 0)
    def _():
        half_log_a_max = (-0.5 * LRU_C) * _softplus(-lam_ref[...])
        t_r = jnp.tanh(ga_buf[...] + 0.5 * ba_ref[...])
        log_a = t_r * half_log_a_max + half_log_a_max
        a = jnp.exp(log_a)
        one_minus_a2 = jnp.maximum(jnp.tanh(log_a) * (-1.0 - a * a), 1e-12)
        mult = one_minus_a2 * lax.rsqrt(one_minus_a2)
        gate_i = _sigmoid_of_half(gb_buf[...] + 0.5 * bx_ref[...])
        a_buf[...] = a
        h_buf[...] = mult * gate_i * h_buf[...]

        if n_seq == SUBLANES:
            def scan_step(t, h):
                sl = pl.ds(pl.multiple_of(t * n_seq, n_seq), n_seq)
                h = a_buf[sl, :] * h + h_buf[sl, :]
                h_buf[sl, :] = h
                return h
            h_ref[...] = lax.fori_loop(0, n_t, scan_step, h_ref[...], unroll=True)
        else:
            for t in range(n_t):
                sl = slice(t * n_seq, (t + 1) * n_seq)
                prev = h_ref[...] if t == 0 else h_buf[(t - 1) * n_seq:t * n_seq, :]
                h_buf[sl, :] = a_buf[sl, :] * prev + h_buf[sl, :]
            h_ref[...] = h_buf[(n_t - 1) * n_seq:n_t * n_seq, :]

        ya_buf[...] = (h_buf[...] * _gelu_tanh(proj(1))).astype(BF16)

        vc = chbuf[0:rows, :] * scw_ref[0:1, :]
        for k in range(1, SC_CONV_W):
            vc = vc + chbuf[k * n_seq:k * n_seq + rows, :] * scw_ref[k:k + 1, :]
        yb_buf[...] = (proj(2) * vc).astype(BF16)

        ga_buf[...] = _sigmoid_of_half(proj(5))
        gb_buf[...] = _sigmoid_of_half(proj(6))

        new_rc = xrbuf[rows:rows + rc_rows, :]
        new_sc = chbuf[rows:rows + sc_rows, :]
        xrbuf[0:rc_rows, :] = new_rc
        chbuf[0:sc_rows, :] = new_sc

        @pl.when(step == pl.num_programs(0) - 1)
        def _():
            rc_ref[...] = new_rc
            sc_ref[...] = new_sc

    @pl.when(step >= 0)
    def _():
        m = ga_buf[...] * _dot(ya_buf[...], wba_ref[...])
        m = m + gb_buf[...] * _dot(yb_buf[...], wbb_ref[...])
        xo_ref[...] = read_x() + _dot(m.astype(BF16), wout_ref[...])


def _mixer_kernel(x_ref, *refs, n_seq, n_t):
    _mixer_body(lambda: x_ref[...], pl.program_id(0), *refs, n_seq=n_seq, n_t=n_t)


def _seq_major_tile_copies(seq_hbm, tile_buf, sem, tile, slot, *, n_seq, n_t, to_hbm):
    copies = []
    for b in range(n_seq):
        if isinstance(tile, int):
            assert tile == 0
            hbm = seq_hbm.at[b, pl.ds(0, n_t - N_META), :]
            vmem = tile_buf.at[slot, pl.ds(N_META, n_t - N_META), b, :]
        else:
            t0 = pl.multiple_of(tile * n_t - N_META, SUBLANES)
            hbm = seq_hbm.at[b, pl.ds(t0, n_t), :]
            vmem = tile_buf.at[slot, :, b, :]
        src, dst = (vmem, hbm) if to_hbm else (hbm, vmem)
        copies.append(pltpu.make_async_copy(src, dst, sem.at[slot]))
    return copies


def _mixer_from_seq_major_kernel(xs_hbm, meta_ref, *refs, n_seq, n_t):
    *refs, xin, sem = refs
    n_in, n_out, n_cast = 3 + len(_MIXER_WEIGHTS), 4, len(_FFN_BIG_WEIGHTS)
    cast_src, cast_dst = refs[n_in:n_in + n_cast], refs[n_in + n_cast + n_out:n_in + 2 * n_cast + n_out]
    refs = refs[:n_in] + refs[n_in + n_cast:n_in + n_cast + n_out] + refs[n_in + 2 * n_cast + n_out:]
    step = pl.program_id(0)
    n_steps = pl.num_programs(0)
    slot = step % 2
    copies = functools.partial(_seq_major_tile_copies, xs_hbm, xin, sem, n_seq=n_seq, n_t=n_t, to_hbm=False)

    @pl.when(step == 0)
    def _():
        for c in copies(0, 0):
            c.start()
        xin[0, 0:N_META, :, :] = jnp.broadcast_to(meta_ref[...][:, None, :], (N_META, n_seq, D_MODEL))

        @pl.when(n_steps > 1)
        def _():
            for c in copies(step + 1, 1):
                c.start()
        for c in copies(0, 0):
            c.wait()

    @pl.when(step > 0)
    def _():
        @pl.when(step + 1 < n_steps)
        def _():
            for c in copies(step + 1, 1 - slot):
                c.start()
        for c in copies(step, slot):
            c.wait()

    _mixer_body(lambda: xin[slot].reshape(n_t * n_seq, D_MODEL), step, *refs, n_seq=n_seq, n_t=n_t)
    _cast_weight_blocks(_FFN_BIG_WEIGHTS, cast_src, cast_dst)


def _ffn_body(x, g2_ref, wg_ref, wu_ref, wd_ref, gf_ref, *, final_norm):
    v = _rmsnorm(x, g2_ref[...]).astype(BF16)
    half = _dot(v, wg_ref[...])
    hid = ((half * jnp.tanh(half) + half) * _dot(v, wu_ref[...])).astype(BF16)
    y = x + _dot(hid, wd_ref[...])
    if final_norm:
        y = _rmsnorm(y, gf_ref[...])
    return y


def _cast_weight_blocks(names, src_refs, dst_refs):
    @pl.when(pl.program_id(0) < N_CONVERT_STEPS)
    def _():
        for name, src, dst in zip(names, src_refs, dst_refs):
            if name == "w_in":
                plain = (N_SPLITS - 2) * D_MODEL
                dst[:, :plain] = src[:, :plain].astype(BF16)
                dst[:, plain:] = (0.5 * src[:, plain:]).astype(BF16)
            elif name == "w_ff_gate":
                dst[...] = (0.5 * src[...]).astype(BF16)
            else:
                dst[...] = src[...].astype(BF16)


def _ffn_kernel(x_ref, g2_ref, wg_ref, wu_ref, wd_ref, gf_ref, *rest, final_norm):
    n_big = (len(rest) - 1) // 2
    next_f32, o_ref, next_bf16 = rest[:n_big], rest[n_big], rest[n_big + 1:]
    o_ref[...] = _ffn_body(x_ref[...], g2_ref, wg_ref, wu_ref, wd_ref, gf_ref, final_norm=final_norm)
    if n_big:
        _cast_weight_blocks(_BIG_WEIGHTS, next_f32, next_bf16)


def _ffn_to_seq_major_kernel(x_ref, g2_ref, wg_ref, wu_ref, wd_ref, gf_ref, ys_hbm, yout, sem, *, n_seq, n_t):
    step = pl.program_id(0)
    n_steps = pl.num_programs(0)
    slot = step % 2
    copies = functools.partial(_seq_major_tile_copies, ys_hbm, yout, sem, n_seq=n_seq, n_t=n_t, to_hbm=True)
    y = _ffn_body(x_ref[...], g2_ref, wg_ref, wu_ref, wd_ref, gf_ref, final_norm=True)

    @pl.when(step == 1)
    def _():
        for c in copies(0, 0):
            c.wait()

    @pl.when(step > 1)
    def _():
        for c in copies(step - 1, 1 - slot):
            c.wait()

    yout[slot] = y.reshape(n_t, n_seq, D_MODEL)

    @pl.when(step == 0)
    def _():
        for c in copies(0, 0):
            c.start()

        @pl.when(n_steps == 1)
        def _():
            for c in copies(0, 0):
                c.wait()

    @pl.when(step > 0)
    def _():
        for c in copies(step, slot):
            c.start()

        @pl.when(step == n_steps - 1)
        def _():
            for c in copies(step, slot):
                c.wait()


def _resident(shape, index):
    return pl.BlockSpec(shape, index, pipeline_mode=pl.Buffered(1))


def _weight_operands(names, small, big, layer):
    specs, arrays = [], []
    for name in names:
        if name in big:
            arr = big[name]
            specs.append(_resident(arr.shape, lambda i, nd=arr.ndim: (0,) * nd))
        else:
            arr = small[name]
            specs.append(_resident((None,) + arr.shape[1:], lambda i, nd=arr.ndim: (layer,) + (0,) * (nd - 1)))
        arrays.append(arr)
    return specs, arrays


_MIXER_WEIGHTS = ("norm1_g", "w_in", "rnn_conv_w", "rnn_conv_b", "w_gate", "gate_a_b", "gate_x_b",
                  "lru_lambda", "w_branch_a", "sc_conv_w", "w_branch_b", "w_out")
_FFN_WEIGHTS = ("norm2_g", "w_ff_gate", "w_ff_up", "w_ff_down")
_BIG_WEIGHTS = ("w_in", "w_branch_a", "w_branch_b", "w_out", "w_ff_gate", "w_ff_up", "w_ff_down")
_FFN_BIG_WEIGHTS = _BIG_WEIGHTS[4:]
N_CONVERT_STEPS = 16
_COMPILER_PARAMS = pltpu.CompilerParams(dimension_semantics=("arbitrary",), vmem_limit_bytes=VMEM_LIMIT_BYTES)


def _cast_operands(names, stacked_f32, layer, n_steps):
    assert n_steps >= N_CONVERT_STEPS
    block = lambda i: jnp.minimum(i, N_CONVERT_STEPS - 1)
    in_specs, out_specs, out_shapes = [], [], []
    for name in names:
        _, n_in, n_out = stacked_f32[name].shape
        blk = n_in // N_CONVERT_STEPS
        assert blk * N_CONVERT_STEPS == n_in and blk % (2 * SUBLANES) == 0
        in_specs.append(pl.BlockSpec((None, blk, n_out), lambda i: (layer, block(i), 0)))
        out_specs.append(pl.BlockSpec((blk, n_out), lambda i: (block(i), 0)))
        out_shapes.append(jax.ShapeDtypeStruct((n_in, n_out), BF16))
    return in_specs, out_specs, out_shapes, [stacked_f32[name] for name in names]


def _mixer_call(x, h0, rc0, sc0, small, big, layer, *, n_seq, n_t, meta=None, ffn_f32=None):
    rows = n_seq * n_t
    rc_rows = (RNN_CONV_W - 1) * n_seq
    sc_rows = (SC_CONV_W - 1) * n_seq
    row_spec = pl.BlockSpec((rows, D_MODEL), lambda i: (i, 0))
    state_spec = lambda r: _resident((r, D_MODEL), lambda i: (0, 0))
    scratch = [pltpu.VMEM((rows, D_MODEL), BF16),
               pltpu.VMEM((rc_rows + rows, D_MODEL), F32),
               pltpu.VMEM((sc_rows + rows, D_MODEL), F32),
               pltpu.VMEM((rows, D_MODEL), F32),
               pltpu.VMEM((rows, D_MODEL), F32),
               pltpu.VMEM((rows, D_MODEL), BF16),
               pltpu.VMEM((rows, D_MODEL), BF16),
               pltpu.VMEM((rows, D_MODEL), F32),
               pltpu.VMEM((rows, D_MODEL), F32)]
    if meta is None:
        body, x_args, n_rows = _mixer_kernel, (x,), x.shape[0]
        x_specs = [row_spec]
    else:
        assert n_t % SUBLANES == 0 and n_t > N_META and n_seq == SUBLANES
        body, x_args, n_rows = _mixer_from_seq_major_kernel, (x, meta), (x.shape[1] + N_META) * n_seq
        x_specs = [pl.BlockSpec(memory_space=pl.ANY), _resident(meta.shape, lambda i: (0, 0))]
        scratch += [pltpu.VMEM((2, n_t, n_seq, D_MODEL), F32), pltpu.SemaphoreType.DMA((2,))]
    assert n_rows % rows == 0
    w_specs, w_arrays = _weight_operands(_MIXER_WEIGHTS, small, big, layer)
    cast_in, cast_out, cast_shapes, cast_arrays = ([], [], [], []) if meta is None else _cast_operands(
        _FFN_BIG_WEIGHTS, ffn_f32, layer, n_rows // rows)
    outs = pl.pallas_call(
        functools.partial(body, n_seq=n_seq, n_t=n_t),
        grid=(n_rows // rows,),
        in_specs=x_specs + [state_spec(n_seq), state_spec(rc_rows), state_spec(sc_rows)] + w_specs + cast_in,
        out_specs=[row_spec,
                   pl.BlockSpec((n_seq, D_MODEL), lambda i: (0, 0)),
                   pl.BlockSpec((rc_rows, D_MODEL), lambda i: (0, 0)),
                   pl.BlockSpec((sc_rows, D_MODEL), lambda i: (0, 0))] + cast_out,
        out_shape=[jax.ShapeDtypeStruct((n_rows, D_MODEL), F32),
                   jax.ShapeDtypeStruct((n_seq, D_MODEL), F32),
                   jax.ShapeDtypeStruct((rc_rows, D_MODEL), F32),
                   jax.ShapeDtypeStruct((sc_rows, D_MODEL), F32)] + cast_shapes,
        scratch_shapes=scratch,
        compiler_params=_COMPILER_PARAMS,
        name=f"mixer_s{n_seq}" + ("_in" if meta is not None else ""),
    )(*x_args, h0, rc0, sc0, *w_arrays, *cast_arrays)
    if meta is None:
        return outs
    return (*outs[:4], dict(zip(_FFN_BIG_WEIGHTS, outs[4:])))


def _ffn_call(x, small, big, layer, *, rows, final_norm, seq_major_out=None, next_f32=None):
    n_rows = x.shape[0]
    assert n_rows % rows == 0
    n_steps = n_rows // rows
    row_spec = pl.BlockSpec((rows, D_MODEL), lambda i: (i, 0))
    gf = small["final_norm_g"]
    w_specs, w_arrays = _weight_operands(_FFN_WEIGHTS, small, big, layer)
    in_specs = [row_spec] + w_specs + [_resident(gf.shape, lambda i: (0, 0))]
    args = (x, *w_arrays, gf)
    if seq_major_out is not None:
        n_seq, n_t = seq_major_out
        assert final_norm and next_f32 is None and rows == n_seq * n_t and n_t % SUBLANES == 0 and n_t > N_META
        return pl.pallas_call(
            functools.partial(_ffn_to_seq_major_kernel, n_seq=n_seq, n_t=n_t),
            grid=(n_steps,), in_specs=in_specs, out_specs=pl.BlockSpec(memory_space=pl.ANY),
            out_shape=jax.ShapeDtypeStruct((n_seq, n_rows // n_seq - N_META, D_MODEL), F32),
            scratch_shapes=[pltpu.VMEM((2, n_t, n_seq, D_MODEL), F32), pltpu.SemaphoreType.DMA((2,))],
            compiler_params=_COMPILER_PARAMS, name=f"ffn_r{rows}_out",
        )(*args)
    out_specs, out_shape = [row_spec], [jax.ShapeDtypeStruct((n_rows, D_MODEL), F32)]
    if next_f32 is not None:
        cast_in, cast_out, cast_shapes, cast_arrays = _cast_operands(_BIG_WEIGHTS, next_f32, layer + 1, n_steps)
        in_specs, out_specs, out_shape = in_specs + cast_in, out_specs + cast_out, out_shape + cast_shapes
        args += tuple(cast_arrays)
    outs = pl.pallas_call(
        functools.partial(_ffn_kernel, final_norm=final_norm),
        grid=(n_steps,), in_specs=in_specs, out_specs=out_specs, out_shape=out_shape,
        compiler_params=_COMPILER_PARAMS,
        name=f"ffn_r{rows}" + ("_cast" if next_f32 is not None else ""),
    )(*args)
    if next_f32 is None:
        return outs[0]
    return outs[0], dict(zip(_BIG_WEIGHTS, outs[1:]))


def _block_diag_gates(gate_a_w, gate_x_w):
    hpc = GATE_CHUNK // RNN_HEAD_DIM
    own_block = (jnp.arange(GATE_CHUNK) // RNN_HEAD_DIM)[None, :] == jnp.arange(hpc)[:, None]

    def bd(wt):
        wt = wt.reshape(DEPTH, N_GATE_CHUNKS, hpc, RNN_HEAD_DIM, RNN_HEAD_DIM)
        across = jnp.tile(wt, (1, 1, 1, 1, hpc))
        return jnp.where(own_block[:, None, :], across, 0.0).reshape(DEPTH, N_GATE_CHUNKS, GATE_CHUNK, GATE_CHUNK)

    return jnp.concatenate([bd(gate_a_w), bd(gate_x_w)], axis=-1).astype(BF16)


def _time_major(state):
    return jnp.swapaxes(state, 0, 1).reshape(-1, D_MODEL)


def _seq_major(state, n_seq):
    return jnp.swapaxes(state.reshape(-1, n_seq, D_MODEL), 0, 1)


PROMPT_N_T = 86
PROMPT_FFN_ROWS = 688
PROMPT_IO_N_T = 48


def kernel(x_prompt, x_sample, state_rnn_h, state_rnn_conv, state_sc_conv, meta_tokens, norm1_g, w_in, rnn_conv_w, rnn_conv_b, gate_a_w, gate_a_b, gate_x_w, gate_x_b, lru_lambda, w_branch_a, sc_conv_w, w_branch_b, w_out, norm2_g, w_ff_gate, w_ff_up, w_ff_down, final_norm_g):
    row = lambda p: p.reshape(DEPTH, 1, D_MODEL)
    small = {
        "norm1_g": row(norm1_g), "rnn_conv_w": rnn_conv_w, "rnn_conv_b": row(rnn_conv_b),
        "w_gate": _block_diag_gates(0.5 * gate_a_w, 0.5 * gate_x_w),
        "gate_a_b": row(gate_a_b), "gate_x_b": row(gate_x_b), "lru_lambda": row(lru_lambda),
        "sc_conv_w": sc_conv_w, "norm2_g": row(norm2_g), "final_norm_g": final_norm_g.reshape(1, D_MODEL),
    }
    big_f32 = {"w_in": w_in, "w_branch_a": w_branch_a, "w_branch_b": w_branch_b, "w_out": w_out,
               "w_ff_gate": w_ff_gate, "w_ff_up": w_ff_up, "w_ff_down": w_ff_down}
    merge_gate_half = jnp.concatenate([jnp.ones(((N_SPLITS - 2) * D_MODEL,), F32), jnp.full((2 * D_MODEL,), 0.5, F32)])
    pre_scale = {"w_in": merge_gate_half, "w_ff_gate": 0.5}
    big = [{n: (big_f32[n][0] * pre_scale.get(n, 1.0)).astype(BF16) for n in _BIG_WEIGHTS[:4]}]
    dt = x_prompt.dtype

    def trunk(x, h_st, rc_st, sc_st, *, n_seq, n_t, ffn_rows, meta=None):
        hs, rcs, scs = [], [], []
        seq_major_io = meta is not None
        for layer in range(DEPTH):
            first, last = layer == 0, layer == DEPTH - 1
            if seq_major_io and first:
                x, h, rc, sc, ffn_big = _mixer_call(
                    x, h_st[layer], _time_major(rc_st[layer]), _time_major(sc_st[layer]), small, big[layer], layer,
                    n_seq=n_seq, n_t=PROMPT_IO_N_T, meta=meta, ffn_f32=big_f32)
                big[layer].update(ffn_big)
            else:
                x, h, rc, sc = _mixer_call(
                    x, h_st[layer], _time_major(rc_st[layer]), _time_major(sc_st[layer]), small, big[layer], layer,
                    n_seq=n_seq, n_t=n_t)
            if seq_major_io and last:
                x = _ffn_call(x, small, big[layer], layer, rows=n_seq * PROMPT_IO_N_T, final_norm=True,
                              seq_major_out=(n_seq, PROMPT_IO_N_T))
            elif seq_major_io:
                x, next_big = _ffn_call(x, small, big[layer], layer, rows=ffn_rows, final_norm=False,
                                        next_f32=big_f32)
                big.append(next_big)
            else:
                x = _ffn_call(x, small, big[layer], layer, rows=ffn_rows, final_norm=last)
            hs.append(h)
            rcs.append(_seq_major(rc, n_seq))
            scs.append(_seq_major(sc, n_seq))
        return x, jnp.stack(hs), jnp.stack(rcs), jnp.stack(scs)

    bp = x_prompt.shape[0]
    y_prompt, rnn_h_p, rnn_conv_p, sc_conv_p = trunk(
        x_prompt, jnp.zeros((DEPTH, bp, D_MODEL), dt), jnp.zeros((DEPTH, bp, RNN_CONV_W - 1, D_MODEL), dt),
        jnp.zeros((DEPTH, bp, SC_CONV_W - 1, D_MODEL), dt),
        n_seq=bp, n_t=PROMPT_N_T, ffn_rows=PROMPT_FFN_ROWS, meta=meta_tokens.astype(dt))

    bs, t_s, _ = x_sample.shape
    xs = jnp.swapaxes(x_sample, 0, 1).reshape(t_s * bs, D_MODEL)
    ys, rnn_h_s, rnn_conv_s, sc_conv_s = trunk(
        xs, state_rnn_h, state_rnn_conv, state_sc_conv, n_seq=bs, n_t=t_s, ffn_rows=t_s * bs)
    y_sample = jnp.swapaxes(ys.reshape(t_s, bs, D_MODEL), 0, 1)

    return (y_prompt, y_sample, rnn_h_p, rnn_conv_p, sc_conv_p, rnn_h_s, rnn_conv_s, sc_conv_s)
```

```python
import functools
import math

import jax
import jax.numpy as jnp
from jax import lax
from jax.experimental import pallas as pl
from jax.experimental.pallas import tpu as pltpu

D_MODEL = 1024
DEPTH = 4
N_META = 16
N_RNN_HEADS = 16
RNN_HEAD_DIM = D_MODEL // N_RNN_HEADS
RNN_CONV_W = 4
SC_CONV_W = 3
LRU_C = 8.0
D_FF = 2816
EPS = 1e-6
N_SPLITS = 7
GATE_CHUNK = 256
N_GATE_CHUNKS = D_MODEL // GATE_CHUNK
SUBLANES = 8

V7X_VMEM_BYTES = 64 * 1024 * 1024
VMEM_LIMIT_BYTES = V7X_VMEM_BYTES - 2 * 1024 * 1024

F32 = jnp.float32
BF16 = jnp.bfloat16


def _sigmoid_of_half(half_x):
    return 0.5 * jnp.tanh(half_x) + 0.5


def _gelu_tanh(x):
    c = math.sqrt(2.0 / math.pi)
    half_x = 0.5 * x
    return half_x * jnp.tanh(x * (c + (c * 0.044715) * (x * x))) + half_x


def _softplus(x):
    return jnp.maximum(x, 0.0) + jnp.log1p(jnp.exp(-jnp.abs(x)))


def _rmsnorm(x, g):
    ms = jnp.mean(x * x, axis=-1, keepdims=True)
    return x * lax.rsqrt(ms + EPS) * g


def _dot(a, b):
    return jnp.dot(a, b, preferred_element_type=F32)


def _mixer_body(read_x, step, h0_ref, rc0_ref, sc0_ref, g1_ref, win_ref, cw_ref, cb_ref, wgate_ref,
                ba_ref, bx_ref, lam_ref, wba_ref, scw_ref, wbb_ref, wout_ref,
                xo_ref, h_ref, rc_ref, sc_ref,
                u_bf, xrbuf, chbuf, a_buf, h_buf, ya_buf, yb_buf, ga_buf, gb_buf, *, n_seq, n_t, layer):
    rows = n_seq * n_t
    rc_rows = (RNN_CONV_W - 1) * n_seq
    sc_rows = (SC_CONV_W - 1) * n_seq
    row = slice(layer, layer + 1)

    def proj(k):
        return _dot(u_bf[...], win_ref[:, k * D_MODEL:(k + 1) * D_MODEL])

    @pl.when(step == 0)
    def _():
        h_ref[...] = h0_ref[...]
        xrbuf[0:rc_rows, :] = rc0_ref[...]
        chbuf[0:sc_rows, :] = sc0_ref[...]

    u_bf[...] = _rmsnorm(read_x(), g1_ref[row, :]).astype(BF16)
    xrbuf[rc_rows:rc_rows + rows, :] = proj(0)
    xc = cb_ref[row, :] + xrbuf[0:rows, :] * cw_ref[0:1, :]
    for k in range(1, RNN_CONV_W):
        xc = xc + xrbuf[k * n_seq:k * n_seq + rows, :] * cw_ref[k:k + 1, :]
    h_buf[...] = xc
    xc_b = xc.astype(BF16)
    for j in range(N_GATE_CHUNKS):
        cols = slice(j * GATE_CHUNK, (j + 1) * GATE_CHUNK)
        gates = _dot(xc_b[:, cols], wgate_ref[j])
        ga_buf[:, cols] = gates[:, :GATE_CHUNK]
        gb_buf[:, cols] = gates[:, GATE_CHUNK:]
    chbuf[sc_rows:sc_rows + rows, :] = proj(3) * proj(4)

    @pl.when(step >= 0)
    def _():
        half_log_a_max = (-0.5 * LRU_C) * _softplus(-lam_ref[row, :])
        t_r = jnp.tanh(ga_buf[...] + 0.5 * ba_ref[row, :])
        log_a = t_r * half_log_a_max + half_log_a_max
        a = jnp.exp(log_a)
        one_minus_a2 = jnp.maximum(jnp.tanh(log_a) * (-1.0 - a * a), 1e-12)
        mult = one_minus_a2 * lax.rsqrt(one_minus_a2)
        gate_i = _sigmoid_of_half(gb_buf[...] + 0.5 * bx_ref[row, :])
        a_buf[...] = a
        h_buf[...] = mult * gate_i * h_buf[...]

        if n_seq == SUBLANES:
            def scan_step(t, h):
                sl = pl.ds(pl.multiple_of(t * n_seq, n_seq), n_seq)
                h = a_buf[sl, :] * h + h_buf[sl, :]
                h_buf[sl, :] = h
                return h
            h_ref[...] = lax.fori_loop(0, n_t, scan_step, h_ref[...], unroll=True)
        else:
            for t in range(n_t):
                sl = slice(t * n_seq, (t + 1) * n_seq)
                prev = h_ref[...] if t == 0 else h_buf[(t - 1) * n_seq:t * n_seq, :]
                h_buf[sl, :] = a_buf[sl, :] * prev + h_buf[sl, :]
            h_ref[...] = h_buf[(n_t - 1) * n_seq:n_t * n_seq, :]

        ya_buf[...] = (h_buf[...] * _gelu_tanh(proj(1))).astype(BF16)

        vc = chbuf[0:rows, :] * scw_ref[0:1, :]
        for k in range(1, SC_CONV_W):
            vc = vc + chbuf[k * n_seq:k * n_seq + rows, :] * scw_ref[k:k + 1, :]
        yb_buf[...] = (proj(2) * vc).astype(BF16)

        ga_buf[...] = _sigmoid_of_half(proj(5))
        gb_buf[...] = _sigmoid_of_half(proj(6))

        new_rc = xrbuf[rows:rows + rc_rows, :]
        new_sc = chbuf[rows:rows + sc_rows, :]
        xrbuf[0:rc_rows, :] = new_rc
        chbuf[0:sc_rows, :] = new_sc

        @pl.when(step == pl.num_programs(0) - 1)
        def _():
            rc_ref[...] = new_rc
            sc_ref[...] = new_sc

    @pl.when(step >= 0)
    def _():
        m = ga_buf[...] * _dot(ya_buf[...], wba_ref[...])
        m = m + gb_buf[...] * _dot(yb_buf[...], wbb_ref[...])
        xo_ref[...] = read_x() + _dot(m.astype(BF16), wout_ref[...])


def _mixer_kernel(x_ref, *refs, n_seq, n_t, layer):
    _mixer_body(lambda: x_ref[...], pl.program_id(0), *refs, n_seq=n_seq, n_t=n_t, layer=layer)


def _seq_major_tile_copies(seq_hbm, tile_buf, sem, tile, slot, *, n_seq, n_t, to_hbm):
    copies = []
    for b in range(n_seq):
        if isinstance(tile, int):
            assert tile == 0
            hbm = seq_hbm.at[b, pl.ds(0, n_t - N_META), :]
            vmem = tile_buf.at[slot, pl.ds(N_META, n_t - N_META), b, :]
        else:
            t0 = pl.multiple_of(tile * n_t - N_META, SUBLANES)
            hbm = seq_hbm.at[b, pl.ds(t0, n_t), :]
            vmem = tile_buf.at[slot, :, b, :]
        src, dst = (vmem, hbm) if to_hbm else (hbm, vmem)
        copies.append(pltpu.make_async_copy(src, dst, sem.at[slot]))
    return copies


def _mixer_from_seq_major_kernel(xs_hbm, meta_ref, *refs, n_seq, n_t, layer):
    *refs, xin, sem = refs
    n_in, n_out, n_cast = 3 + len(_MIXER_WEIGHTS), 4, len(_FFN_BIG_WEIGHTS)
    cast_src, cast_dst = refs[n_in:n_in + n_cast], refs[n_in + n_cast + n_out:n_in + 2 * n_cast + n_out]
    refs = refs[:n_in] + refs[n_in + n_cast:n_in + n_cast + n_out] + refs[n_in + 2 * n_cast + n_out:]
    step = pl.program_id(0)
    n_steps = pl.num_programs(0)
    slot = step % 2
    copies = functools.partial(_seq_major_tile_copies, xs_hbm, xin, sem, n_seq=n_seq, n_t=n_t, to_hbm=False)

    @pl.when(step == 0)
    def _():
        for c in copies(0, 0):
            c.start()
        xin[0, 0:N_META, :, :] = jnp.broadcast_to(meta_ref[...][:, None, :], (N_META, n_seq, D_MODEL))

        @pl.when(n_steps > 1)
        def _():
            for c in copies(step + 1, 1):
                c.start()
        for c in copies(0, 0):
            c.wait()

    @pl.when(step > 0)
    def _():
        @pl.when(step + 1 < n_steps)
        def _():
            for c in copies(step + 1, 1 - slot):
                c.start()
        for c in copies(step, slot):
            c.wait()

    _mixer_body(lambda: xin[slot].reshape(n_t * n_seq, D_MODEL), step, *refs, n_seq=n_seq, n_t=n_t, layer=layer)
    _cast_weight_blocks(_FFN_BIG_WEIGHTS, cast_src, cast_dst)


def _ffn_body(x, g2_ref, wg_ref, wu_ref, wd_ref, gf_ref, *, final_norm, layer):
    v = _rmsnorm(x, g2_ref[layer:layer + 1, :]).astype(BF16)
    half = _dot(v, wg_ref[...])
    hid = ((half * jnp.tanh(half) + half) * _dot(v, wu_ref[...])).astype(BF16)
    y = x + _dot(hid, wd_ref[...])
    if final_norm:
        y = _rmsnorm(y, gf_ref[...])
    return y


def _cast_weight_blocks(names, src_refs, dst_refs):
    @pl.when(pl.program_id(0) < N_CONVERT_STEPS)
    def _():
        for name, src, dst in zip(names, src_refs, dst_refs):
            if name == "w_in":
                plain = (N_SPLITS - 2) * D_MODEL
                dst[:, :plain] = src[:, :plain].astype(BF16)
                dst[:, plain:] = (0.5 * src[:, plain:]).astype(BF16)
            elif name == "w_ff_gate":
                dst[...] = (0.5 * src[...]).astype(BF16)
            else:
                dst[...] = src[...].astype(BF16)


def _ffn_kernel(x_ref, g2_ref, wg_ref, wu_ref, wd_ref, gf_ref, *rest, final_norm, layer):
    n_big = (len(rest) - 1) // 2
    next_f32, o_ref, next_bf16 = rest[:n_big], rest[n_big], rest[n_big + 1:]
    o_ref[...] = _ffn_body(x_ref[...], g2_ref, wg_ref, wu_ref, wd_ref, gf_ref, final_norm=final_norm, layer=layer)
    if n_big:
        _cast_weight_blocks(_BIG_WEIGHTS, next_f32, next_bf16)


def _ffn_to_seq_major_kernel(x_ref, g2_ref, wg_ref, wu_ref, wd_ref, gf_ref, ys_hbm, yout, sem, *, n_seq, n_t, layer):
    step = pl.program_id(0)
    n_steps = pl.num_programs(0)
    slot = step % 2
    copies = functools.partial(_seq_major_tile_copies, ys_hbm, yout, sem, n_seq=n_seq, n_t=n_t, to_hbm=True)
    y = _ffn_body(x_ref[...], g2_ref, wg_ref, wu_ref, wd_ref, gf_ref, final_norm=True, layer=layer)

    @pl.when(step == 1)
    def _():
        for c in copies(0, 0):
            c.wait()

    @pl.when(step > 1)
    def _():
        for c in copies(step - 1, 1 - slot):
            c.wait()

    yout[slot] = y.reshape(n_t, n_seq, D_MODEL)

    @pl.when(step == 0)
    def _():
        for c in copies(0, 0):
            c.start()

        @pl.when(n_steps == 1)
        def _():
            for c in copies(0, 0):
                c.wait()

    @pl.when(step > 0)
    def _():
        for c in copies(step, slot):
            c.start()

        @pl.when(step == n_steps - 1)
        def _():
            for c in copies(step, slot):
                c.wait()


def _resident(shape, index):
    return pl.BlockSpec(shape, index, pipeline_mode=pl.Buffered(1))


def _weight_operands(names, small, big, layer):
    specs, arrays = [], []
    for name in names:
        arr = big[name] if name in big else small[name]
        if name in big or arr.ndim == 2:
            specs.append(_resident(arr.shape, lambda i, nd=arr.ndim: (0,) * nd))
        else:
            specs.append(_resident((None,) + arr.shape[1:], lambda i, nd=arr.ndim: (layer,) + (0,) * (nd - 1)))
        arrays.append(arr)
    return specs, arrays


_MIXER_WEIGHTS = ("norm1_g", "w_in", "rnn_conv_w", "rnn_conv_b", "w_gate", "gate_a_b", "gate_x_b",
                  "lru_lambda", "w_branch_a", "sc_conv_w", "w_branch_b", "w_out")
_FFN_WEIGHTS = ("norm2_g", "w_ff_gate", "w_ff_up", "w_ff_down")
_BIG_WEIGHTS = ("w_in", "w_branch_a", "w_branch_b", "w_out", "w_ff_gate", "w_ff_up", "w_ff_down")
_FFN_BIG_WEIGHTS = _BIG_WEIGHTS[4:]
N_CONVERT_STEPS = 16
_COMPILER_PARAMS = pltpu.CompilerParams(dimension_semantics=("arbitrary",), vmem_limit_bytes=VMEM_LIMIT_BYTES)


def _cast_operands(names, stacked_f32, layer, n_steps):
    assert n_steps >= N_CONVERT_STEPS
    block = lambda i: jnp.minimum(i, N_CONVERT_STEPS - 1)
    in_specs, out_specs, out_shapes = [], [], []
    for name in names:
        _, n_in, n_out = stacked_f32[name].shape
        blk = n_in // N_CONVERT_STEPS
        assert blk * N_CONVERT_STEPS == n_in and blk % (2 * SUBLANES) == 0
        in_specs.append(pl.BlockSpec((None, blk, n_out), lambda i: (layer, block(i), 0)))
        out_specs.append(pl.BlockSpec((blk, n_out), lambda i: (block(i), 0)))
        out_shapes.append(jax.ShapeDtypeStruct((n_in, n_out), BF16))
    return in_specs, out_specs, out_shapes, [stacked_f32[name] for name in names]


def _mixer_call(x, h0, rc0, sc0, small, big, layer, *, n_seq, n_t, meta=None, ffn_f32=None):
    rows = n_seq * n_t
    rc_rows = (RNN_CONV_W - 1) * n_seq
    sc_rows = (SC_CONV_W - 1) * n_seq
    row_spec = pl.BlockSpec((rows, D_MODEL), lambda i: (i, 0))
    state_layer = layer if h0.shape[0] > 1 else 0
    state_spec = lambda r: _resident((None, r, D_MODEL), lambda i: (state_layer, 0, 0))
    scratch = [pltpu.VMEM((rows, D_MODEL), BF16),
               pltpu.VMEM((rc_rows + rows, D_MODEL), F32),
               pltpu.VMEM((sc_rows + rows, D_MODEL), F32),
               pltpu.VMEM((rows, D_MODEL), F32),
               pltpu.VMEM((rows, D_MODEL), F32),
               pltpu.VMEM((rows, D_MODEL), BF16),
               pltpu.VMEM((rows, D_MODEL), BF16),
               pltpu.VMEM((rows, D_MODEL), F32),
               pltpu.VMEM((rows, D_MODEL), F32)]
    if meta is None:
        body, x_args, n_rows = _mixer_kernel, (x,), x.shape[0]
        x_specs = [row_spec]
    else:
        assert n_t % SUBLANES == 0 and n_t > N_META and n_seq == SUBLANES
        body, x_args, n_rows = _mixer_from_seq_major_kernel, (x, meta), (x.shape[1] + N_META) * n_seq
        x_specs = [pl.BlockSpec(memory_space=pl.ANY), _resident(meta.shape, lambda i: (0, 0))]
        scratch += [pltpu.VMEM((2, n_t, n_seq, D_MODEL), F32), pltpu.SemaphoreType.DMA((2,))]
    assert n_rows % rows == 0
    w_specs, w_arrays = _weight_operands(_MIXER_WEIGHTS, small, big, layer)
    cast_in, cast_out, cast_shapes, cast_arrays = ([], [], [], []) if meta is None else _cast_operands(
        _FFN_BIG_WEIGHTS, ffn_f32, layer, n_rows // rows)
    outs = pl.pallas_call(
        functools.partial(body, n_seq=n_seq, n_t=n_t, layer=layer),
        grid=(n_rows // rows,),
        in_specs=x_specs + [state_spec(n_seq), state_spec(rc_rows), state_spec(sc_rows)] + w_specs + cast_in,
        out_specs=[row_spec,
                   pl.BlockSpec((n_seq, D_MODEL), lambda i: (0, 0)),
                   pl.BlockSpec((rc_rows, D_MODEL), lambda i: (0, 0)),
                   pl.BlockSpec((sc_rows, D_MODEL), lambda i: (0, 0))] + cast_out,
        out_shape=[jax.ShapeDtypeStruct((n_rows, D_MODEL), F32),
                   jax.ShapeDtypeStruct((n_seq, D_MODEL), F32),
                   jax.ShapeDtypeStruct((rc_rows, D_MODEL), F32),
                   jax.ShapeDtypeStruct((sc_rows, D_MODEL), F32)] + cast_shapes,
        scratch_shapes=scratch,
        compiler_params=_COMPILER_PARAMS,
        name=f"mixer_s{n_seq}" + ("_in" if meta is not None else ""),
    )(*x_args, h0, rc0, sc0, *w_arrays, *cast_arrays)
    if meta is None:
        return outs
    return (*outs[:4], dict(zip(_FFN_BIG_WEIGHTS, outs[4:])))


def _ffn_call(x, small, big, layer, *, rows, final_norm, seq_major_out=None, next_f32=None):
    n_rows = x.shape[0]
    assert n_rows % rows == 0
    n_steps = n_rows // rows
    row_spec = pl.BlockSpec((rows, D_MODEL), lambda i: (i, 0))
    gf = small["final_norm_g"]
    w_specs, w_arrays = _weight_operands(_FFN_WEIGHTS, small, big, layer)
    in_specs = [row_spec] + w_specs + [_resident(gf.shape, lambda i: (0, 0))]
    args = (x, *w_arrays, gf)
    if seq_major_out is not None:
        n_seq, n_t = seq_major_out
        assert final_norm and next_f32 is None and rows == n_seq * n_t and n_t % SUBLANES == 0 and n_t > N_META
        return pl.pallas_call(
            functools.partial(_ffn_to_seq_major_kernel, n_seq=n_seq, n_t=n_t, layer=layer),
            grid=(n_steps,), in_specs=in_specs, out_specs=pl.BlockSpec(memory_space=pl.ANY),
            out_shape=jax.ShapeDtypeStruct((n_seq, n_rows // n_seq - N_META, D_MODEL), F32),
            scratch_shapes=[pltpu.VMEM((2, n_t, n_seq, D_MODEL), F32), pltpu.SemaphoreType.DMA((2,))],
            compiler_params=_COMPILER_PARAMS, name=f"ffn_r{rows}_out",
        )(*args)
    out_specs, out_shape = [row_spec], [jax.ShapeDtypeStruct((n_rows, D_MODEL), F32)]
    if next_f32 is not None:
        cast_in, cast_out, cast_shapes, cast_arrays = _cast_operands(_BIG_WEIGHTS, next_f32, layer + 1, n_steps)
        in_specs, out_specs, out_shape = in_specs + cast_in, out_specs + cast_out, out_shape + cast_shapes
        args += tuple(cast_arrays)
    outs = pl.pallas_call(
        functools.partial(_ffn_kernel, final_norm=final_norm, layer=layer),
        grid=(n_steps,), in_specs=in_specs, out_specs=out_specs, out_shape=out_shape,
        compiler_params=_COMPILER_PARAMS,
        name=f"ffn_r{rows}" + ("_cast" if next_f32 is not None else ""),
    )(*args)
    if next_f32 is None:
        return outs[0]
    return outs[0], dict(zip(_BIG_WEIGHTS, outs[1:]))


def _block_diag_gates(gate_a_w, gate_x_w):
    hpc = GATE_CHUNK // RNN_HEAD_DIM
    own_block = (jnp.arange(GATE_CHUNK) // RNN_HEAD_DIM)[None, :] == jnp.arange(hpc)[:, None]

    def bd(wt):
        wt = wt.reshape(DEPTH, N_GATE_CHUNKS, hpc, RNN_HEAD_DIM, RNN_HEAD_DIM)
        across = jnp.tile(wt, (1, 1, 1, 1, hpc))
        return jnp.where(own_block[:, None, :], across, 0.0).reshape(DEPTH, N_GATE_CHUNKS, GATE_CHUNK, GATE_CHUNK)

    return jnp.concatenate([bd(gate_a_w), bd(gate_x_w)], axis=-1).astype(BF16)


def _time_major(states):
    return jnp.swapaxes(states, 1, 2).reshape(states.shape[0], -1, D_MODEL)


def _seq_major(states, n_seq):
    return jnp.swapaxes(jnp.stack(states).reshape(len(states), -1, n_seq, D_MODEL), 1, 2)


PROMPT_N_T = 86
PROMPT_FFN_ROWS = 688
PROMPT_IO_N_T = 48


def kernel(x_prompt, x_sample, state_rnn_h, state_rnn_conv, state_sc_conv, meta_tokens, norm1_g, w_in, rnn_conv_w, rnn_conv_b, gate_a_w, gate_a_b, gate_x_w, gate_x_b, lru_lambda, w_branch_a, sc_conv_w, w_branch_b, w_out, norm2_g, w_ff_gate, w_ff_up, w_ff_down, final_norm_g):
    small = {
        "norm1_g": norm1_g, "rnn_conv_w": rnn_conv_w, "rnn_conv_b": rnn_conv_b,
        "w_gate": _block_diag_gates(0.5 * gate_a_w, 0.5 * gate_x_w),
        "gate_a_b": gate_a_b, "gate_x_b": gate_x_b, "lru_lambda": lru_lambda,
        "sc_conv_w": sc_conv_w, "norm2_g": norm2_g, "final_norm_g": final_norm_g.reshape(1, D_MODEL),
    }
    big_f32 = {"w_in": w_in, "w_branch_a": w_branch_a, "w_branch_b": w_branch_b, "w_out": w_out,
               "w_ff_gate": w_ff_gate, "w_ff_up": w_ff_up, "w_ff_down": w_ff_down}
    merge_gate_half = jnp.concatenate([jnp.ones(((N_SPLITS - 2) * D_MODEL,), F32), jnp.full((2 * D_MODEL,), 0.5, F32)])
    pre_scale = {"w_in": merge_gate_half, "w_ff_gate": 0.5}
    big = [{n: (big_f32[n][0] * pre_scale.get(n, 1.0)).astype(BF16) for n in _BIG_WEIGHTS[:4]}]
    dt = x_prompt.dtype

    def trunk(x, h_st, rc_st, sc_st, *, n_seq, n_t, ffn_rows, meta=None):
        hs, rcs, scs = [], [], []
        seq_major_io = meta is not None
        for layer in range(DEPTH):
            first, last = layer == 0, layer == DEPTH - 1
            if seq_major_io and first:
                x, h, rc, sc, ffn_big = _mixer_call(x, h_st, rc_st, sc_st, small, big[layer], layer, n_seq=n_seq,
                                                    n_t=PROMPT_IO_N_T, meta=meta, ffn_f32=big_f32)
                big[layer].update(ffn_big)
            else:
                x, h, rc, sc = _mixer_call(x, h_st, rc_st, sc_st, small, big[layer], layer, n_seq=n_seq, n_t=n_t)
            if seq_major_io and last:
                x = _ffn_call(x, small, big[layer], layer, rows=n_seq * PROMPT_IO_N_T, final_norm=True,
                              seq_major_out=(n_seq, PROMPT_IO_N_T))
            elif seq_major_io:
                x, next_big = _ffn_call(x, small, big[layer], layer, rows=ffn_rows, final_norm=False,
                                        next_f32=big_f32)
                big.append(next_big)
            else:
                x = _ffn_call(x, small, big[layer], layer, rows=ffn_rows, final_norm=last)
            hs.append(h)
            rcs.append(rc)
            scs.append(sc)
        return x, jnp.stack(hs), _seq_major(rcs, n_seq), _seq_major(scs, n_seq)

    bp = x_prompt.shape[0]
    zeros = lambda k: jnp.zeros((1, k * bp, D_MODEL), dt)
    y_prompt, rnn_h_p, rnn_conv_p, sc_conv_p = trunk(
        x_prompt, zeros(1), zeros(RNN_CONV_W - 1), zeros(SC_CONV_W - 1),
        n_seq=bp, n_t=PROMPT_N_T, ffn_rows=PROMPT_FFN_ROWS, meta=meta_tokens.astype(dt))

    bs, t_s, _ = x_sample.shape
    xs = jnp.swapaxes(x_sample, 0, 1).reshape(t_s * bs, D_MODEL)
    ys, rnn_h_s, rnn_conv_s, sc_conv_s = trunk(
        xs, state_rnn_h, _time_major(state_rnn_conv), _time_major(state_sc_conv),
        n_seq=bs, n_t=t_s, ffn_rows=t_s * bs)
    y_sample = jnp.swapaxes(ys.reshape(t_s, bs, D_MODEL), 0, 1)

    return (y_prompt, y_sample, rnn_h_p, rnn_conv_p, sc_conv_p, rnn_h_s, rnn_conv_s, sc_conv_s)
```

```python
import functools
import math

import jax
import jax.numpy as jnp
from jax import lax
from jax.experimental import pallas as pl
from jax.experimental.pallas import tpu as pltpu

D_MODEL = 1024
DEPTH = 4
N_META = 16
N_RNN_HEADS = 16
RNN_HEAD_DIM = D_MODEL // N_RNN_HEADS
RNN_CONV_W = 4
SC_CONV_W = 3
LRU_C = 8.0
D_FF = 2816
EPS = 1e-6
N_SPLITS = 7
GATE_CHUNK = 256
N_GATE_CHUNKS = D_MODEL // GATE_CHUNK
SUBLANES = 8

V7X_VMEM_BYTES = 64 * 1024 * 1024
VMEM_LIMIT_BYTES = V7X_VMEM_BYTES - 2 * 1024 * 1024

F32 = jnp.float32
BF16 = jnp.bfloat16


def _sigmoid_of_half(half_x):
    return 0.5 * jnp.tanh(half_x) + 0.5


def _gelu_tanh(x):
    c = math.sqrt(2.0 / math.pi)
    half_x = 0.5 * x
    return half_x * jnp.tanh(x * (c + (c * 0.044715) * (x * x))) + half_x


def _softplus(x):
    return jnp.maximum(x, 0.0) + jnp.log1p(jnp.exp(-jnp.abs(x)))


def _rmsnorm(x, g):
    ms = jnp.mean(x * x, axis=-1, keepdims=True)
    return x * lax.rsqrt(ms + EPS) * g


def _dot(a, b):
    return jnp.dot(a, b, preferred_element_type=F32)


def _mixer_body(read_x, step, h0_ref, rc0_ref, sc0_ref, g1_ref, win_ref, cw_ref, cb_ref, wgate_ref,
                ba_ref, bx_ref, lam_ref, wba_ref, scw_ref, wbb_ref, wout_ref,
                xo_ref, h_ref, rc_ref, sc_ref,
                u_bf, xrbuf, chbuf, a_buf, h_buf, ya_buf, yb_buf, ga_buf, gb_buf, *, n_seq, n_t, layer):
    rows = n_seq * n_t
    rc_rows = (RNN_CONV_W - 1) * n_seq
    sc_rows = (SC_CONV_W - 1) * n_seq
    row = slice(layer, layer + 1)

    def proj(k):
        return _dot(u_bf[...], win_ref[:, k * D_MODEL:(k + 1) * D_MODEL])

    @pl.when(step == 0)
    def _():
        h_ref[...] = h0_ref[...]
        xrbuf[0:rc_rows, :] = rc0_ref[...]
        chbuf[0:sc_rows, :] = sc0_ref[...]

    u_bf[...] = _rmsnorm(read_x(), g1_ref[row, :]).astype(BF16)
    xrbuf[rc_rows:rc_rows + rows, :] = proj(0)
    xc = cb_ref[row, :] + xrbuf[0:rows, :] * cw_ref[0:1, :]
    for k in range(1, RNN_CONV_W):
        xc = xc + xrbuf[k * n_seq:k * n_seq + rows, :] * cw_ref[k:k + 1, :]
    h_buf[...] = xc
    xc_b = xc.astype(BF16)
    for j in range(N_GATE_CHUNKS):
        cols = slice(j * GATE_CHUNK, (j + 1) * GATE_CHUNK)
        gates = _dot(xc_b[:, cols], wgate_ref[j])
        ga_buf[:, cols] = gates[:, :GATE_CHUNK]
        gb_buf[:, cols] = gates[:, GATE_CHUNK:]
    chbuf[sc_rows:sc_rows + rows, :] = proj(3) * proj(4)

    @pl.when(step >= 0)
    def _():
        half_log_a_max = (-0.5 * LRU_C) * _softplus(-lam_ref[row, :])
        t_r = jnp.tanh(ga_buf[...] + 0.5 * ba_ref[row, :])
        log_a = t_r * half_log_a_max + half_log_a_max
        a = jnp.exp(log_a)
        one_minus_a2 = jnp.maximum(jnp.tanh(log_a) * (-1.0 - a * a), 1e-12)
        mult = one_minus_a2 * lax.rsqrt(one_minus_a2)
        gate_i = _sigmoid_of_half(gb_buf[...] + 0.5 * bx_ref[row, :])
        a_buf[...] = a
        h_buf[...] = mult * gate_i * h_buf[...]

        if n_seq == SUBLANES:
            def scan_step(t, h):
                sl = pl.ds(pl.multiple_of(t * n_seq, n_seq), n_seq)
                h = a_buf[sl, :] * h + h_buf[sl, :]
                h_buf[sl, :] = h
                return h
            h_ref[...] = lax.fori_loop(0, n_t, scan_step, h_ref[...], unroll=True)
        else:
            for t in range(n_t):
                sl = slice(t * n_seq, (t + 1) * n_seq)
                prev = h_ref[...] if t == 0 else h_buf[(t - 1) * n_seq:t * n_seq, :]
                h_buf[sl, :] = a_buf[sl, :] * prev + h_buf[sl, :]
            h_ref[...] = h_buf[(n_t - 1) * n_seq:n_t * n_seq, :]

        ya_buf[...] = (h_buf[...] * _gelu_tanh(proj(1))).astype(BF16)

        vc = chbuf[0:rows, :] * scw_ref[0:1, :]
        for k in range(1, SC_CONV_W):
            vc = vc + chbuf[k * n_seq:k * n_seq + rows, :] * scw_ref[k:k + 1, :]
        yb_buf[...] = (proj(2) * vc).astype(BF16)

        ga_buf[...] = _sigmoid_of_half(proj(5))
        gb_buf[...] = _sigmoid_of_half(proj(6))

        new_rc = xrbuf[rows:rows + rc_rows, :]
        new_sc = chbuf[rows:rows + sc_rows, :]
        xrbuf[0:rc_rows, :] = new_rc
        chbuf[0:sc_rows, :] = new_sc

        @pl.when(step == pl.num_programs(0) - 1)
        def _():
            rc_ref[...] = new_rc
            sc_ref[...] = new_sc

    @pl.when(step >= 0)
    def _():
        m = ga_buf[...] * _dot(ya_buf[...], wba_ref[...])
        m = m + gb_buf[...] * _dot(yb_buf[...], wbb_ref[...])
        xo_ref[...] = read_x() + _dot(m.astype(BF16), wout_ref[...])


def _mixer_kernel(x_ref, *refs, n_seq, n_t, layer):
    _mixer_body(lambda: x_ref[...], pl.program_id(0), *refs, n_seq=n_seq, n_t=n_t, layer=layer)


def _seq_major_tile_copies(seq_hbm, tile_buf, sem, tile, slot, *, n_seq, n_t, to_hbm):
    copies = []
    for b in range(n_seq):
        if isinstance(tile, int):
            assert tile == 0
            hbm = seq_hbm.at[b, pl.ds(0, n_t - N_META), :]
            vmem = tile_buf.at[slot, pl.ds(N_META, n_t - N_META), b, :]
        else:
            t0 = pl.multiple_of(tile * n_t - N_META, SUBLANES)
            hbm = seq_hbm.at[b, pl.ds(t0, n_t), :]
            vmem = tile_buf.at[slot, :, b, :]
        src, dst = (vmem, hbm) if to_hbm else (hbm, vmem)
        copies.append(pltpu.make_async_copy(src, dst, sem.at[slot]))
    return copies


def _mixer_from_seq_major_kernel(xs_hbm, meta_ref, *refs, n_seq, n_t, layer):
    *refs, xin, sem = refs
    n_in, n_out, n_cast = 3 + len(_MIXER_WEIGHTS), 4, len(_FFN_BIG_WEIGHTS)
    cast_src, cast_dst = refs[n_in:n_in + n_cast], refs[n_in + n_cast + n_out:n_in + 2 * n_cast + n_out]
    refs = refs[:n_in] + refs[n_in + n_cast:n_in + n_cast + n_out] + refs[n_in + 2 * n_cast + n_out:]
    step = pl.program_id(0)
    n_steps = pl.num_programs(0)
    slot = step % 2
    copies = functools.partial(_seq_major_tile_copies, xs_hbm, xin, sem, n_seq=n_seq, n_t=n_t, to_hbm=False)

    @pl.when(step == 0)
    def _():
        for c in copies(0, 0):
            c.start()
        xin[0, 0:N_META, :, :] = jnp.broadcast_to(meta_ref[...][:, None, :], (N_META, n_seq, D_MODEL))

        @pl.when(n_steps > 1)
        def _():
            for c in copies(step + 1, 1):
                c.start()
        for c in copies(0, 0):
            c.wait()

    @pl.when(step > 0)
    def _():
        @pl.when(step + 1 < n_steps)
        def _():
            for c in copies(step + 1, 1 - slot):
                c.start()
        for c in copies(step, slot):
            c.wait()

    _mixer_body(lambda: xin[slot].reshape(n_t * n_seq, D_MODEL), step, *refs, n_seq=n_seq, n_t=n_t, layer=layer)
    _cast_weight_blocks(_FFN_BIG_WEIGHTS, cast_src, cast_dst)


def _ffn_body(x, *, g2_ref, wg_ref, wu_ref, wd_ref, gf_ref, final_norm, layer):
    v = _rmsnorm(x, g2_ref[layer:layer + 1, :]).astype(BF16)
    half = _dot(v, wg_ref[...])
    hid = ((half * jnp.tanh(half) + half) * _dot(v, wu_ref[...])).astype(BF16)
    y = x + _dot(hid, wd_ref[...])
    if final_norm:
        y = _rmsnorm(y, gf_ref[...])
    return y


def _cast_weight_blocks(names, src_refs, dst_refs):
    @pl.when(pl.program_id(0) < N_CONVERT_STEPS)
    def _():
        for name, src, dst in zip(names, src_refs, dst_refs):
            if name == "w_in":
                plain = (N_SPLITS - 2) * D_MODEL
                dst[:, :plain] = src[:, :plain].astype(BF16)
                dst[:, plain:] = (0.5 * src[:, plain:]).astype(BF16)
            elif name == "w_ff_gate":
                dst[...] = (0.5 * src[...]).astype(BF16)
            else:
                dst[...] = src[...].astype(BF16)


def _ffn_kernel(xp_ref, xs_ref, g2_ref, wg_ref, wu_ref, wd_ref, gf_ref, *rest, final_norm, layer, n_prompt_steps):
    n_big = (len(rest) - 2) // 2
    next_f32, (op_ref, os_ref), next_bf16 = rest[:n_big], rest[n_big:n_big + 2], rest[n_big + 2:]
    body = functools.partial(_ffn_body, g2_ref=g2_ref, wg_ref=wg_ref, wu_ref=wu_ref, wd_ref=wd_ref, gf_ref=gf_ref,
                             final_norm=final_norm, layer=layer)
    step = pl.program_id(0)

    @pl.when(step < n_prompt_steps)
    def _():
        op_ref[...] = body(xp_ref[...])

    @pl.when(step == n_prompt_steps)
    def _():
        os_ref[...] = body(xs_ref[...])

    if n_big:
        _cast_weight_blocks(_BIG_WEIGHTS, next_f32, next_bf16)


def _ffn_to_seq_major_kernel(xp_ref, xs_ref, g2_ref, wg_ref, wu_ref, wd_ref, gf_ref, yp_hbm, os_ref, yout, sem,
                             *, n_seq, n_t, layer, n_prompt_steps):
    step = pl.program_id(0)
    n_steps = n_prompt_steps
    slot = step % 2
    copies = functools.partial(_seq_major_tile_copies, yp_hbm, yout, sem, n_seq=n_seq, n_t=n_t, to_hbm=True)
    body = functools.partial(_ffn_body, g2_ref=g2_ref, wg_ref=wg_ref, wu_ref=wu_ref, wd_ref=wd_ref, gf_ref=gf_ref,
                             final_norm=True, layer=layer)

    @pl.when(step == n_steps)
    def _():
        os_ref[...] = body(xs_ref[...])

    @pl.when(step < n_steps)
    def _():
        y = body(xp_ref[...])

        @pl.when(step == 1)
        def _():
            for c in copies(0, 0):
                c.wait()

        @pl.when(step > 1)
        def _():
            for c in copies(step - 1, 1 - slot):
                c.wait()

        yout[slot] = y.reshape(n_t, n_seq, D_MODEL)

        @pl.when(step == 0)
        def _():
            for c in copies(0, 0):
                c.start()
            if n_steps == 1:
                for c in copies(0, 0):
                    c.wait()

        @pl.when(step > 0)
        def _():
            for c in copies(step, slot):
                c.start()

            @pl.when(step == n_steps - 1)
            def _():
                for c in copies(step, slot):
                    c.wait()


def _resident(shape, index):
    return pl.BlockSpec(shape, index, pipeline_mode=pl.Buffered(1))


def _weight_operands(names, small, big, layer):
    specs, arrays = [], []
    for name in names:
        arr = big[name] if name in big else small[name]
        if name in big or arr.ndim == 2:
            specs.append(_resident(arr.shape, lambda i, nd=arr.ndim: (0,) * nd))
        else:
            specs.append(_resident((None,) + arr.shape[1:], lambda i, nd=arr.ndim: (layer,) + (0,) * (nd - 1)))
        arrays.append(arr)
    return specs, arrays


_MIXER_WEIGHTS = ("norm1_g", "w_in", "rnn_conv_w", "rnn_conv_b", "w_gate", "gate_a_b", "gate_x_b",
                  "lru_lambda", "w_branch_a", "sc_conv_w", "w_branch_b", "w_out")
_FFN_WEIGHTS = ("norm2_g", "w_ff_gate", "w_ff_up", "w_ff_down")
_BIG_WEIGHTS = ("w_in", "w_branch_a", "w_branch_b", "w_out", "w_ff_gate", "w_ff_up", "w_ff_down")
_FFN_BIG_WEIGHTS = _BIG_WEIGHTS[4:]
N_CONVERT_STEPS = 16
_COMPILER_PARAMS = pltpu.CompilerParams(dimension_semantics=("arbitrary",), vmem_limit_bytes=VMEM_LIMIT_BYTES)


def _cast_operands(names, stacked_f32, layer, n_steps):
    assert n_steps >= N_CONVERT_STEPS
    block = lambda i: jnp.minimum(i, N_CONVERT_STEPS - 1)
    in_specs, out_specs, out_shapes = [], [], []
    for name in names:
        _, n_in, n_out = stacked_f32[name].shape
        blk = n_in // N_CONVERT_STEPS
        assert blk * N_CONVERT_STEPS == n_in and blk % (2 * SUBLANES) == 0
        in_specs.append(pl.BlockSpec((None, blk, n_out), lambda i: (layer, block(i), 0)))
        out_specs.append(pl.BlockSpec((blk, n_out), lambda i: (block(i), 0)))
        out_shapes.append(jax.ShapeDtypeStruct((n_in, n_out), BF16))
    return in_specs, out_specs, out_shapes, [stacked_f32[name] for name in names]


def _mixer_call(x, h0, rc0, sc0, small, big, layer, *, n_seq, n_t, meta=None, ffn_f32=None):
    rows = n_seq * n_t
    rc_rows = (RNN_CONV_W - 1) * n_seq
    sc_rows = (SC_CONV_W - 1) * n_seq
    row_spec = pl.BlockSpec((rows, D_MODEL), lambda i: (i, 0))
    state_layer = layer if h0.shape[0] > 1 else 0
    state_spec = lambda r: _resident((None, r, D_MODEL), lambda i: (state_layer, 0, 0))
    scratch = [pltpu.VMEM((rows, D_MODEL), BF16),
               pltpu.VMEM((rc_rows + rows, D_MODEL), F32),
               pltpu.VMEM((sc_rows + rows, D_MODEL), F32),
               pltpu.VMEM((rows, D_MODEL), F32),
               pltpu.VMEM((rows, D_MODEL), F32),
               pltpu.VMEM((rows, D_MODEL), BF16),
               pltpu.VMEM((rows, D_MODEL), BF16),
               pltpu.VMEM((rows, D_MODEL), F32),
               pltpu.VMEM((rows, D_MODEL), F32)]
    if meta is None:
        body, x_args, n_rows = _mixer_kernel, (x,), x.shape[0]
        x_specs = [row_spec]
    else:
        assert n_t % SUBLANES == 0 and n_t > N_META and n_seq == SUBLANES
        body, x_args, n_rows = _mixer_from_seq_major_kernel, (x, meta), (x.shape[1] + N_META) * n_seq
        x_specs = [pl.BlockSpec(memory_space=pl.ANY), _resident(meta.shape, lambda i: (0, 0))]
        scratch += [pltpu.VMEM((2, n_t, n_seq, D_MODEL), F32), pltpu.SemaphoreType.DMA((2,))]
    assert n_rows % rows == 0
    w_specs, w_arrays = _weight_operands(_MIXER_WEIGHTS, small, big, layer)
    cast_in, cast_out, cast_shapes, cast_arrays = ([], [], [], []) if meta is None else _cast_operands(
        _FFN_BIG_WEIGHTS, ffn_f32, layer, n_rows // rows)
    outs = pl.pallas_call(
        functools.partial(body, n_seq=n_seq, n_t=n_t, layer=layer),
        grid=(n_rows // rows,),
        in_specs=x_specs + [state_spec(n_seq), state_spec(rc_rows), state_spec(sc_rows)] + w_specs + cast_in,
        out_specs=[row_spec,
                   pl.BlockSpec((n_seq, D_MODEL), lambda i: (0, 0)),
                   pl.BlockSpec((rc_rows, D_MODEL), lambda i: (0, 0)),
                   pl.BlockSpec((sc_rows, D_MODEL), lambda i: (0, 0))] + cast_out,
        out_shape=[jax.ShapeDtypeStruct((n_rows, D_MODEL), F32),
                   jax.ShapeDtypeStruct((n_seq, D_MODEL), F32),
                   jax.ShapeDtypeStruct((rc_rows, D_MODEL), F32),
                   jax.ShapeDtypeStruct((sc_rows, D_MODEL), F32)] + cast_shapes,
        scratch_shapes=scratch,
        compiler_params=_COMPILER_PARAMS,
        name=f"mixer_s{n_seq}" + ("_in" if meta is not None else ""),
    )(*x_args, h0, rc0, sc0, *w_arrays, *cast_arrays)
    if meta is None:
        return outs
    return (*outs[:4], dict(zip(_FFN_BIG_WEIGHTS, outs[4:])))


def _ffn_call(xp, xs, small, big, layer, *, rows, final_norm, seq_major_out=None, next_f32=None):
    n_rows = xp.shape[0]
    assert n_rows % rows == 0
    n_prompt_steps = n_rows // rows
    prompt_spec = pl.BlockSpec((rows, D_MODEL), lambda i: (jnp.minimum(i, n_prompt_steps - 1), 0))
    sample_spec = pl.BlockSpec(xs.shape, lambda i: (0, 0))
    gf = small["final_norm_g"]
    w_specs, w_arrays = _weight_operands(_FFN_WEIGHTS, small, big, layer)
    in_specs = [prompt_spec, _resident(xs.shape, lambda i: (0, 0))] + w_specs + [_resident(gf.shape, lambda i: (0, 0))]
    args = (xp, xs, *w_arrays, gf)
    sample_shape = jax.ShapeDtypeStruct(xs.shape, F32)
    if seq_major_out is not None:
        n_seq, n_t = seq_major_out
        assert final_norm and next_f32 is None and rows == n_seq * n_t and n_t % SUBLANES == 0 and n_t > N_META
        return pl.pallas_call(
            functools.partial(_ffn_to_seq_major_kernel, n_seq=n_seq, n_t=n_t, layer=layer,
                              n_prompt_steps=n_prompt_steps),
            grid=(n_prompt_steps + 1,), in_specs=in_specs,
            out_specs=[pl.BlockSpec(memory_space=pl.ANY), sample_spec],
            out_shape=[jax.ShapeDtypeStruct((n_seq, n_rows // n_seq - N_META, D_MODEL), F32), sample_shape],
            scratch_shapes=[pltpu.VMEM((2, n_t, n_seq, D_MODEL), F32), pltpu.SemaphoreType.DMA((2,))],
            compiler_params=_COMPILER_PARAMS, name=f"ffn_r{rows}_out",
        )(*args)
    out_specs, out_shape = [prompt_spec, sample_spec], [jax.ShapeDtypeStruct((n_rows, D_MODEL), F32), sample_shape]
    if next_f32 is not None:
        cast_in, cast_out, cast_shapes, cast_arrays = _cast_operands(_BIG_WEIGHTS, next_f32, layer + 1,
                                                                     n_prompt_steps)
        in_specs, out_specs, out_shape = in_specs + cast_in, out_specs + cast_out, out_shape + cast_shapes
        args += tuple(cast_arrays)
    outs = pl.pallas_call(
        functools.partial(_ffn_kernel, final_norm=final_norm, layer=layer, n_prompt_steps=n_prompt_steps),
        grid=(n_prompt_steps + 1,), in_specs=in_specs, out_specs=out_specs, out_shape=out_shape,
        compiler_params=_COMPILER_PARAMS,
        name=f"ffn_r{rows}" + ("_cast" if next_f32 is not None else ""),
    )(*args)
    if next_f32 is None:
        return outs
    return outs[0], outs[1], dict(zip(_BIG_WEIGHTS, outs[2:]))


def _block_diag_gates(gate_a_w, gate_x_w):
    hpc = GATE_CHUNK // RNN_HEAD_DIM
    own_block = (jnp.arange(GATE_CHUNK) // RNN_HEAD_DIM)[None, :] == jnp.arange(hpc)[:, None]

    def bd(wt):
        wt = wt.reshape(DEPTH, N_GATE_CHUNKS, hpc, RNN_HEAD_DIM, RNN_HEAD_DIM)
        across = jnp.tile(wt, (1, 1, 1, 1, hpc))
        return jnp.where(own_block[:, None, :], across, 0.0).reshape(DEPTH, N_GATE_CHUNKS, GATE_CHUNK, GATE_CHUNK)

    return jnp.concatenate([bd(gate_a_w), bd(gate_x_w)], axis=-1).astype(BF16)


def _time_major(states):
    return jnp.swapaxes(states, 1, 2).reshape(states.shape[0], -1, D_MODEL)


def _seq_major(states, n_seq):
    return jnp.swapaxes(jnp.stack(states).reshape(len(states), -1, n_seq, D_MODEL), 1, 2)


PROMPT_N_T = 86
PROMPT_FFN_ROWS = 688
PROMPT_IO_N_T = 48


def kernel(x_prompt, x_sample, state_rnn_h, state_rnn_conv, state_sc_conv, meta_tokens, norm1_g, w_in, rnn_conv_w, rnn_conv_b, gate_a_w, gate_a_b, gate_x_w, gate_x_b, lru_lambda, w_branch_a, sc_conv_w, w_branch_b, w_out, norm2_g, w_ff_gate, w_ff_up, w_ff_down, final_norm_g):
    small = {
        "norm1_g": norm1_g, "rnn_conv_w": rnn_conv_w, "rnn_conv_b": rnn_conv_b,
        "w_gate": _block_diag_gates(0.5 * gate_a_w, 0.5 * gate_x_w),
        "gate_a_b": gate_a_b, "gate_x_b": gate_x_b, "lru_lambda": lru_lambda,
        "sc_conv_w": sc_conv_w, "norm2_g": norm2_g, "final_norm_g": final_norm_g.reshape(1, D_MODEL),
    }
    big_f32 = {"w_in": w_in, "w_branch_a": w_branch_a, "w_branch_b": w_branch_b, "w_out": w_out,
               "w_ff_gate": w_ff_gate, "w_ff_up": w_ff_up, "w_ff_down": w_ff_down}
    merge_gate_half = jnp.concatenate([jnp.ones(((N_SPLITS - 2) * D_MODEL,), F32), jnp.full((2 * D_MODEL,), 0.5, F32)])
    pre_scale = {"w_in": merge_gate_half, "w_ff_gate": 0.5}
    big = [{n: (big_f32[n][0] * pre_scale.get(n, 1.0)).astype(BF16) for n in _BIG_WEIGHTS[:4]}]
    dt = x_prompt.dtype

    bp = x_prompt.shape[0]
    bs, t_s, _ = x_sample.shape
    zeros = lambda k: jnp.zeros((1, k * bp, D_MODEL), dt)
    prompt_states = (zeros(1), zeros(RNN_CONV_W - 1), zeros(SC_CONV_W - 1))
    sample_states = (state_rnn_h, _time_major(state_rnn_conv), _time_major(state_sc_conv))
    xp = x_prompt
    xs = jnp.swapaxes(x_sample, 0, 1).reshape(t_s * bs, D_MODEL)
    new_p, new_s = [], []
    for layer in range(DEPTH):
        if layer == 0:
            xp, *st_p, ffn_big = _mixer_call(xp, *prompt_states, small, big[layer], layer, n_seq=bp,
                                             n_t=PROMPT_IO_N_T, meta=meta_tokens.astype(dt), ffn_f32=big_f32)
            big[layer].update(ffn_big)
        else:
            xp, *st_p = _mixer_call(xp, *prompt_states, small, big[layer], layer, n_seq=bp, n_t=PROMPT_N_T)
        xs, *st_s = _mixer_call(xs, *sample_states, small, big[layer], layer, n_seq=bs, n_t=t_s)
        new_p.append(st_p)
        new_s.append(st_s)
        if layer == DEPTH - 1:
            y_prompt, ys = _ffn_call(xp, xs, small, big[layer], layer, rows=bp * PROMPT_IO_N_T, final_norm=True,
                                     seq_major_out=(bp, PROMPT_IO_N_T))
        else:
            xp, xs, next_big = _ffn_call(xp, xs, small, big[layer], layer, rows=PROMPT_FFN_ROWS, final_norm=False,
                                         next_f32=big_f32)
            big.append(next_big)
    y_sample = jnp.swapaxes(ys.reshape(t_s, bs, D_MODEL), 0, 1)

    def collect(states, n_seq):
        hs, rcs, scs = zip(*states)
        return jnp.stack(hs), _seq_major(list(rcs), n_seq), _seq_major(list(scs), n_seq)

    rnn_h_p, rnn_conv_p, sc_conv_p = collect(new_p, bp)
    rnn_h_s, rnn_conv_s, sc_conv_s = collect(new_s, bs)

    return (y_prompt, y_sample, rnn_h_p, rnn_conv_p, sc_conv_p, rnn_h_s, rnn_conv_s, sc_conv_s)
```

```python
import functools
import math

import jax
import jax.numpy as jnp
from jax import lax
from jax.experimental import pallas as pl
from jax.experimental.pallas import tpu as pltpu

D_MODEL = 1024
DEPTH = 4
N_META = 16
N_RNN_HEADS = 16
RNN_HEAD_DIM = D_MODEL // N_RNN_HEADS
RNN_CONV_W = 4
SC_CONV_W = 3
LRU_C = 8.0
D_FF = 2816
EPS = 1e-6
N_SPLITS = 7
GATE_CHUNK = 256
N_GATE_CHUNKS = D_MODEL // GATE_CHUNK
SUBLANES = 8

V7X_VMEM_BYTES = 64 * 1024 * 1024
VMEM_LIMIT_BYTES = V7X_VMEM_BYTES - 2 * 1024 * 1024

F32 = jnp.float32
BF16 = jnp.bfloat16


def _sigmoid_of_half(half_x):
    return 0.5 * jnp.tanh(half_x) + 0.5


def _gelu_tanh(x):
    c = math.sqrt(2.0 / math.pi)
    half_x = 0.5 * x
    return half_x * jnp.tanh(x * (c + (c * 0.044715) * (x * x))) + half_x


def _softplus(x):
    return jnp.maximum(x, 0.0) + jnp.log1p(jnp.exp(-jnp.abs(x)))


def _rmsnorm(x, g):
    ms = jnp.mean(x * x, axis=-1, keepdims=True)
    return x * lax.rsqrt(ms + EPS) * g


def _dot(a, b):
    return jnp.dot(a, b, preferred_element_type=F32)


def _mixer_body(read_x, step, h0_ref, rc0_ref, sc0_ref, g1_ref, win_ref, cw_ref, cb_ref, wgate_ref,
                ba_ref, bx_ref, lam_ref, wba_ref, scw_ref, wbb_ref, wout_ref,
                xo_ref, h_ref, rc_ref, sc_ref,
                u_bf, xrbuf, chbuf, a_buf, h_buf, ya_buf, yb_buf, ga_buf, gb_buf, *, n_seq, n_t, layer):
    rows = n_seq * n_t
    rc_rows = (RNN_CONV_W - 1) * n_seq
    sc_rows = (SC_CONV_W - 1) * n_seq
    row = slice(layer, layer + 1)

    def proj(k):
        return _dot(u_bf[...], win_ref[:, k * D_MODEL:(k + 1) * D_MODEL])

    @pl.when(step == 0)
    def _():
        h_ref[...] = h0_ref[...]
        xrbuf[0:rc_rows, :] = rc0_ref[...]
        chbuf[0:sc_rows, :] = sc0_ref[...]

    u_bf[...] = _rmsnorm(read_x(), g1_ref[row, :]).astype(BF16)
    xrbuf[rc_rows:rc_rows + rows, :] = proj(0)
    xc = cb_ref[row, :] + xrbuf[0:rows, :] * cw_ref[0:1, :]
    for k in range(1, RNN_CONV_W):
        xc = xc + xrbuf[k * n_seq:k * n_seq + rows, :] * cw_ref[k:k + 1, :]
    h_buf[...] = xc
    xc_b = xc.astype(BF16)
    for j in range(N_GATE_CHUNKS):
        cols = slice(j * GATE_CHUNK, (j + 1) * GATE_CHUNK)
        gates = _dot(xc_b[:, cols], wgate_ref[j])
        ga_buf[:, cols] = gates[:, :GATE_CHUNK]
        gb_buf[:, cols] = gates[:, GATE_CHUNK:]
    chbuf[sc_rows:sc_rows + rows, :] = proj(3) * proj(4)

    @pl.when(step >= 0)
    def _():
        half_log_a_max = (-0.5 * LRU_C) * _softplus(-lam_ref[row, :])
        t_r = jnp.tanh(ga_buf[...] + 0.5 * ba_ref[row, :])
        log_a = t_r * half_log_a_max + half_log_a_max
        a = jnp.exp(log_a)
        one_minus_a2 = jnp.maximum(jnp.tanh(log_a) * (-1.0 - a * a), 1e-12)
        mult = one_minus_a2 * lax.rsqrt(one_minus_a2)
        gate_i = _sigmoid_of_half(gb_buf[...] + 0.5 * bx_ref[row, :])
        a_buf[...] = a
        h_buf[...] = mult * gate_i * h_buf[...]

        if n_seq == SUBLANES:
            def scan_step(t, h):
                sl = pl.ds(pl.multiple_of(t * n_seq, n_seq), n_seq)
                h = a_buf[sl, :] * h + h_buf[sl, :]
                h_buf[sl, :] = h
                return h
            h_ref[...] = lax.fori_loop(0, n_t, scan_step, h_ref[...], unroll=True)
        else:
            for t in range(n_t):
                sl = slice(t * n_seq, (t + 1) * n_seq)
                prev = h_ref[...] if t == 0 else h_buf[(t - 1) * n_seq:t * n_seq, :]
                h_buf[sl, :] = a_buf[sl, :] * prev + h_buf[sl, :]
            h_ref[...] = h_buf[(n_t - 1) * n_seq:n_t * n_seq, :]

        ya_buf[...] = (h_buf[...] * _gelu_tanh(proj(1))).astype(BF16)

        vc = chbuf[0:rows, :] * scw_ref[0:1, :]
        for k in range(1, SC_CONV_W):
            vc = vc + chbuf[k * n_seq:k * n_seq + rows, :] * scw_ref[k:k + 1, :]
        yb_buf[...] = (proj(2) * vc).astype(BF16)

        ga_buf[...] = _sigmoid_of_half(proj(5))
        gb_buf[...] = _sigmoid_of_half(proj(6))

        new_rc = xrbuf[rows:rows + rc_rows, :]
        new_sc = chbuf[rows:rows + sc_rows, :]
        xrbuf[0:rc_rows, :] = new_rc
        chbuf[0:sc_rows, :] = new_sc

        @pl.when(step == pl.num_programs(0) - 1)
        def _():
            rc_ref[...] = new_rc
            sc_ref[...] = new_sc

    @pl.when(step >= 0)
    def _():
        m = ga_buf[...] * _dot(ya_buf[...], wba_ref[...])
        m = m + gb_buf[...] * _dot(yb_buf[...], wbb_ref[...])
        xo_ref[...] = read_x() + _dot(m.astype(BF16), wout_ref[...])


def _mixer_kernel(x_ref, *refs, n_seq, n_t, layer):
    _mixer_body(lambda: x_ref[...], pl.program_id(0), *refs, n_seq=n_seq, n_t=n_t, layer=layer)


def _seq_major_tile_copies(seq_hbm, tile_buf, sem, tile, slot, *, n_seq, n_t, to_hbm):
    copies = []
    for b in range(n_seq):
        if isinstance(tile, int):
            assert tile == 0
            hbm = seq_hbm.at[b, pl.ds(0, n_t - N_META), :]
            vmem = tile_buf.at[slot, pl.ds(N_META, n_t - N_META), b, :]
        else:
            t0 = pl.multiple_of(tile * n_t - N_META, SUBLANES)
            hbm = seq_hbm.at[b, pl.ds(t0, n_t), :]
            vmem = tile_buf.at[slot, :, b, :]
        src, dst = (vmem, hbm) if to_hbm else (hbm, vmem)
        copies.append(pltpu.make_async_copy(src, dst, sem.at[slot]))
    return copies


def _load_and_cast_mixer_weights(f32_hbm, bf16_vmem, stage, sem, layer):
    chunks = []
    for k in range(N_SPLITS):
        cols = pl.ds(k * D_MODEL, D_MODEL)
        chunks.append((f32_hbm["w_in"].at[layer, :, cols], bf16_vmem["w_in"].at[:, cols], k >= N_SPLITS - 2))
    for name in _MIXER_BIG_WEIGHTS[1:]:
        chunks.append((f32_hbm[name].at[layer], bf16_vmem[name], False))
    copy = lambda c: pltpu.make_async_copy(chunks[c][0], stage.at[c % 2], sem.at[c % 2])
    copy(0).start()
    copy(1).start()
    for c, (_, dst, halve) in enumerate(chunks):
        copy(c).wait()
        value = stage[c % 2]
        dst[...] = (0.5 * value if halve else value).astype(BF16)
        if c + 2 < len(chunks):
            copy(c + 2).start()


def _mixer_from_seq_major_kernel(xs_hbm, meta_ref, *refs, n_seq, n_t, layer):
    refs = list(refs)
    n_w, n_out, n_cast, n_big = len(_MIXER_WEIGHTS), 4, len(_FFN_BIG_WEIGHTS), len(_MIXER_BIG_WEIGHTS)
    states, weights, cast_src = refs[:3], refs[3:3 + n_w], refs[3 + n_w:3 + n_w + n_cast]
    o = 3 + n_w + n_cast
    outs, cast_dst, big_out = refs[o:o + n_out], refs[o + n_out:o + n_out + n_cast], refs[o + n_out + n_cast:o + n_out + n_cast + n_big]
    sc = o + n_out + n_cast + n_big
    n_body_scratch = len(refs) - sc - 2 - n_big - 3
    body_scratch, (xin, sem) = refs[sc:sc + n_body_scratch], refs[sc + n_body_scratch:sc + n_body_scratch + 2]
    big_vmem = dict(zip(_MIXER_BIG_WEIGHTS, refs[sc + n_body_scratch + 2:sc + n_body_scratch + 2 + n_big]))
    stage, load_sem, store_sem = refs[sc + n_body_scratch + 2 + n_big:]
    big_f32 = {name: ref for name, ref in zip(_MIXER_WEIGHTS, weights) if name in big_vmem}
    weights = [big_vmem.get(name, ref) for name, ref in zip(_MIXER_WEIGHTS, weights)]

    step = pl.program_id(0)
    n_steps = pl.num_programs(0)
    slot = step % 2
    copies = functools.partial(_seq_major_tile_copies, xs_hbm, xin, sem, n_seq=n_seq, n_t=n_t, to_hbm=False)
    write_back = [pltpu.make_async_copy(big_vmem[name], dst, store_sem.at[0])
                  for name, dst in zip(_MIXER_BIG_WEIGHTS, big_out)]

    @pl.when(step == 0)
    def _():
        for c in copies(0, 0):
            c.start()
        xin[0, 0:N_META, :, :] = jnp.broadcast_to(meta_ref[...][:, None, :], (N_META, n_seq, D_MODEL))

        @pl.when(n_steps > 1)
        def _():
            for c in copies(step + 1, 1):
                c.start()
        _load_and_cast_mixer_weights(big_f32, big_vmem, stage, load_sem, layer)
        for c in write_back:
            c.start()
        for c in copies(0, 0):
            c.wait()

    @pl.when(step > 0)
    def _():
        @pl.when(step + 1 < n_steps)
        def _():
            for c in copies(step + 1, 1 - slot):
                c.start()
        for c in copies(step, slot):
            c.wait()

    _mixer_body(lambda: xin[slot].reshape(n_t * n_seq, D_MODEL), step, *states, *weights, *outs, *body_scratch,
                n_seq=n_seq, n_t=n_t, layer=layer)
    _cast_weight_blocks(_FFN_BIG_WEIGHTS, cast_src, cast_dst)

    @pl.when(step == n_steps - 1)
    def _():
        for c in write_back:
            c.wait()


def _ffn_body(x, *, g2_ref, wg_ref, wu_ref, wd_ref, gf_ref, final_norm, layer):
    v = _rmsnorm(x, g2_ref[layer:layer + 1, :]).astype(BF16)
    half = _dot(v, wg_ref[...])
    hid = ((half * jnp.tanh(half) + half) * _dot(v, wu_ref[...])).astype(BF16)
    y = x + _dot(hid, wd_ref[...])
    if final_norm:
        y = _rmsnorm(y, gf_ref[...])
    return y


def _cast_weight_blocks(names, src_refs, dst_refs):
    @pl.when(pl.program_id(0) < N_CONVERT_STEPS)
    def _():
        for name, src, dst in zip(names, src_refs, dst_refs):
            if name == "w_in":
                plain = (N_SPLITS - 2) * D_MODEL
                dst[:, :plain] = src[:, :plain].astype(BF16)
                dst[:, plain:] = (0.5 * src[:, plain:]).astype(BF16)
            elif name == "w_ff_gate":
                dst[...] = (0.5 * src[...]).astype(BF16)
            else:
                dst[...] = src[...].astype(BF16)


def _ffn_kernel(xp_ref, xs_ref, g2_ref, wg_ref, wu_ref, wd_ref, gf_ref, *rest, final_norm, layer, n_prompt_steps):
    n_big = (len(rest) - 2) // 2
    next_f32, (op_ref, os_ref), next_bf16 = rest[:n_big], rest[n_big:n_big + 2], rest[n_big + 2:]
    body = functools.partial(_ffn_body, g2_ref=g2_ref, wg_ref=wg_ref, wu_ref=wu_ref, wd_ref=wd_ref, gf_ref=gf_ref,
                             final_norm=final_norm, layer=layer)
    step = pl.program_id(0)

    @pl.when(step < n_prompt_steps)
    def _():
        op_ref[...] = body(xp_ref[...])

    @pl.when(step == n_prompt_steps)
    def _():
        os_ref[...] = body(xs_ref[...])

    if n_big:
        _cast_weight_blocks(_BIG_WEIGHTS, next_f32, next_bf16)


def _ffn_to_seq_major_kernel(xp_ref, xs_ref, g2_ref, wg_ref, wu_ref, wd_ref, gf_ref, yp_hbm, os_ref, yout, sem,
                             *, n_seq, n_t, layer, n_prompt_steps):
    step = pl.program_id(0)
    n_steps = n_prompt_steps
    slot = step % 2
    copies = functools.partial(_seq_major_tile_copies, yp_hbm, yout, sem, n_seq=n_seq, n_t=n_t, to_hbm=True)
    body = functools.partial(_ffn_body, g2_ref=g2_ref, wg_ref=wg_ref, wu_ref=wu_ref, wd_ref=wd_ref, gf_ref=gf_ref,
                             final_norm=True, layer=layer)

    @pl.when(step == n_steps)
    def _():
        os_ref[...] = body(xs_ref[...])

    @pl.when(step < n_steps)
    def _():
        y = body(xp_ref[...])

        @pl.when(step == 1)
        def _():
            for c in copies(0, 0):
                c.wait()

        @pl.when(step > 1)
        def _():
            for c in copies(step - 1, 1 - slot):
                c.wait()

        yout[slot] = y.reshape(n_t, n_seq, D_MODEL)

        @pl.when(step == 0)
        def _():
            for c in copies(0, 0):
                c.start()
            if n_steps == 1:
                for c in copies(0, 0):
                    c.wait()

        @pl.when(step > 0)
        def _():
            for c in copies(step, slot):
                c.start()

            @pl.when(step == n_steps - 1)
            def _():
                for c in copies(step, slot):
                    c.wait()


def _resident(shape, index):
    return pl.BlockSpec(shape, index, pipeline_mode=pl.Buffered(1))


def _weight_operands(names, small, big, layer):
    specs, arrays = [], []
    for name in names:
        if name not in big and name not in small:
            specs.append(None)
            arrays.append(None)
            continue
        arr = big[name] if name in big else small[name]
        if name in big or arr.ndim == 2:
            specs.append(_resident(arr.shape, lambda i, nd=arr.ndim: (0,) * nd))
        else:
            specs.append(_resident((None,) + arr.shape[1:], lambda i, nd=arr.ndim: (layer,) + (0,) * (nd - 1)))
        arrays.append(arr)
    return specs, arrays


_MIXER_WEIGHTS = ("norm1_g", "w_in", "rnn_conv_w", "rnn_conv_b", "w_gate", "gate_a_b", "gate_x_b",
                  "lru_lambda", "w_branch_a", "sc_conv_w", "w_branch_b", "w_out")
_FFN_WEIGHTS = ("norm2_g", "w_ff_gate", "w_ff_up", "w_ff_down")
_BIG_WEIGHTS = ("w_in", "w_branch_a", "w_branch_b", "w_out", "w_ff_gate", "w_ff_up", "w_ff_down")
_MIXER_BIG_WEIGHTS, _FFN_BIG_WEIGHTS = _BIG_WEIGHTS[:4], _BIG_WEIGHTS[4:]
N_CONVERT_STEPS = 16
_COMPILER_PARAMS = pltpu.CompilerParams(dimension_semantics=("arbitrary",), vmem_limit_bytes=VMEM_LIMIT_BYTES)


def _cast_operands(names, stacked_f32, layer, n_steps):
    assert n_steps >= N_CONVERT_STEPS
    block = lambda i: jnp.minimum(i, N_CONVERT_STEPS - 1)
    in_specs, out_specs, out_shapes = [], [], []
    for name in names:
        _, n_in, n_out = stacked_f32[name].shape
        blk = n_in // N_CONVERT_STEPS
        assert blk * N_CONVERT_STEPS == n_in and blk % (2 * SUBLANES) == 0
        in_specs.append(pl.BlockSpec((None, blk, n_out), lambda i: (layer, block(i), 0)))
        out_specs.append(pl.BlockSpec((blk, n_out), lambda i: (block(i), 0)))
        out_shapes.append(jax.ShapeDtypeStruct((n_in, n_out), BF16))
    return in_specs, out_specs, out_shapes, [stacked_f32[name] for name in names]


def _mixer_call(x, h0, rc0, sc0, small, big, layer, *, n_seq, n_t, meta=None, big_f32=None):
    rows = n_seq * n_t
    rc_rows = (RNN_CONV_W - 1) * n_seq
    sc_rows = (SC_CONV_W - 1) * n_seq
    row_spec = pl.BlockSpec((rows, D_MODEL), lambda i: (i, 0))
    state_layer = layer if h0.shape[0] > 1 else 0
    state_spec = lambda r: _resident((None, r, D_MODEL), lambda i: (state_layer, 0, 0))
    scratch = [pltpu.VMEM((rows, D_MODEL), BF16),
               pltpu.VMEM((rc_rows + rows, D_MODEL), F32),
               pltpu.VMEM((sc_rows + rows, D_MODEL), F32),
               pltpu.VMEM((rows, D_MODEL), F32),
               pltpu.VMEM((rows, D_MODEL), F32),
               pltpu.VMEM((rows, D_MODEL), BF16),
               pltpu.VMEM((rows, D_MODEL), BF16),
               pltpu.VMEM((rows, D_MODEL), F32),
               pltpu.VMEM((rows, D_MODEL), F32)]
    if meta is None:
        body, x_args, n_rows = _mixer_kernel, (x,), x.shape[0]
        x_specs = [row_spec]
    else:
        assert n_t % SUBLANES == 0 and n_t > N_META and n_seq == SUBLANES
        body, x_args, n_rows = _mixer_from_seq_major_kernel, (x, meta), (x.shape[1] + N_META) * n_seq
        x_specs = [pl.BlockSpec(memory_space=pl.ANY), _resident(meta.shape, lambda i: (0, 0))]
        scratch += [pltpu.VMEM((2, n_t, n_seq, D_MODEL), F32), pltpu.SemaphoreType.DMA((2,))]
    assert n_rows % rows == 0
    w_specs, w_arrays = _weight_operands(_MIXER_WEIGHTS, small, big, layer)
    cast_in, cast_out, cast_shapes, cast_arrays = [], [], [], []
    if meta is not None:
        cast_in, cast_out, cast_shapes, cast_arrays = _cast_operands(_FFN_BIG_WEIGHTS, big_f32, layer, n_rows // rows)
        for name in _MIXER_BIG_WEIGHTS:
            i, shape = _MIXER_WEIGHTS.index(name), big_f32[name].shape[1:]
            w_specs[i], w_arrays[i] = pl.BlockSpec(memory_space=pl.ANY), big_f32[name]
            cast_out.append(pl.BlockSpec(memory_space=pl.ANY))
            cast_shapes.append(jax.ShapeDtypeStruct(shape, BF16))
            scratch.append(pltpu.VMEM(shape, BF16))
        scratch += [pltpu.VMEM((2, D_MODEL, D_MODEL), F32), pltpu.SemaphoreType.DMA((2,)), pltpu.SemaphoreType.DMA((1,))]
    outs = pl.pallas_call(
        functools.partial(body, n_seq=n_seq, n_t=n_t, layer=layer),
        grid=(n_rows // rows,),
        in_specs=x_specs + [state_spec(n_seq), state_spec(rc_rows), state_spec(sc_rows)] + w_specs + cast_in,
        out_specs=[row_spec,
                   pl.BlockSpec((n_seq, D_MODEL), lambda i: (0, 0)),
                   pl.BlockSpec((rc_rows, D_MODEL), lambda i: (0, 0)),
                   pl.BlockSpec((sc_rows, D_MODEL), lambda i: (0, 0))] + cast_out,
        out_shape=[jax.ShapeDtypeStruct((n_rows, D_MODEL), F32),
                   jax.ShapeDtypeStruct((n_seq, D_MODEL), F32),
                   jax.ShapeDtypeStruct((rc_rows, D_MODEL), F32),
                   jax.ShapeDtypeStruct((sc_rows, D_MODEL), F32)] + cast_shapes,
        scratch_shapes=scratch,
        compiler_params=_COMPILER_PARAMS,
        name=f"mixer_s{n_seq}" + ("_in" if meta is not None else ""),
    )(*x_args, h0, rc0, sc0, *w_arrays, *cast_arrays)
    if meta is None:
        return outs
    n_ffn = len(_FFN_BIG_WEIGHTS)
    return (*outs[:4], dict(zip(_FFN_BIG_WEIGHTS, outs[4:4 + n_ffn])), dict(zip(_MIXER_BIG_WEIGHTS, outs[4 + n_ffn:])))


def _ffn_call(xp, xs, small, big, layer, *, rows, final_norm, seq_major_out=None, next_f32=None):
    n_rows = xp.shape[0]
    assert n_rows % rows == 0
    n_prompt_steps = n_rows // rows
    prompt_spec = pl.BlockSpec((rows, D_MODEL), lambda i: (jnp.minimum(i, n_prompt_steps - 1), 0))
    sample_spec = pl.BlockSpec(xs.shape, lambda i: (0, 0))
    gf = small["final_norm_g"]
    w_specs, w_arrays = _weight_operands(_FFN_WEIGHTS, small, big, layer)
    in_specs = [prompt_spec, _resident(xs.shape, lambda i: (0, 0))] + w_specs + [_resident(gf.shape, lambda i: (0, 0))]
    args = (xp, xs, *w_arrays, gf)
    sample_shape = jax.ShapeDtypeStruct(xs.shape, F32)
    if seq_major_out is not None:
        n_seq, n_t = seq_major_out
        assert final_norm and next_f32 is None and rows == n_seq * n_t and n_t % SUBLANES == 0 and n_t > N_META
        return pl.pallas_call(
            functools.partial(_ffn_to_seq_major_kernel, n_seq=n_seq, n_t=n_t, layer=layer,
                              n_prompt_steps=n_prompt_steps),
            grid=(n_prompt_steps + 1,), in_specs=in_specs,
            out_specs=[pl.BlockSpec(memory_space=pl.ANY), sample_spec],
            out_shape=[jax.ShapeDtypeStruct((n_seq, n_rows // n_seq - N_META, D_MODEL), F32), sample_shape],
            scratch_shapes=[pltpu.VMEM((2, n_t, n_seq, D_MODEL), F32), pltpu.SemaphoreType.DMA((2,))],
            compiler_params=_COMPILER_PARAMS, name=f"ffn_r{rows}_out",
        )(*args)
    out_specs, out_shape = [prompt_spec, sample_spec], [jax.ShapeDtypeStruct((n_rows, D_MODEL), F32), sample_shape]
    if next_f32 is not None:
        cast_in, cast_out, cast_shapes, cast_arrays = _cast_operands(_BIG_WEIGHTS, next_f32, layer + 1,
                                                                     n_prompt_steps)
        in_specs, out_specs, out_shape = in_specs + cast_in, out_specs + cast_out, out_shape + cast_shapes
        args += tuple(cast_arrays)
    outs = pl.pallas_call(
        functools.partial(_ffn_kernel, final_norm=final_norm, layer=layer, n_prompt_steps=n_prompt_steps),
        grid=(n_prompt_steps + 1,), in_specs=in_specs, out_specs=out_specs, out_shape=out_shape,
        compiler_params=_COMPILER_PARAMS,
        name=f"ffn_r{rows}" + ("_cast" if next_f32 is not None else ""),
    )(*args)
    if next_f32 is None:
        return outs
    return outs[0], outs[1], dict(zip(_BIG_WEIGHTS, outs[2:]))


def _block_diag_gates(gate_a_w, gate_x_w):
    hpc = GATE_CHUNK // RNN_HEAD_DIM
    own_block = (jnp.arange(GATE_CHUNK) // RNN_HEAD_DIM)[None, :] == jnp.arange(hpc)[:, None]

    def bd(wt):
        wt = wt.reshape(DEPTH, N_GATE_CHUNKS, hpc, RNN_HEAD_DIM, RNN_HEAD_DIM)
        across = jnp.tile(wt, (1, 1, 1, 1, hpc))
        return jnp.where(own_block[:, None, :], across, 0.0).reshape(DEPTH, N_GATE_CHUNKS, GATE_CHUNK, GATE_CHUNK)

    return jnp.concatenate([bd(gate_a_w), bd(gate_x_w)], axis=-1).astype(BF16)


def _time_major(states):
    return jnp.swapaxes(states, 1, 2).reshape(states.shape[0], -1, D_MODEL)


def _seq_major(states, n_seq):
    return jnp.swapaxes(jnp.stack(states).reshape(len(states), -1, n_seq, D_MODEL), 1, 2)


PROMPT_N_T = 86
PROMPT_FFN_ROWS = 688
PROMPT_IO_N_T = 48


def kernel(x_prompt, x_sample, state_rnn_h, state_rnn_conv, state_sc_conv, meta_tokens, norm1_g, w_in, rnn_conv_w, rnn_conv_b, gate_a_w, gate_a_b, gate_x_w, gate_x_b, lru_lambda, w_branch_a, sc_conv_w, w_branch_b, w_out, norm2_g, w_ff_gate, w_ff_up, w_ff_down, final_norm_g):
    small = {
        "norm1_g": norm1_g, "rnn_conv_w": rnn_conv_w, "rnn_conv_b": rnn_conv_b,
        "w_gate": _block_diag_gates(0.5 * gate_a_w, 0.5 * gate_x_w),
        "gate_a_b": gate_a_b, "gate_x_b": gate_x_b, "lru_lambda": lru_lambda,
        "sc_conv_w": sc_conv_w, "norm2_g": norm2_g, "final_norm_g": final_norm_g.reshape(1, D_MODEL),
    }
    big_f32 = {"w_in": w_in, "w_branch_a": w_branch_a, "w_branch_b": w_branch_b, "w_out": w_out,
               "w_ff_gate": w_ff_gate, "w_ff_up": w_ff_up, "w_ff_down": w_ff_down}
    big = [{}]
    dt = x_prompt.dtype

    bp = x_prompt.shape[0]
    bs, t_s, _ = x_sample.shape
    zeros = lambda k: jnp.zeros((1, k * bp, D_MODEL), dt)
    prompt_states = (zeros(1), zeros(RNN_CONV_W - 1), zeros(SC_CONV_W - 1))
    sample_states = (state_rnn_h, _time_major(state_rnn_conv), _time_major(state_sc_conv))
    xp = x_prompt
    xs = jnp.swapaxes(x_sample, 0, 1).reshape(t_s * bs, D_MODEL)
    new_p, new_s = [], []
    for layer in range(DEPTH):
        if layer == 0:
            xp, *st_p, ffn_big, mixer_big = _mixer_call(xp, *prompt_states, small, big[layer], layer, n_seq=bp,
                                                        n_t=PROMPT_IO_N_T, meta=meta_tokens.astype(dt), big_f32=big_f32)
            big[layer] = {**mixer_big, **ffn_big}
        else:
            xp, *st_p = _mixer_call(xp, *prompt_states, small, big[layer], layer, n_seq=bp, n_t=PROMPT_N_T)
        xs, *st_s = _mixer_call(xs, *sample_states, small, big[layer], layer, n_seq=bs, n_t=t_s)
        new_p.append(st_p)
        new_s.append(st_s)
        if layer == DEPTH - 1:
            y_prompt, ys = _ffn_call(xp, xs, small, big[layer], layer, rows=bp * PROMPT_IO_N_T, final_norm=True,
                                     seq_major_out=(bp, PROMPT_IO_N_T))
        else:
            xp, xs, next_big = _ffn_call(xp, xs, small, big[layer], layer, rows=PROMPT_FFN_ROWS, final_norm=False,
                                         next_f32=big_f32)
            big.append(next_big)
    y_sample = jnp.swapaxes(ys.reshape(t_s, bs, D_MODEL), 0, 1)

    def collect(states, n_seq):
        hs, rcs, scs = zip(*states)
        return jnp.stack(hs), _seq_major(list(rcs), n_seq), _seq_major(list(scs), n_seq)

    rnn_h_p, rnn_conv_p, sc_conv_p = collect(new_p, bp)
    rnn_h_s, rnn_conv_s, sc_conv_s = collect(new_s, bs)

    return (y_prompt, y_sample, rnn_h_p, rnn_conv_p, sc_conv_p, rnn_h_s, rnn_conv_s, sc_conv_s)
```

```python
import functools
import math

import jax
import jax.numpy as jnp
from jax import lax
from jax.experimental import pallas as pl
from jax.experimental.pallas import tpu as pltpu

D_MODEL = 1024
DEPTH = 4
N_META = 16
N_RNN_HEADS = 16
RNN_HEAD_DIM = D_MODEL // N_RNN_HEADS
RNN_CONV_W = 4
SC_CONV_W = 3
LRU_C = 8.0
D_FF = 2816
EPS = 1e-6
N_SPLITS = 7
GATE_CHUNK = 256
N_GATE_CHUNKS = D_MODEL // GATE_CHUNK
SUBLANES = 8

V7X_VMEM_BYTES = 64 * 1024 * 1024
VMEM_LIMIT_BYTES = V7X_VMEM_BYTES - 2 * 1024 * 1024

F32 = jnp.float32
BF16 = jnp.bfloat16


def _sigmoid_of_half(half_x):
    return 0.5 * jnp.tanh(half_x) + 0.5


def _gelu_tanh(x):
    c = math.sqrt(2.0 / math.pi)
    half_x = 0.5 * x
    return half_x * jnp.tanh(x * (c + (c * 0.044715) * (x * x))) + half_x


def _softplus(x):
    return jnp.maximum(x, 0.0) + jnp.log1p(jnp.exp(-jnp.abs(x)))


def _rmsnorm(x, g):
    ms = jnp.mean(x * x, axis=-1, keepdims=True)
    return x * lax.rsqrt(ms + EPS) * g


def _dot(a, b):
    return jnp.dot(a, b, preferred_element_type=F32)


def _mixer_body(read_x, step, region_gate, wait_weights, h0_ref, rc0_ref, sc0_ref, g1_ref, win_ref, cw_ref, cb_ref, wgate_ref,
                ba_ref, bx_ref, lam_ref, wba_ref, scw_ref, wbb_ref, wout_ref,
                xo_ref, h_ref, rc_ref, sc_ref,
                u_bf, xrbuf, chbuf, a_buf, h_buf, ya_buf, yb_buf, ga_buf, gb_buf, *, n_seq, n_t, layer):
    rows = n_seq * n_t
    rc_rows = (RNN_CONV_W - 1) * n_seq
    sc_rows = (SC_CONV_W - 1) * n_seq
    row = slice(layer, layer + 1)

    def proj(k):
        return _dot(u_bf[...], win_ref[:, k * D_MODEL:(k + 1) * D_MODEL])

    wait_weights(1)

    @pl.when(step == 0)
    def _():
        h_ref[...] = h0_ref[...]
        xrbuf[0:rc_rows, :] = rc0_ref[...]
        chbuf[0:sc_rows, :] = sc0_ref[...]

    u_bf[...] = _rmsnorm(read_x(), g1_ref[row, :]).astype(BF16)
    xrbuf[rc_rows:rc_rows + rows, :] = proj(0)
    xc = cb_ref[row, :] + xrbuf[0:rows, :] * cw_ref[0:1, :]
    for k in range(1, RNN_CONV_W):
        xc = xc + xrbuf[k * n_seq:k * n_seq + rows, :] * cw_ref[k:k + 1, :]
    h_buf[...] = xc
    xc_b = xc.astype(BF16)
    for j in range(N_GATE_CHUNKS):
        cols = slice(j * GATE_CHUNK, (j + 1) * GATE_CHUNK)
        gates = _dot(xc_b[:, cols], wgate_ref[j])
        ga_buf[:, cols] = gates[:, :GATE_CHUNK]
        gb_buf[:, cols] = gates[:, GATE_CHUNK:]
    chbuf[sc_rows:sc_rows + rows, :] = proj(3) * proj(4)

    @pl.when(region_gate)
    def _():
        wait_weights(2)
        half_log_a_max = (-0.5 * LRU_C) * _softplus(-lam_ref[row, :])
        t_r = jnp.tanh(ga_buf[...] + 0.5 * ba_ref[row, :])
        log_a = t_r * half_log_a_max + half_log_a_max
        a = jnp.exp(log_a)
        one_minus_a2 = jnp.maximum(jnp.tanh(log_a) * (-1.0 - a * a), 1e-12)
        mult = one_minus_a2 * lax.rsqrt(one_minus_a2)
        gate_i = _sigmoid_of_half(gb_buf[...] + 0.5 * bx_ref[row, :])
        a_buf[...] = a
        h_buf[...] = mult * gate_i * h_buf[...]

        if n_seq == SUBLANES:
            def scan_step(t, h):
                sl = pl.ds(pl.multiple_of(t * n_seq, n_seq), n_seq)
                h = a_buf[sl, :] * h + h_buf[sl, :]
                h_buf[sl, :] = h
                return h
            h_ref[...] = lax.fori_loop(0, n_t, scan_step, h_ref[...], unroll=True)
        else:
            for t in range(n_t):
                sl = slice(t * n_seq, (t + 1) * n_seq)
                prev = h_ref[...] if t == 0 else h_buf[(t - 1) * n_seq:t * n_seq, :]
                h_buf[sl, :] = a_buf[sl, :] * prev + h_buf[sl, :]
            h_ref[...] = h_buf[(n_t - 1) * n_seq:n_t * n_seq, :]

        ya_buf[...] = (h_buf[...] * _gelu_tanh(proj(1))).astype(BF16)

        vc = chbuf[0:rows, :] * scw_ref[0:1, :]
        for k in range(1, SC_CONV_W):
            vc = vc + chbuf[k * n_seq:k * n_seq + rows, :] * scw_ref[k:k + 1, :]
        yb_buf[...] = (proj(2) * vc).astype(BF16)

        ga_buf[...] = _sigmoid_of_half(proj(5))
        gb_buf[...] = _sigmoid_of_half(proj(6))

        new_rc = xrbuf[rows:rows + rc_rows, :]
        new_sc = chbuf[rows:rows + sc_rows, :]
        xrbuf[0:rc_rows, :] = new_rc
        chbuf[0:sc_rows, :] = new_sc

        @pl.when(step == pl.num_programs(0) - 1)
        def _():
            rc_ref[...] = new_rc
            sc_ref[...] = new_sc

    @pl.when(region_gate)
    def _():
        wait_weights(3)
        m = ga_buf[...] * _dot(ya_buf[...], wba_ref[...])
        m = m + gb_buf[...] * _dot(yb_buf[...], wbb_ref[...])
        xo_ref[...] = read_x() + _dot(m.astype(BF16), wout_ref[...])


_REGION_W_IN_GROUPS = {1: (0, 3, 4), 2: (1, 2, 5, 6)}


def _mixer_kernel(x_ref, zero_ref, *refs, n_seq, n_t, layer):
    refs = list(refs)
    n_w, n_big = len(_MIXER_WEIGHTS), len(_MIXER_BIG_WEIGHTS)
    *refs, sem = refs
    big_vmem = dict(zip(_MIXER_BIG_WEIGHTS, refs[-n_big:]))
    refs = refs[:-n_big]
    states, weights, rest = refs[:3], refs[3:3 + n_w], refs[3 + n_w:]
    big_hbm = {name: ref for name, ref in zip(_MIXER_WEIGHTS, weights) if name in big_vmem}
    weights = [big_vmem.get(name, ref) for name, ref in zip(_MIXER_WEIGHTS, weights)]
    step = pl.program_id(0)

    def region_copies(region):
        if region in _REGION_W_IN_GROUPS:
            pairs = [(big_hbm["w_in"].at[:, pl.ds(k * D_MODEL, D_MODEL)], big_vmem["w_in"].at[:, pl.ds(k * D_MODEL, D_MODEL)])
                     for k in _REGION_W_IN_GROUPS[region]]
        else:
            pairs = [(big_hbm[name], big_vmem[name]) for name in _MIXER_BIG_WEIGHTS[1:]]
        return [pltpu.make_async_copy(src, dst, sem.at[region - 1]) for src, dst in pairs]

    @pl.when(step == 0)
    def _():
        for region in (1, 2, 3):
            for c in region_copies(region):
                c.start()

    def wait_weights(region):
        @pl.when(step == 0)
        def _():
            for c in region_copies(region):
                c.wait()

    _mixer_body(lambda: x_ref[...], step, step + zero_ref[0] >= 0, wait_weights, *states, *weights, *rest,
                n_seq=n_seq, n_t=n_t, layer=layer)


def _seq_major_tile_copies(seq_hbm, tile_buf, sem, tile, slot, *, n_seq, n_t, to_hbm):
    copies = []
    for b in range(n_seq):
        if isinstance(tile, int):
            assert tile == 0
            hbm = seq_hbm.at[b, pl.ds(0, n_t - N_META), :]
            vmem = tile_buf.at[slot, pl.ds(N_META, n_t - N_META), b, :]
        else:
            t0 = pl.multiple_of(tile * n_t - N_META, SUBLANES)
            hbm = seq_hbm.at[b, pl.ds(t0, n_t), :]
            vmem = tile_buf.at[slot, :, b, :]
        src, dst = (vmem, hbm) if to_hbm else (hbm, vmem)
        copies.append(pltpu.make_async_copy(src, dst, sem.at[slot]))
    return copies


def _load_and_cast_mixer_weights(f32_hbm, bf16_vmem, stage, sem, layer):
    chunks = []
    for k in range(N_SPLITS):
        cols = pl.ds(k * D_MODEL, D_MODEL)
        chunks.append((f32_hbm["w_in"].at[layer, :, cols], bf16_vmem["w_in"].at[:, cols], k >= N_SPLITS - 2))
    for name in _MIXER_BIG_WEIGHTS[1:]:
        chunks.append((f32_hbm[name].at[layer], bf16_vmem[name], False))
    copy = lambda c: pltpu.make_async_copy(chunks[c][0], stage.at[c % 2], sem.at[c % 2])
    copy(0).start()
    copy(1).start()
    for c, (_, dst, halve) in enumerate(chunks):
        copy(c).wait()
        value = stage[c % 2]
        dst[...] = (0.5 * value if halve else value).astype(BF16)
        if c + 2 < len(chunks):
            copy(c + 2).start()


def _mixer_from_seq_major_kernel(xs_hbm, meta_ref, zero_ref, *refs, n_seq, n_t, layer):
    refs = list(refs)
    n_w, n_out, n_cast, n_big = len(_MIXER_WEIGHTS), 4, len(_FFN_BIG_WEIGHTS), len(_MIXER_BIG_WEIGHTS)
    states, weights, cast_src = refs[:3], refs[3:3 + n_w], refs[3 + n_w:3 + n_w + n_cast]
    o = 3 + n_w + n_cast
    outs, cast_dst, big_out = refs[o:o + n_out], refs[o + n_out:o + n_out + n_cast], refs[o + n_out + n_cast:o + n_out + n_cast + n_big]
    sc = o + n_out + n_cast + n_big
    n_body_scratch = len(refs) - sc - 2 - n_big - 3
    body_scratch, (xin, sem) = refs[sc:sc + n_body_scratch], refs[sc + n_body_scratch:sc + n_body_scratch + 2]
    big_vmem = dict(zip(_MIXER_BIG_WEIGHTS, refs[sc + n_body_scratch + 2:sc + n_body_scratch + 2 + n_big]))
    stage, load_sem, store_sem = refs[sc + n_body_scratch + 2 + n_big:]
    big_f32 = {name: ref for name, ref in zip(_MIXER_WEIGHTS, weights) if name in big_vmem}
    weights = [big_vmem.get(name, ref) for name, ref in zip(_MIXER_WEIGHTS, weights)]

    step = pl.program_id(0)
    n_steps = pl.num_programs(0)
    slot = step % 2
    copies = functools.partial(_seq_major_tile_copies, xs_hbm, xin, sem, n_seq=n_seq, n_t=n_t, to_hbm=False)
    write_back = [pltpu.make_async_copy(big_vmem[name], dst, store_sem.at[0])
                  for name, dst in zip(_MIXER_BIG_WEIGHTS, big_out)]

    @pl.when(step == 0)
    def _():
        for c in copies(0, 0):
            c.start()
        xin[0, 0:N_META, :, :] = jnp.broadcast_to(meta_ref[...][:, None, :], (N_META, n_seq, D_MODEL))

        @pl.when(n_steps > 1)
        def _():
            for c in copies(step + 1, 1):
                c.start()
        _load_and_cast_mixer_weights(big_f32, big_vmem, stage, load_sem, layer)
        for c in write_back:
            c.start()
        for c in copies(0, 0):
            c.wait()

    @pl.when(step > 0)
    def _():
        @pl.when(step + 1 < n_steps)
        def _():
            for c in copies(step + 1, 1 - slot):
                c.start()
        for c in copies(step, slot):
            c.wait()

    _mixer_body(lambda: xin[slot].reshape(n_t * n_seq, D_MODEL), step, step + zero_ref[0] >= 0, lambda region: None,
                *states, *weights, *outs, *body_scratch, n_seq=n_seq, n_t=n_t, layer=layer)
    _cast_weight_blocks(_FFN_BIG_WEIGHTS, cast_src, cast_dst)

    @pl.when(step == n_steps - 1)
    def _():
        for c in write_back:
            c.wait()


def _ffn_body(x, *, g2_ref, wg_ref, wu_ref, wd_ref, gf_ref, final_norm, layer):
    v = _rmsnorm(x, g2_ref[layer:layer + 1, :]).astype(BF16)
    half = _dot(v, wg_ref[...])
    hid = ((half * jnp.tanh(half) + half) * _dot(v, wu_ref[...])).astype(BF16)
    y = x + _dot(hid, wd_ref[...])
    if final_norm:
        y = _rmsnorm(y, gf_ref[...])
    return y


def _cast_weight_blocks(names, src_refs, dst_refs):
    @pl.when(pl.program_id(0) < N_CONVERT_STEPS)
    def _():
        for name, src, dst in zip(names, src_refs, dst_refs):
            if name == "w_in":
                plain = (N_SPLITS - 2) * D_MODEL
                dst[:, :plain] = src[:, :plain].astype(BF16)
                dst[:, plain:] = (0.5 * src[:, plain:]).astype(BF16)
            elif name == "w_ff_gate":
                dst[...] = (0.5 * src[...]).astype(BF16)
            else:
                dst[...] = src[...].astype(BF16)


def _ffn_kernel(xp_ref, xs_ref, g2_ref, wg_ref, wu_ref, wd_ref, gf_ref, *rest, final_norm, layer, n_prompt_steps):
    n_big = (len(rest) - 2) // 2
    next_f32, (op_ref, os_ref), next_bf16 = rest[:n_big], rest[n_big:n_big + 2], rest[n_big + 2:]
    body = functools.partial(_ffn_body, g2_ref=g2_ref, wg_ref=wg_ref, wu_ref=wu_ref, wd_ref=wd_ref, gf_ref=gf_ref,
                             final_norm=final_norm, layer=layer)
    step = pl.program_id(0)

    @pl.when(step < n_prompt_steps)
    def _():
        op_ref[...] = body(xp_ref[...])

    @pl.when(step == n_prompt_steps)
    def _():
        os_ref[...] = body(xs_ref[...])

    if n_big:
        _cast_weight_blocks(_BIG_WEIGHTS, next_f32, next_bf16)


def _ffn_to_seq_major_kernel(xp_ref, xs_ref, g2_ref, wg_ref, wu_ref, wd_ref, gf_ref, yp_hbm, os_ref, yout, sem,
                             *, n_seq, n_t, layer, n_prompt_steps):
    step = pl.program_id(0)
    n_steps = n_prompt_steps
    slot = step % 2
    copies = functools.partial(_seq_major_tile_copies, yp_hbm, yout, sem, n_seq=n_seq, n_t=n_t, to_hbm=True)
    body = functools.partial(_ffn_body, g2_ref=g2_ref, wg_ref=wg_ref, wu_ref=wu_ref, wd_ref=wd_ref, gf_ref=gf_ref,
                             final_norm=True, layer=layer)

    @pl.when(step == n_steps)
    def _():
        os_ref[...] = body(xs_ref[...])

    @pl.when(step < n_steps)
    def _():
        y = body(xp_ref[...])

        @pl.when(step == 1)
        def _():
            for c in copies(0, 0):
                c.wait()

        @pl.when(step > 1)
        def _():
            for c in copies(step - 1, 1 - slot):
                c.wait()

        yout[slot] = y.reshape(n_t, n_seq, D_MODEL)

        @pl.when(step == 0)
        def _():
            for c in copies(0, 0):
                c.start()
            if n_steps == 1:
                for c in copies(0, 0):
                    c.wait()

        @pl.when(step > 0)
        def _():
            for c in copies(step, slot):
                c.start()

            @pl.when(step == n_steps - 1)
            def _():
                for c in copies(step, slot):
                    c.wait()


def _resident(shape, index):
    return pl.BlockSpec(shape, index, pipeline_mode=pl.Buffered(1))


def _weight_operands(names, small, big, layer):
    specs, arrays = [], []
    for name in names:
        if name not in big and name not in small:
            specs.append(None)
            arrays.append(None)
            continue
        arr = big[name] if name in big else small[name]
        if name in big or arr.ndim == 2:
            specs.append(_resident(arr.shape, lambda i, nd=arr.ndim: (0,) * nd))
        else:
            specs.append(_resident((None,) + arr.shape[1:], lambda i, nd=arr.ndim: (layer,) + (0,) * (nd - 1)))
        arrays.append(arr)
    return specs, arrays


_MIXER_WEIGHTS = ("norm1_g", "w_in", "rnn_conv_w", "rnn_conv_b", "w_gate", "gate_a_b", "gate_x_b",
                  "lru_lambda", "w_branch_a", "sc_conv_w", "w_branch_b", "w_out")
_FFN_WEIGHTS = ("norm2_g", "w_ff_gate", "w_ff_up", "w_ff_down")
_BIG_WEIGHTS = ("w_in", "w_branch_a", "w_branch_b", "w_out", "w_ff_gate", "w_ff_up", "w_ff_down")
_MIXER_BIG_WEIGHTS, _FFN_BIG_WEIGHTS = _BIG_WEIGHTS[:4], _BIG_WEIGHTS[4:]
N_CONVERT_STEPS = 16
_COMPILER_PARAMS = pltpu.CompilerParams(dimension_semantics=("arbitrary",), vmem_limit_bytes=VMEM_LIMIT_BYTES)


def _cast_operands(names, stacked_f32, layer, n_steps):
    assert n_steps >= N_CONVERT_STEPS
    block = lambda i: jnp.minimum(i, N_CONVERT_STEPS - 1)
    in_specs, out_specs, out_shapes = [], [], []
    for name in names:
        _, n_in, n_out = stacked_f32[name].shape
        blk = n_in // N_CONVERT_STEPS
        assert blk * N_CONVERT_STEPS == n_in and blk % (2 * SUBLANES) == 0
        in_specs.append(pl.BlockSpec((None, blk, n_out), lambda i: (layer, block(i), 0)))
        out_specs.append(pl.BlockSpec((blk, n_out), lambda i: (block(i), 0)))
        out_shapes.append(jax.ShapeDtypeStruct((n_in, n_out), BF16))
    return in_specs, out_specs, out_shapes, [stacked_f32[name] for name in names]


def _mixer_call(x, h0, rc0, sc0, small, big, layer, *, n_seq, n_t, meta=None, big_f32=None):
    rows = n_seq * n_t
    rc_rows = (RNN_CONV_W - 1) * n_seq
    sc_rows = (SC_CONV_W - 1) * n_seq
    row_spec = pl.BlockSpec((rows, D_MODEL), lambda i: (i, 0))
    state_layer = layer if h0.shape[0] > 1 else 0
    state_spec = lambda r: _resident((None, r, D_MODEL), lambda i: (state_layer, 0, 0))
    scratch = [pltpu.VMEM((rows, D_MODEL), BF16),
               pltpu.VMEM((rc_rows + rows, D_MODEL), F32),
               pltpu.VMEM((sc_rows + rows, D_MODEL), F32),
               pltpu.VMEM((rows, D_MODEL), F32),
               pltpu.VMEM((rows, D_MODEL), F32),
               pltpu.VMEM((rows, D_MODEL), BF16),
               pltpu.VMEM((rows, D_MODEL), BF16),
               pltpu.VMEM((rows, D_MODEL), F32),
               pltpu.VMEM((rows, D_MODEL), F32)]
    w_specs, w_arrays = _weight_operands(_MIXER_WEIGHTS, small, big, layer)
    zero, smem_spec = jnp.zeros((1,), jnp.int32), pl.BlockSpec(memory_space=pltpu.SMEM)
    if meta is None:
        body, x_args, n_rows = _mixer_kernel, (x, zero), x.shape[0]
        x_specs = [row_spec, smem_spec]
        for name in _MIXER_BIG_WEIGHTS:
            w_specs[_MIXER_WEIGHTS.index(name)] = pl.BlockSpec(memory_space=pl.ANY)
            scratch.append(pltpu.VMEM(big[name].shape, BF16))
        scratch.append(pltpu.SemaphoreType.DMA((3,)))
    else:
        assert n_t % SUBLANES == 0 and n_t > N_META and n_seq == SUBLANES
        body, x_args, n_rows = _mixer_from_seq_major_kernel, (x, meta, zero), (x.shape[1] + N_META) * n_seq
        x_specs = [pl.BlockSpec(memory_space=pl.ANY), _resident(meta.shape, lambda i: (0, 0)), smem_spec]
        scratch += [pltpu.VMEM((2, n_t, n_seq, D_MODEL), F32), pltpu.SemaphoreType.DMA((2,))]
    assert n_rows % rows == 0
    cast_in, cast_out, cast_shapes, cast_arrays = [], [], [], []
    if meta is not None:
        cast_in, cast_out, cast_shapes, cast_arrays = _cast_operands(_FFN_BIG_WEIGHTS, big_f32, layer, n_rows // rows)
        for name in _MIXER_BIG_WEIGHTS:
            i, shape = _MIXER_WEIGHTS.index(name), big_f32[name].shape[1:]
            w_specs[i], w_arrays[i] = pl.BlockSpec(memory_space=pl.ANY), big_f32[name]
            cast_out.append(pl.BlockSpec(memory_space=pl.ANY))
            cast_shapes.append(jax.ShapeDtypeStruct(shape, BF16))
            scratch.append(pltpu.VMEM(shape, BF16))
        scratch += [pltpu.VMEM((2, D_MODEL, D_MODEL), F32), pltpu.SemaphoreType.DMA((2,)), pltpu.SemaphoreType.DMA((1,))]
    outs = pl.pallas_call(
        functools.partial(body, n_seq=n_seq, n_t=n_t, layer=layer),
        grid=(n_rows // rows,),
        in_specs=x_specs + [state_spec(n_seq), state_spec(rc_rows), state_spec(sc_rows)] + w_specs + cast_in,
        out_specs=[row_spec,
                   pl.BlockSpec((n_seq, D_MODEL), lambda i: (0, 0)),
                   pl.BlockSpec((rc_rows, D_MODEL), lambda i: (0, 0)),
                   pl.BlockSpec((sc_rows, D_MODEL), lambda i: (0, 0))] + cast_out,
        out_shape=[jax.ShapeDtypeStruct((n_rows, D_MODEL), F32),
                   jax.ShapeDtypeStruct((n_seq, D_MODEL), F32),
                   jax.ShapeDtypeStruct((rc_rows, D_MODEL), F32),
                   jax.ShapeDtypeStruct((sc_rows, D_MODEL), F32)] + cast_shapes,
        scratch_shapes=scratch,
        compiler_params=_COMPILER_PARAMS,
        name=f"mixer_s{n_seq}" + ("_in" if meta is not None else ""),
    )(*x_args, h0, rc0, sc0, *w_arrays, *cast_arrays)
    if meta is None:
        return outs
    n_ffn = len(_FFN_BIG_WEIGHTS)
    return (*outs[:4], dict(zip(_FFN_BIG_WEIGHTS, outs[4:4 + n_ffn])), dict(zip(_MIXER_BIG_WEIGHTS, outs[4 + n_ffn:])))


def _ffn_call(xp, xs, small, big, layer, *, rows, final_norm, seq_major_out=None, next_f32=None):
    n_rows = xp.shape[0]
    assert n_rows % rows == 0
    n_prompt_steps = n_rows // rows
    prompt_spec = pl.BlockSpec((rows, D_MODEL), lambda i: (jnp.minimum(i, n_prompt_steps - 1), 0))
    sample_spec = pl.BlockSpec(xs.shape, lambda i: (0, 0))
    gf = small["final_norm_g"]
    w_specs, w_arrays = _weight_operands(_FFN_WEIGHTS, small, big, layer)
    in_specs = [prompt_spec, _resident(xs.shape, lambda i: (0, 0))] + w_specs + [_resident(gf.shape, lambda i: (0, 0))]
    args = (xp, xs, *w_arrays, gf)
    sample_shape = jax.ShapeDtypeStruct(xs.shape, F32)
    if seq_major_out is not None:
        n_seq, n_t = seq_major_out
        assert final_norm and next_f32 is None and rows == n_seq * n_t and n_t % SUBLANES == 0 and n_t > N_META
        return pl.pallas_call(
            functools.partial(_ffn_to_seq_major_kernel, n_seq=n_seq, n_t=n_t, layer=layer,
                              n_prompt_steps=n_prompt_steps),
            grid=(n_prompt_steps + 1,), in_specs=in_specs,
            out_specs=[pl.BlockSpec(memory_space=pl.ANY), sample_spec],
            out_shape=[jax.ShapeDtypeStruct((n_seq, n_rows // n_seq - N_META, D_MODEL), F32), sample_shape],
            scratch_shapes=[pltpu.VMEM((2, n_t, n_seq, D_MODEL), F32), pltpu.SemaphoreType.DMA((2,))],
            compiler_params=_COMPILER_PARAMS, name=f"ffn_r{rows}_out",
        )(*args)
    out_specs, out_shape = [prompt_spec, sample_spec], [jax.ShapeDtypeStruct((n_rows, D_MODEL), F32), sample_shape]
    if next_f32 is not None:
        cast_in, cast_out, cast_shapes, cast_arrays = _cast_operands(_BIG_WEIGHTS, next_f32, layer + 1,
                                                                     n_prompt_steps)
        in_specs, out_specs, out_shape = in_specs + cast_in, out_specs + cast_out, out_shape + cast_shapes
        args += tuple(cast_arrays)
    outs = pl.pallas_call(
        functools.partial(_ffn_kernel, final_norm=final_norm, layer=layer, n_prompt_steps=n_prompt_steps),
        grid=(n_prompt_steps + 1,), in_specs=in_specs, out_specs=out_specs, out_shape=out_shape,
        compiler_params=_COMPILER_PARAMS,
        name=f"ffn_r{rows}" + ("_cast" if next_f32 is not None else ""),
    )(*args)
    if next_f32 is None:
        return outs
    return outs[0], outs[1], dict(zip(_BIG_WEIGHTS, outs[2:]))


def _block_diag_gates(gate_a_w, gate_x_w):
    hpc = GATE_CHUNK // RNN_HEAD_DIM
    own_block = (jnp.arange(GATE_CHUNK) // RNN_HEAD_DIM)[None, :] == jnp.arange(hpc)[:, None]

    def bd(wt):
        wt = wt.reshape(DEPTH, N_GATE_CHUNKS, hpc, RNN_HEAD_DIM, RNN_HEAD_DIM)
        across = jnp.tile(wt, (1, 1, 1, 1, hpc))
        return jnp.where(own_block[:, None, :], across, 0.0).reshape(DEPTH, N_GATE_CHUNKS, GATE_CHUNK, GATE_CHUNK)

    return jnp.concatenate([bd(gate_a_w), bd(gate_x_w)], axis=-1).astype(BF16)


def _time_major(states):
    return jnp.swapaxes(states, 1, 2).reshape(states.shape[0], -1, D_MODEL)


def _seq_major(states, n_seq):
    return jnp.swapaxes(jnp.stack(states).reshape(len(states), -1, n_seq, D_MODEL), 1, 2)


PROMPT_N_T = 86
PROMPT_FFN_ROWS = 688
PROMPT_IO_N_T = 48


def kernel(x_prompt, x_sample, state_rnn_h, state_rnn_conv, state_sc_conv, meta_tokens, norm1_g, w_in, rnn_conv_w, rnn_conv_b, gate_a_w, gate_a_b, gate_x_w, gate_x_b, lru_lambda, w_branch_a, sc_conv_w, w_branch_b, w_out, norm2_g, w_ff_gate, w_ff_up, w_ff_down, final_norm_g):
    small = {
        "norm1_g": norm1_g, "rnn_conv_w": rnn_conv_w, "rnn_conv_b": rnn_conv_b,
        "w_gate": _block_diag_gates(0.5 * gate_a_w, 0.5 * gate_x_w),
        "gate_a_b": gate_a_b, "gate_x_b": gate_x_b, "lru_lambda": lru_lambda,
        "sc_conv_w": sc_conv_w, "norm2_g": norm2_g, "final_norm_g": final_norm_g.reshape(1, D_MODEL),
    }
    big_f32 = {"w_in": w_in, "w_branch_a": w_branch_a, "w_branch_b": w_branch_b, "w_out": w_out,
               "w_ff_gate": w_ff_gate, "w_ff_up": w_ff_up, "w_ff_down": w_ff_down}
    big = [{}]
    dt = x_prompt.dtype

    bp = x_prompt.shape[0]
    bs, t_s, _ = x_sample.shape
    zeros = lambda k: jnp.zeros((1, k * bp, D_MODEL), dt)
    prompt_states = (zeros(1), zeros(RNN_CONV_W - 1), zeros(SC_CONV_W - 1))
    sample_states = (state_rnn_h, _time_major(state_rnn_conv), _time_major(state_sc_conv))
    xp = x_prompt
    xs = jnp.swapaxes(x_sample, 0, 1).reshape(t_s * bs, D_MODEL)
    new_p, new_s = [], []
    for layer in range(DEPTH):
        if layer == 0:
            xp, *st_p, ffn_big, mixer_big = _mixer_call(xp, *prompt_states, small, big[layer], layer, n_seq=bp,
                                                        n_t=PROMPT_IO_N_T, meta=meta_tokens.astype(dt), big_f32=big_f32)
            big[layer] = {**mixer_big, **ffn_big}
        else:
            xp, *st_p = _mixer_call(xp, *prompt_states, small, big[layer], layer, n_seq=bp, n_t=PROMPT_N_T)
        xs, *st_s = _mixer_call(xs, *sample_states, small, big[layer], layer, n_seq=bs, n_t=t_s)
        new_p.append(st_p)
        new_s.append(st_s)
        if layer == DEPTH - 1:
            y_prompt, ys = _ffn_call(xp, xs, small, big[layer], layer, rows=bp * PROMPT_IO_N_T, final_norm=True,
                                     seq_major_out=(bp, PROMPT_IO_N_T))
        else:
            xp, xs, next_big = _ffn_call(xp, xs, small, big[layer], layer, rows=PROMPT_FFN_ROWS, final_norm=False,
                                         next_f32=big_f32)
            big.append(next_big)
    y_sample = jnp.swapaxes(ys.reshape(t_s, bs, D_MODEL), 0, 1)

    def collect(states, n_seq):
        hs, rcs, scs = zip(*states)
        return jnp.stack(hs), _seq_major(list(rcs), n_seq), _seq_major(list(scs), n_seq)

    rnn_h_p, rnn_conv_p, sc_conv_p = collect(new_p, bp)
    rnn_h_s, rnn_conv_s, sc_conv_s = collect(new_s, bs)

    return (y_prompt, y_sample, rnn_h_p, rnn_conv_p, sc_conv_p, rnn_h_s, rnn_conv_s, sc_conv_s)
```

```python
import functools
import math

import jax
import jax.numpy as jnp
from jax import lax
from jax.experimental import pallas as pl
from jax.experimental.pallas import tpu as pltpu

D_MODEL = 1024
DEPTH = 4
N_META = 16
N_RNN_HEADS = 16
RNN_HEAD_DIM = D_MODEL // N_RNN_HEADS
RNN_CONV_W = 4
SC_CONV_W = 3
LRU_C = 8.0
D_FF = 2816
EPS = 1e-6
N_SPLITS = 7
GATE_CHUNK = 256
N_GATE_CHUNKS = D_MODEL // GATE_CHUNK
SUBLANES = 8

V7X_VMEM_BYTES = 64 * 1024 * 1024
VMEM_LIMIT_BYTES = V7X_VMEM_BYTES - 2 * 1024 * 1024

F32 = jnp.float32
BF16 = jnp.bfloat16


def _sigmoid_of_half(half_x):
    return 0.5 * jnp.tanh(half_x) + 0.5


def _gelu_tanh(x):
    c = math.sqrt(2.0 / math.pi)
    half_x = 0.5 * x
    return half_x * jnp.tanh(x * (c + (c * 0.044715) * (x * x))) + half_x


def _softplus(x):
    return jnp.maximum(x, 0.0) + jnp.log1p(jnp.exp(-jnp.abs(x)))


def _rmsnorm(x, g):
    ms = jnp.mean(x * x, axis=-1, keepdims=True)
    return x * lax.rsqrt(ms + EPS) * g


def _dot(a, b):
    return jnp.dot(a, b, preferred_element_type=F32)


def _mixer_body(read_x, step, region_gate, wait_weights, h0_ref, rc0_ref, sc0_ref, g1_ref, win_ref, cw_ref, cb_ref, wgate_ref,
                ba_ref, bx_ref, lam_ref, wba_ref, scw_ref, wbb_ref, wout_ref,
                xo_ref, h_ref, rc_ref, sc_ref,
                u_bf, xrbuf, chbuf, a_buf, h_buf, ya_buf, yb_buf, ga_buf, gb_buf, *, n_seq, n_t, layer):
    rows = n_seq * n_t
    rc_rows = (RNN_CONV_W - 1) * n_seq
    sc_rows = (SC_CONV_W - 1) * n_seq
    row = slice(layer, layer + 1)

    def proj(k):
        return _dot(u_bf[...], win_ref[:, k * D_MODEL:(k + 1) * D_MODEL])

    wait_weights(1)

    @pl.when(step == 0)
    def _():
        h_ref[...] = h0_ref[...]
        xrbuf[0:rc_rows, :] = rc0_ref[...]
        chbuf[0:sc_rows, :] = sc0_ref[...]

    u_bf[...] = _rmsnorm(read_x(), g1_ref[row, :]).astype(BF16)
    xrbuf[rc_rows:rc_rows + rows, :] = proj(0)
    xc = cb_ref[row, :] + xrbuf[0:rows, :] * cw_ref[0:1, :]
    for k in range(1, RNN_CONV_W):
        xc = xc + xrbuf[k * n_seq:k * n_seq + rows, :] * cw_ref[k:k + 1, :]
    h_buf[...] = xc
    xc_b = xc.astype(BF16)
    for j in range(N_GATE_CHUNKS):
        cols = slice(j * GATE_CHUNK, (j + 1) * GATE_CHUNK)
        gates = _dot(xc_b[:, cols], wgate_ref[j])
        ga_buf[:, cols] = gates[:, :GATE_CHUNK]
        gb_buf[:, cols] = gates[:, GATE_CHUNK:]
    chbuf[sc_rows:sc_rows + rows, :] = proj(3) * proj(4)

    @pl.when(region_gate)
    def _():
        wait_weights(2)
        half_log_a_max = (-0.5 * LRU_C) * _softplus(-lam_ref[row, :])
        t_r = jnp.tanh(ga_buf[...] + 0.5 * ba_ref[row, :])
        log_a = t_r * half_log_a_max + half_log_a_max
        a = jnp.exp(log_a)
        one_minus_a2 = jnp.maximum(jnp.tanh(log_a) * (-1.0 - a * a), 1e-12)
        mult = one_minus_a2 * lax.rsqrt(one_minus_a2)
        gate_i = _sigmoid_of_half(gb_buf[...] + 0.5 * bx_ref[row, :])
        a_buf[...] = a
        h_buf[...] = mult * gate_i * h_buf[...]

        if n_seq == SUBLANES:
            def scan_step(t, h):
                sl = pl.ds(pl.multiple_of(t * n_seq, n_seq), n_seq)
                h = a_buf[sl, :] * h + h_buf[sl, :]
                h_buf[sl, :] = h
                return h
            h_ref[...] = lax.fori_loop(0, n_t, scan_step, h_ref[...], unroll=True)
        else:
            for t in range(n_t):
                sl = slice(t * n_seq, (t + 1) * n_seq)
                prev = h_ref[...] if t == 0 else h_buf[(t - 1) * n_seq:t * n_seq, :]
                h_buf[sl, :] = a_buf[sl, :] * prev + h_buf[sl, :]
            h_ref[...] = h_buf[(n_t - 1) * n_seq:n_t * n_seq, :]

        ya_buf[...] = (h_buf[...] * _gelu_tanh(proj(1))).astype(BF16)

        vc = chbuf[0:rows, :] * scw_ref[0:1, :]
        for k in range(1, SC_CONV_W):
            vc = vc + chbuf[k * n_seq:k * n_seq + rows, :] * scw_ref[k:k + 1, :]
        yb_buf[...] = (proj(2) * vc).astype(BF16)

        ga_buf[...] = _sigmoid_of_half(proj(5))
        gb_buf[...] = _sigmoid_of_half(proj(6))

        new_rc = xrbuf[rows:rows + rc_rows, :]
        new_sc = chbuf[rows:rows + sc_rows, :]
        xrbuf[0:rc_rows, :] = new_rc
        chbuf[0:sc_rows, :] = new_sc

        @pl.when(step == pl.num_programs(0) - 1)
        def _():
            rc_ref[...] = new_rc
            sc_ref[...] = new_sc

    @pl.when(region_gate)
    def _():
        wait_weights(3)
        m = ga_buf[...] * _dot(ya_buf[...], wba_ref[...])
        m = m + gb_buf[...] * _dot(yb_buf[...], wbb_ref[...])
        xo_ref[...] = read_x() + _dot(m.astype(BF16), wout_ref[...])


_REGION_W_IN_GROUPS = {1: (0, 3, 4), 2: (1, 2, 5, 6)}


def _mixer_kernel(x_ref, zero_ref, *refs, n_seq, n_t, layer):
    refs = list(refs)
    n_w, n_big = len(_MIXER_WEIGHTS), len(_MIXER_BIG_WEIGHTS)
    *refs, sem = refs
    big_vmem = dict(zip(_MIXER_BIG_WEIGHTS, refs[-n_big:]))
    refs = refs[:-n_big]
    states, weights, rest = refs[:3], refs[3:3 + n_w], refs[3 + n_w:]
    big_hbm = {name: ref for name, ref in zip(_MIXER_WEIGHTS, weights) if name in big_vmem}
    weights = [big_vmem.get(name, ref) for name, ref in zip(_MIXER_WEIGHTS, weights)]
    step = pl.program_id(0)

    def region_copies(region):
        if region in _REGION_W_IN_GROUPS:
            pairs = [(big_hbm["w_in"].at[:, pl.ds(k * D_MODEL, D_MODEL)], big_vmem["w_in"].at[:, pl.ds(k * D_MODEL, D_MODEL)])
                     for k in _REGION_W_IN_GROUPS[region]]
        else:
            pairs = [(big_hbm[name], big_vmem[name]) for name in _MIXER_BIG_WEIGHTS[1:]]
        return [pltpu.make_async_copy(src, dst, sem.at[region - 1]) for src, dst in pairs]

    @pl.when(step == 0)
    def _():
        for region in (1, 2, 3):
            for c in region_copies(region):
                c.start()

    def wait_weights(region):
        @pl.when(step == 0)
        def _():
            for c in region_copies(region):
                c.wait()

    _mixer_body(lambda: x_ref[...], step, step + zero_ref[0] >= 0, wait_weights, *states, *weights, *rest,
                n_seq=n_seq, n_t=n_t, layer=layer)


def _seq_major_tile_copies(seq_hbm, tile_buf, sem, tile, slot, *, n_seq, n_t, to_hbm):
    copies = []
    for b in range(n_seq):
        if isinstance(tile, int):
            assert tile == 0
            hbm = seq_hbm.at[b, pl.ds(0, n_t - N_META), :]
            vmem = tile_buf.at[slot, pl.ds(N_META, n_t - N_META), b, :]
        else:
            t0 = pl.multiple_of(tile * n_t - N_META, SUBLANES)
            hbm = seq_hbm.at[b, pl.ds(t0, n_t), :]
            vmem = tile_buf.at[slot, :, b, :]
        src, dst = (vmem, hbm) if to_hbm else (hbm, vmem)
        copies.append(pltpu.make_async_copy(src, dst, sem.at[slot]))
    return copies


def _load_and_cast_mixer_weights(f32_hbm, bf16_vmem, stage, sem, layer):
    chunks = []
    for k in range(N_SPLITS):
        cols = pl.ds(k * D_MODEL, D_MODEL)
        chunks.append((f32_hbm["w_in"].at[layer, :, cols], bf16_vmem["w_in"].at[:, cols], k >= N_SPLITS - 2))
    for name in _MIXER_BIG_WEIGHTS[1:]:
        chunks.append((f32_hbm[name].at[layer], bf16_vmem[name], False))
    copy = lambda c: pltpu.make_async_copy(chunks[c][0], stage.at[c % 2], sem.at[c % 2])
    copy(0).start()
    copy(1).start()
    for c, (_, dst, halve) in enumerate(chunks):
        copy(c).wait()
        value = stage[c % 2]
        dst[...] = (0.5 * value if halve else value).astype(BF16)
        if c + 2 < len(chunks):
            copy(c + 2).start()


def _mixer_from_seq_major_kernel(xs_hbm, meta_ref, zero_ref, *refs, n_seq, n_t, layer):
    refs = list(refs)
    n_w, n_out, n_cast, n_big = len(_MIXER_WEIGHTS), 4, len(_FFN_BIG_WEIGHTS), len(_MIXER_BIG_WEIGHTS)
    states, weights, cast_src = refs[:3], refs[3:3 + n_w], refs[3 + n_w:3 + n_w + n_cast]
    o = 3 + n_w + n_cast
    outs, cast_dst, big_out = refs[o:o + n_out], refs[o + n_out:o + n_out + n_cast], refs[o + n_out + n_cast:o + n_out + n_cast + n_big]
    sc = o + n_out + n_cast + n_big
    n_body_scratch = len(refs) - sc - 2 - n_big - 3
    body_scratch, (xin, sem) = refs[sc:sc + n_body_scratch], refs[sc + n_body_scratch:sc + n_body_scratch + 2]
    big_vmem = dict(zip(_MIXER_BIG_WEIGHTS, refs[sc + n_body_scratch + 2:sc + n_body_scratch + 2 + n_big]))
    stage, load_sem, store_sem = refs[sc + n_body_scratch + 2 + n_big:]
    big_f32 = {name: ref for name, ref in zip(_MIXER_WEIGHTS, weights) if name in big_vmem}
    weights = [big_vmem.get(name, ref) for name, ref in zip(_MIXER_WEIGHTS, weights)]

    step = pl.program_id(0)
    n_steps = pl.num_programs(0)
    slot = step % 2
    copies = functools.partial(_seq_major_tile_copies, xs_hbm, xin, sem, n_seq=n_seq, n_t=n_t, to_hbm=False)
    write_back = [pltpu.make_async_copy(big_vmem[name], dst, store_sem.at[0])
                  for name, dst in zip(_MIXER_BIG_WEIGHTS, big_out)]

    @pl.when(step == 0)
    def _():
        for c in copies(0, 0):
            c.start()
        xin[0, 0:N_META, :, :] = jnp.broadcast_to(meta_ref[...][:, None, :], (N_META, n_seq, D_MODEL))

        @pl.when(n_steps > 1)
        def _():
            for c in copies(step + 1, 1):
                c.start()
        _load_and_cast_mixer_weights(big_f32, big_vmem, stage, load_sem, layer)
        for c in write_back:
            c.start()
        for c in copies(0, 0):
            c.wait()

    @pl.when(step > 0)
    def _():
        @pl.when(step + 1 < n_steps)
        def _():
            for c in copies(step + 1, 1 - slot):
                c.start()
        for c in copies(step, slot):
            c.wait()

    _mixer_body(lambda: xin[slot].reshape(n_t * n_seq, D_MODEL), step, step + zero_ref[0] >= 0, lambda region: None,
                *states, *weights, *outs, *body_scratch, n_seq=n_seq, n_t=n_t, layer=layer)
    _cast_weight_blocks(_FFN_BIG_WEIGHTS, cast_src, cast_dst)

    @pl.when(step == n_steps - 1)
    def _():
        for c in write_back:
            c.wait()


def _ffn_body(x, *, g2_ref, wg_ref, wu_ref, wd_ref, gf_ref, final_norm, layer):
    v = _rmsnorm(x, g2_ref[layer:layer + 1, :]).astype(BF16)
    half = _dot(v, wg_ref[...])
    hid = ((half * jnp.tanh(half) + half) * _dot(v, wu_ref[...])).astype(BF16)
    y = x + _dot(hid, wd_ref[...])
    if final_norm:
        y = _rmsnorm(y, gf_ref[...])
    return y


def _cast_weight_blocks(names, src_refs, dst_refs):
    @pl.when(pl.program_id(0) < N_CONVERT_STEPS)
    def _():
        for name, src, dst in zip(names, src_refs, dst_refs):
            if name == "w_in":
                plain = (N_SPLITS - 2) * D_MODEL
                dst[:, :plain] = src[:, :plain].astype(BF16)
                dst[:, plain:] = (0.5 * src[:, plain:]).astype(BF16)
            elif name == "w_ff_gate":
                dst[...] = (0.5 * src[...]).astype(BF16)
            else:
                dst[...] = src[...].astype(BF16)


def _ffn_kernel(xp_ref, xs_ref, g2_ref, wg_ref, wu_ref, wd_ref, gf_ref, *rest, final_norm, layer, n_prompt_steps):
    n_big = (len(rest) - 2) // 2
    next_f32, (op_ref, os_ref), next_bf16 = rest[:n_big], rest[n_big:n_big + 2], rest[n_big + 2:]
    body = functools.partial(_ffn_body, g2_ref=g2_ref, wg_ref=wg_ref, wu_ref=wu_ref, wd_ref=wd_ref, gf_ref=gf_ref,
                             final_norm=final_norm, layer=layer)
    step = pl.program_id(0)

    @pl.when(step < n_prompt_steps)
    def _():
        op_ref[...] = body(xp_ref[...])

    @pl.when(step == n_prompt_steps)
    def _():
        os_ref[...] = body(xs_ref[...])

    if n_big:
        _cast_weight_blocks(_BIG_WEIGHTS, next_f32, next_bf16)


def _ffn_to_seq_major_kernel(xp_ref, xs_ref, g2_ref, wg_ref, wu_ref, wd_ref, gf_ref, yp_hbm, os_ref, yout, sem,
                             *, n_seq, n_t, layer, n_prompt_steps):
    step = pl.program_id(0)
    n_steps = n_prompt_steps
    slot = step % 2
    copies = functools.partial(_seq_major_tile_copies, yp_hbm, yout, sem, n_seq=n_seq, n_t=n_t, to_hbm=True)
    body = functools.partial(_ffn_body, g2_ref=g2_ref, wg_ref=wg_ref, wu_ref=wu_ref, wd_ref=wd_ref, gf_ref=gf_ref,
                             final_norm=True, layer=layer)

    @pl.when(step == n_steps)
    def _():
        os_ref[...] = body(xs_ref[...])

    @pl.when(step < n_steps)
    def _():
        y = body(xp_ref[...])

        @pl.when(step == 1)
        def _():
            for c in copies(0, 0):
                c.wait()

        @pl.when(step > 1)
        def _():
            for c in copies(step - 1, 1 - slot):
                c.wait()

        yout[slot] = y.reshape(n_t, n_seq, D_MODEL)

        @pl.when(step == 0)
        def _():
            for c in copies(0, 0):
                c.start()
            if n_steps == 1:
                for c in copies(0, 0):
                    c.wait()

        @pl.when(step > 0)
        def _():
            for c in copies(step, slot):
                c.start()

            @pl.when(step == n_steps - 1)
            def _():
                for c in copies(step, slot):
                    c.wait()


def _resident(shape, index):
    return pl.BlockSpec(shape, index, pipeline_mode=pl.Buffered(1))


def _weight_operands(names, small, big, layer):
    specs, arrays = [], []
    for name in names:
        if name not in big and name not in small:
            specs.append(None)
            arrays.append(None)
            continue
        arr = big[name] if name in big else small[name]
        if name in big or arr.ndim == 2:
            specs.append(_resident(arr.shape, lambda i, nd=arr.ndim: (0,) * nd))
        else:
            specs.append(_resident((None,) + arr.shape[1:], lambda i, nd=arr.ndim: (layer,) + (0,) * (nd - 1)))
        arrays.append(arr)
    return specs, arrays


_MIXER_WEIGHTS = ("norm1_g", "w_in", "rnn_conv_w", "rnn_conv_b", "w_gate", "gate_a_b", "gate_x_b",
                  "lru_lambda", "w_branch_a", "sc_conv_w", "w_branch_b", "w_out")
_FFN_WEIGHTS = ("norm2_g", "w_ff_gate", "w_ff_up", "w_ff_down")
_BIG_WEIGHTS = ("w_in", "w_branch_a", "w_branch_b", "w_out", "w_ff_gate", "w_ff_up", "w_ff_down")
_MIXER_BIG_WEIGHTS, _FFN_BIG_WEIGHTS = _BIG_WEIGHTS[:4], _BIG_WEIGHTS[4:]
N_CONVERT_STEPS = 16
_COMPILER_PARAMS = pltpu.CompilerParams(dimension_semantics=("arbitrary",), vmem_limit_bytes=VMEM_LIMIT_BYTES)


def _cast_operands(names, stacked_f32, layer, n_steps):
    assert n_steps >= N_CONVERT_STEPS
    block = lambda i: jnp.minimum(i, N_CONVERT_STEPS - 1)
    in_specs, out_specs, out_shapes = [], [], []
    for name in names:
        _, n_in, n_out = stacked_f32[name].shape
        blk = n_in // N_CONVERT_STEPS
        assert blk * N_CONVERT_STEPS == n_in and blk % (2 * SUBLANES) == 0
        in_specs.append(pl.BlockSpec((None, blk, n_out), lambda i: (layer, block(i), 0)))
        out_specs.append(pl.BlockSpec((blk, n_out), lambda i: (block(i), 0)))
        out_shapes.append(jax.ShapeDtypeStruct((n_in, n_out), BF16))
    return in_specs, out_specs, out_shapes, [stacked_f32[name] for name in names]


def _mixer_call(x, h0, rc0, sc0, small, big, layer, *, n_seq, n_t, meta=None, big_f32=None):
    rows = n_seq * n_t
    rc_rows = (RNN_CONV_W - 1) * n_seq
    sc_rows = (SC_CONV_W - 1) * n_seq
    row_spec = pl.BlockSpec((rows, D_MODEL), lambda i: (i, 0))
    state_layer = layer if h0.shape[0] > 1 else 0
    state_spec = lambda r: _resident((None, r, D_MODEL), lambda i: (state_layer, 0, 0))
    scratch = [pltpu.VMEM((rows, D_MODEL), BF16),
               pltpu.VMEM((rc_rows + rows, D_MODEL), F32),
               pltpu.VMEM((sc_rows + rows, D_MODEL), F32),
               pltpu.VMEM((rows, D_MODEL), F32),
               pltpu.VMEM((rows, D_MODEL), F32),
               pltpu.VMEM((rows, D_MODEL), BF16),
               pltpu.VMEM((rows, D_MODEL), BF16),
               pltpu.VMEM((rows, D_MODEL), F32),
               pltpu.VMEM((rows, D_MODEL), F32)]
    w_specs, w_arrays = _weight_operands(_MIXER_WEIGHTS, small, big, layer)
    zero, smem_spec = jnp.zeros((1,), jnp.int32), pl.BlockSpec(memory_space=pltpu.SMEM)
    if meta is None:
        body, x_args, n_rows = _mixer_kernel, (x, zero), x.shape[0]
        x_specs = [row_spec, smem_spec]
        for name in _MIXER_BIG_WEIGHTS:
            w_specs[_MIXER_WEIGHTS.index(name)] = pl.BlockSpec(memory_space=pl.ANY)
            scratch.append(pltpu.VMEM(big[name].shape, BF16))
        scratch.append(pltpu.SemaphoreType.DMA((3,)))
    else:
        assert n_t % SUBLANES == 0 and n_t > N_META and n_seq == SUBLANES
        body, x_args, n_rows = _mixer_from_seq_major_kernel, (x, meta, zero), (x.shape[1] + N_META) * n_seq
        x_specs = [pl.BlockSpec(memory_space=pl.ANY), _resident(meta.shape, lambda i: (0, 0)), smem_spec]
        scratch += [pltpu.VMEM((2, n_t, n_seq, D_MODEL), F32), pltpu.SemaphoreType.DMA((2,))]
    assert n_rows % rows == 0
    cast_in, cast_out, cast_shapes, cast_arrays = [], [], [], []
    if meta is not None:
        cast_in, cast_out, cast_shapes, cast_arrays = _cast_operands(_FFN_BIG_WEIGHTS, big_f32, layer, n_rows // rows)
        for name in _MIXER_BIG_WEIGHTS:
            i, shape = _MIXER_WEIGHTS.index(name), big_f32[name].shape[1:]
            w_specs[i], w_arrays[i] = pl.BlockSpec(memory_space=pl.ANY), big_f32[name]
            cast_out.append(pl.BlockSpec(memory_space=pl.ANY))
            cast_shapes.append(jax.ShapeDtypeStruct(shape, BF16))
            scratch.append(pltpu.VMEM(shape, BF16))
        scratch += [pltpu.VMEM((2, D_MODEL, D_MODEL), F32), pltpu.SemaphoreType.DMA((2,)), pltpu.SemaphoreType.DMA((1,))]
    outs = pl.pallas_call(
        functools.partial(body, n_seq=n_seq, n_t=n_t, layer=layer),
        grid=(n_rows // rows,),
        in_specs=x_specs + [state_spec(n_seq), state_spec(rc_rows), state_spec(sc_rows)] + w_specs + cast_in,
        out_specs=[row_spec,
                   pl.BlockSpec((n_seq, D_MODEL), lambda i: (0, 0)),
                   pl.BlockSpec((rc_rows, D_MODEL), lambda i: (0, 0)),
                   pl.BlockSpec((sc_rows, D_MODEL), lambda i: (0, 0))] + cast_out,
        out_shape=[jax.ShapeDtypeStruct((n_rows, D_MODEL), F32),
                   jax.ShapeDtypeStruct((n_seq, D_MODEL), F32),
                   jax.ShapeDtypeStruct((rc_rows, D_MODEL), F32),
                   jax.ShapeDtypeStruct((sc_rows, D_MODEL), F32)] + cast_shapes,
        scratch_shapes=scratch,
        compiler_params=_COMPILER_PARAMS,
        name=f"mixer_s{n_seq}" + ("_in" if meta is not None else ""),
    )(*x_args, h0, rc0, sc0, *w_arrays, *cast_arrays)
    if meta is None:
        return outs
    n_ffn = len(_FFN_BIG_WEIGHTS)
    return (*outs[:4], dict(zip(_FFN_BIG_WEIGHTS, outs[4:4 + n_ffn])), dict(zip(_MIXER_BIG_WEIGHTS, outs[4 + n_ffn:])))


def _ffn_call(xp, xs, small, big, layer, *, rows, final_norm, seq_major_out=None, next_f32=None):
    n_rows = xp.shape[0]
    assert n_rows % rows == 0
    n_prompt_steps = n_rows // rows
    prompt_spec = pl.BlockSpec((rows, D_MODEL), lambda i: (jnp.minimum(i, n_prompt_steps - 1), 0))
    sample_spec = pl.BlockSpec(xs.shape, lambda i: (0, 0))
    gf = small["final_norm_g"]
    w_specs, w_arrays = _weight_operands(_FFN_WEIGHTS, small, big, layer)
    in_specs = [prompt_spec, _resident(xs.shape, lambda i: (0, 0))] + w_specs + [_resident(gf.shape, lambda i: (0, 0))]
    args = (xp, xs, *w_arrays, gf)
    sample_shape = jax.ShapeDtypeStruct(xs.shape, F32)
    if seq_major_out is not None:
        n_seq, n_t = seq_major_out
        assert final_norm and next_f32 is None and rows == n_seq * n_t and n_t % SUBLANES == 0 and n_t > N_META
        return pl.pallas_call(
            functools.partial(_ffn_to_seq_major_kernel, n_seq=n_seq, n_t=n_t, layer=layer,
                              n_prompt_steps=n_prompt_steps),
            grid=(n_prompt_steps + 1,), in_specs=in_specs,
            out_specs=[pl.BlockSpec(memory_space=pl.ANY), sample_spec],
            out_shape=[jax.ShapeDtypeStruct((n_seq, n_rows // n_seq - N_META, D_MODEL), F32), sample_shape],
            scratch_shapes=[pltpu.VMEM((2, n_t, n_seq, D_MODEL), F32), pltpu.SemaphoreType.DMA((2,))],
            compiler_params=_COMPILER_PARAMS, name=f"ffn_r{rows}_out",
        )(*args)
    out_specs, out_shape = [prompt_spec, sample_spec], [jax.ShapeDtypeStruct((n_rows, D_MODEL), F32), sample_shape]
    if next_f32 is not None:
        cast_in, cast_out, cast_shapes, cast_arrays = _cast_operands(_BIG_WEIGHTS, next_f32, layer + 1,
                                                                     n_prompt_steps)
        in_specs, out_specs, out_shape = in_specs + cast_in, out_specs + cast_out, out_shape + cast_shapes
        args += tuple(cast_arrays)
    outs = pl.pallas_call(
        functools.partial(_ffn_kernel, final_norm=final_norm, layer=layer, n_prompt_steps=n_prompt_steps),
        grid=(n_prompt_steps + 1,), in_specs=in_specs, out_specs=out_specs, out_shape=out_shape,
        compiler_params=_COMPILER_PARAMS,
        name=f"ffn_r{rows}" + ("_cast" if next_f32 is not None else ""),
    )(*args)
    if next_f32 is None:
        return outs
    return outs[0], outs[1], dict(zip(_BIG_WEIGHTS, outs[2:]))


def _block_diag_gates(gate_a_w, gate_x_w):
    hpc = GATE_CHUNK // RNN_HEAD_DIM

    def bd(wt):
        wt = wt.reshape(DEPTH, N_GATE_CHUNKS, hpc, RNN_HEAD_DIM, RNN_HEAD_DIM)
        bands = [jnp.pad(wt[:, :, p], ((0, 0), (0, 0), (0, 0), (p * RNN_HEAD_DIM, (hpc - 1 - p) * RNN_HEAD_DIM)))
                 for p in range(hpc)]
        return jnp.stack(bands, axis=2).reshape(DEPTH, N_GATE_CHUNKS, GATE_CHUNK, GATE_CHUNK)

    return jnp.concatenate([bd(gate_a_w), bd(gate_x_w)], axis=-1).astype(BF16)


def _time_major(states):
    return jnp.swapaxes(states, 1, 2).reshape(states.shape[0], -1, D_MODEL)


def _seq_major(states, n_seq):
    return jnp.swapaxes(jnp.stack(states).reshape(len(states), -1, n_seq, D_MODEL), 1, 2)


PROMPT_N_T = 86
PROMPT_FFN_ROWS = 688
PROMPT_IO_N_T = 48


def kernel(x_prompt, x_sample, state_rnn_h, state_rnn_conv, state_sc_conv, meta_tokens, norm1_g, w_in, rnn_conv_w, rnn_conv_b, gate_a_w, gate_a_b, gate_x_w, gate_x_b, lru_lambda, w_branch_a, sc_conv_w, w_branch_b, w_out, norm2_g, w_ff_gate, w_ff_up, w_ff_down, final_norm_g):
    small = {
        "norm1_g": norm1_g, "rnn_conv_w": rnn_conv_w, "rnn_conv_b": rnn_conv_b,
        "w_gate": _block_diag_gates(0.5 * gate_a_w, 0.5 * gate_x_w),
        "gate_a_b": gate_a_b, "gate_x_b": gate_x_b, "lru_lambda": lru_lambda,
        "sc_conv_w": sc_conv_w, "norm2_g": norm2_g, "final_norm_g": final_norm_g.reshape(1, D_MODEL),
    }
    big_f32 = {"w_in": w_in, "w_branch_a": w_branch_a, "w_branch_b": w_branch_b, "w_out": w_out,
               "w_ff_gate": w_ff_gate, "w_ff_up": w_ff_up, "w_ff_down": w_ff_down}
    big = [{}]
    dt = x_prompt.dtype

    bp = x_prompt.shape[0]
    bs, t_s, _ = x_sample.shape
    zeros = lambda k: jnp.zeros((1, k * bp, D_MODEL), dt)
    prompt_states = (zeros(1), zeros(RNN_CONV_W - 1), zeros(SC_CONV_W - 1))
    sample_states = (state_rnn_h, _time_major(state_rnn_conv), _time_major(state_sc_conv))
    xp = x_prompt
    xs = jnp.swapaxes(x_sample, 0, 1).reshape(t_s * bs, D_MODEL)
    new_p, new_s = [], []
    for layer in range(DEPTH):
        if layer == 0:
            xp, *st_p, ffn_big, mixer_big = _mixer_call(xp, *prompt_states, small, big[layer], layer, n_seq=bp,
                                                        n_t=PROMPT_IO_N_T, meta=meta_tokens.astype(dt), big_f32=big_f32)
            big[layer] = {**mixer_big, **ffn_big}
        else:
            xp, *st_p = _mixer_call(xp, *prompt_states, small, big[layer], layer, n_seq=bp, n_t=PROMPT_N_T)
        xs, *st_s = _mixer_call(xs, *sample_states, small, big[layer], layer, n_seq=bs, n_t=t_s)
        new_p.append(st_p)
        new_s.append(st_s)
        if layer == DEPTH - 1:
            y_prompt, ys = _ffn_call(xp, xs, small, big[layer], layer, rows=bp * PROMPT_IO_N_T, final_norm=True,
                                     seq_major_out=(bp, PROMPT_IO_N_T))
        else:
            xp, xs, next_big = _ffn_call(xp, xs, small, big[layer], layer, rows=PROMPT_FFN_ROWS, final_norm=False,
                                         next_f32=big_f32)
            big.append(next_big)
    y_sample = jnp.swapaxes(ys.reshape(t_s, bs, D_MODEL), 0, 1)

    def collect(states, n_seq):
        hs, rcs, scs = zip(*states)
        return jnp.stack(hs), _seq_major(list(rcs), n_seq), _seq_major(list(scs), n_seq)

    rnn_h_p, rnn_conv_p, sc_conv_p = collect(new_p, bp)
    rnn_h_s, rnn_conv_s, sc_conv_s = collect(new_s, bs)

    return (y_prompt, y_sample, rnn_h_p, rnn_conv_p, sc_conv_p, rnn_h_s, rnn_conv_s, sc_conv_s)
```

```python
import functools
import math

import jax
import jax.numpy as jnp
from jax import lax
from jax.experimental import pallas as pl
from jax.experimental.pallas import tpu as pltpu

D_MODEL = 1024
DEPTH = 4
N_META = 16
N_RNN_HEADS = 16
RNN_HEAD_DIM = D_MODEL // N_RNN_HEADS
RNN_CONV_W = 4
SC_CONV_W = 3
LRU_C = 8.0
D_FF = 2816
EPS = 1e-6
N_SPLITS = 7
GATE_CHUNK = 256
N_GATE_CHUNKS = D_MODEL // GATE_CHUNK
SUBLANES = 8

V7X_VMEM_BYTES = 64 * 1024 * 1024
VMEM_LIMIT_BYTES = V7X_VMEM_BYTES - 2 * 1024 * 1024

F32 = jnp.float32
BF16 = jnp.bfloat16


def _sigmoid_of_half(half_x):
    return 0.5 * jnp.tanh(half_x) + 0.5


def _gelu_tanh(x):
    c = math.sqrt(2.0 / math.pi)
    half_x = 0.5 * x
    return half_x * jnp.tanh(x * (c + (c * 0.044715) * (x * x))) + half_x


def _softplus(x):
    return jnp.maximum(x, 0.0) + jnp.log1p(jnp.exp(-jnp.abs(x)))


def _rmsnorm(x, g):
    ms = jnp.mean(x * x, axis=-1, keepdims=True)
    return x * lax.rsqrt(ms + EPS) * g


def _dot(a, b):
    return jnp.dot(a, b, preferred_element_type=F32)


def _mixer_body(read_x, step, region_gate, wait_weights, h0_ref, rc0_ref, sc0_ref, g1_ref, win_ref, cw_ref, cb_ref, wgate_ref,
                ba_ref, bx_ref, lam_ref, wba_ref, scw_ref, wbb_ref, wout_ref,
                xo_ref, h_ref, rc_ref, sc_ref,
                u_bf, xrbuf, chbuf, a_buf, h_buf, ya_buf, yb_buf, ga_buf, gb_buf, *, n_seq, n_t, layer):
    rows = n_seq * n_t
    rc_rows = (RNN_CONV_W - 1) * n_seq
    sc_rows = (SC_CONV_W - 1) * n_seq
    row = slice(layer, layer + 1)

    def proj(k):
        return _dot(u_bf[...], win_ref[:, k * D_MODEL:(k + 1) * D_MODEL])

    wait_weights(1)

    @pl.when(step == 0)
    def _():
        h_ref[...] = h0_ref[...]
        xrbuf[0:rc_rows, :] = rc0_ref[...]
        chbuf[0:sc_rows, :] = sc0_ref[...]

    u_bf[...] = _rmsnorm(read_x(), g1_ref[row, :]).astype(BF16)
    xrbuf[rc_rows:rc_rows + rows, :] = proj(0)
    xc = cb_ref[row, :] + xrbuf[0:rows, :] * cw_ref[0:1, :]
    for k in range(1, RNN_CONV_W):
        xc = xc + xrbuf[k * n_seq:k * n_seq + rows, :] * cw_ref[k:k + 1, :]
    h_buf[...] = xc
    xc_b = xc.astype(BF16)
    for j in range(N_GATE_CHUNKS):
        cols = slice(j * GATE_CHUNK, (j + 1) * GATE_CHUNK)
        gates = _dot(xc_b[:, cols], wgate_ref[j])
        ga_buf[:, cols] = gates[:, :GATE_CHUNK]
        gb_buf[:, cols] = gates[:, GATE_CHUNK:]
    chbuf[sc_rows:sc_rows + rows, :] = proj(3) * proj(4)

    @pl.when(region_gate)
    def _():
        wait_weights(2)
        half_log_a_max = (-0.5 * LRU_C) * _softplus(-lam_ref[row, :])
        t_r = jnp.tanh(ga_buf[...] + 0.5 * ba_ref[row, :])
        log_a = t_r * half_log_a_max + half_log_a_max
        a = jnp.exp(log_a)
        one_minus_a2 = jnp.maximum(jnp.tanh(log_a) * (-1.0 - a * a), 1e-12)
        mult = one_minus_a2 * lax.rsqrt(one_minus_a2)
        gate_i = _sigmoid_of_half(gb_buf[...] + 0.5 * bx_ref[row, :])
        a_buf[...] = a
        h_buf[...] = mult * gate_i * h_buf[...]

        if n_seq == SUBLANES:
            def scan_step(t, h):
                sl = pl.ds(pl.multiple_of(t * n_seq, n_seq), n_seq)
                h = a_buf[sl, :] * h + h_buf[sl, :]
                h_buf[sl, :] = h
                return h
            h_ref[...] = lax.fori_loop(0, n_t, scan_step, h_ref[...], unroll=True)
        else:
            for t in range(n_t):
                sl = slice(t * n_seq, (t + 1) * n_seq)
                prev = h_ref[...] if t == 0 else h_buf[(t - 1) * n_seq:t * n_seq, :]
                h_buf[sl, :] = a_buf[sl, :] * prev + h_buf[sl, :]
            h_ref[...] = h_buf[(n_t - 1) * n_seq:n_t * n_seq, :]

        ya_buf[...] = (h_buf[...] * _gelu_tanh(proj(1))).astype(BF16)

        vc = chbuf[0:rows, :] * scw_ref[0:1, :]
        for k in range(1, SC_CONV_W):
            vc = vc + chbuf[k * n_seq:k * n_seq + rows, :] * scw_ref[k:k + 1, :]
        yb_buf[...] = (proj(2) * vc).astype(BF16)

        ga_buf[...] = _sigmoid_of_half(proj(5))
        gb_buf[...] = _sigmoid_of_half(proj(6))

        new_rc = xrbuf[rows:rows + rc_rows, :]
        new_sc = chbuf[rows:rows + sc_rows, :]
        xrbuf[0:rc_rows, :] = new_rc
        chbuf[0:sc_rows, :] = new_sc

        @pl.when(step == pl.num_programs(0) - 1)
        def _():
            rc_ref[...] = new_rc
            sc_ref[...] = new_sc

    @pl.when(region_gate)
    def _():
        wait_weights(3)
        m = ga_buf[...] * _dot(ya_buf[...], wba_ref[...])
        m = m + gb_buf[...] * _dot(yb_buf[...], wbb_ref[...])
        xo_ref[...] = read_x() + _dot(m.astype(BF16), wout_ref[...])


_REGION_W_IN_GROUPS = {1: (0, 3, 4), 2: (1, 2, 5, 6)}


def _mixer_kernel(x_ref, zero_ref, *refs, n_seq, n_t, layer):
    refs = list(refs)
    n_w, n_big = len(_MIXER_WEIGHTS), len(_MIXER_BIG_WEIGHTS)
    *refs, sem = refs
    big_vmem = dict(zip(_MIXER_BIG_WEIGHTS, refs[-n_big:]))
    refs = refs[:-n_big]
    states, weights, rest = refs[:3], refs[3:3 + n_w], refs[3 + n_w:]
    big_hbm = {name: ref for name, ref in zip(_MIXER_WEIGHTS, weights) if name in big_vmem}
    weights = [big_vmem.get(name, ref) for name, ref in zip(_MIXER_WEIGHTS, weights)]
    step = pl.program_id(0)

    def region_copies(region):
        if region in _REGION_W_IN_GROUPS:
            pairs = [(big_hbm["w_in"].at[:, pl.ds(k * D_MODEL, D_MODEL)], big_vmem["w_in"].at[:, pl.ds(k * D_MODEL, D_MODEL)])
                     for k in _REGION_W_IN_GROUPS[region]]
        else:
            pairs = [(big_hbm[name], big_vmem[name]) for name in _MIXER_BIG_WEIGHTS[1:]]
        return [pltpu.make_async_copy(src, dst, sem.at[region - 1]) for src, dst in pairs]

    @pl.when(step == 0)
    def _():
        for region in (1, 2, 3):
            for c in region_copies(region):
                c.start()

    def wait_weights(region):
        @pl.when(step == 0)
        def _():
            for c in region_copies(region):
                c.wait()

    _mixer_body(lambda: x_ref[...], step, step + zero_ref[0] >= 0, wait_weights, *states, *weights, *rest,
                n_seq=n_seq, n_t=n_t, layer=layer)


def _seq_major_tile_copies(seq_hbm, tile_buf, sem, tile, slot, *, n_seq, n_t):
    copies = []
    for b in range(n_seq):
        if isinstance(tile, int):
            assert tile == 0
            hbm = seq_hbm.at[b, pl.ds(0, n_t - N_META), :]
            vmem = tile_buf.at[slot, pl.ds(N_META, n_t - N_META), b, :]
        else:
            t0 = pl.multiple_of(tile * n_t - N_META, SUBLANES)
            hbm = seq_hbm.at[b, pl.ds(t0, n_t), :]
            vmem = tile_buf.at[slot, :, b, :]
        copies.append(pltpu.make_async_copy(hbm, vmem, sem.at[slot]))
    return copies


def _load_and_cast_mixer_weights(f32_hbm, bf16_vmem, stage, sem, layer):
    chunks = []
    for k in range(N_SPLITS):
        cols = pl.ds(k * D_MODEL, D_MODEL)
        chunks.append((f32_hbm["w_in"].at[layer, :, cols], bf16_vmem["w_in"].at[:, cols], k >= N_SPLITS - 2))
    for name in _MIXER_BIG_WEIGHTS[1:]:
        chunks.append((f32_hbm[name].at[layer], bf16_vmem[name], False))
    copy = lambda c: pltpu.make_async_copy(chunks[c][0], stage.at[c % 2], sem.at[c % 2])
    copy(0).start()
    copy(1).start()
    for c, (_, dst, halve) in enumerate(chunks):
        copy(c).wait()
        value = stage[c % 2]
        dst[...] = (0.5 * value if halve else value).astype(BF16)
        if c + 2 < len(chunks):
            copy(c + 2).start()


def _mixer_from_seq_major_kernel(xs_hbm, meta_ref, zero_ref, *refs, n_seq, n_t, layer):
    refs = list(refs)
    n_w, n_out, n_cast, n_big = len(_MIXER_WEIGHTS), 4, len(_FFN_BIG_WEIGHTS), len(_MIXER_BIG_WEIGHTS)
    states, weights, cast_src = refs[:3], refs[3:3 + n_w], refs[3 + n_w:3 + n_w + n_cast]
    o = 3 + n_w + n_cast
    outs, cast_dst, big_out = refs[o:o + n_out], refs[o + n_out:o + n_out + n_cast], refs[o + n_out + n_cast:o + n_out + n_cast + n_big]
    sc = o + n_out + n_cast + n_big
    n_body_scratch = len(refs) - sc - 2 - n_big - 3
    body_scratch, (xin, sem) = refs[sc:sc + n_body_scratch], refs[sc + n_body_scratch:sc + n_body_scratch + 2]
    big_vmem = dict(zip(_MIXER_BIG_WEIGHTS, refs[sc + n_body_scratch + 2:sc + n_body_scratch + 2 + n_big]))
    stage, load_sem, store_sem = refs[sc + n_body_scratch + 2 + n_big:]
    big_f32 = {name: ref for name, ref in zip(_MIXER_WEIGHTS, weights) if name in big_vmem}
    weights = [big_vmem.get(name, ref) for name, ref in zip(_MIXER_WEIGHTS, weights)]

    step = pl.program_id(0)
    n_steps = pl.num_programs(0)
    slot = step % 2
    copies = functools.partial(_seq_major_tile_copies, xs_hbm, xin, sem, n_seq=n_seq, n_t=n_t)
    write_back = [pltpu.make_async_copy(big_vmem[name], dst, store_sem.at[0])
                  for name, dst in zip(_MIXER_BIG_WEIGHTS, big_out)]

    @pl.when(step == 0)
    def _():
        for c in copies(0, 0):
            c.start()
        xin[0, 0:N_META, :, :] = jnp.broadcast_to(meta_ref[...][:, None, :], (N_META, n_seq, D_MODEL))

        @pl.when(n_steps > 1)
        def _():
            for c in copies(step + 1, 1):
                c.start()
        _load_and_cast_mixer_weights(big_f32, big_vmem, stage, load_sem, layer)
        for c in write_back:
            c.start()
        for c in copies(0, 0):
            c.wait()

    @pl.when(step > 0)
    def _():
        @pl.when(step + 1 < n_steps)
        def _():
            for c in copies(step + 1, 1 - slot):
                c.start()
        for c in copies(step, slot):
            c.wait()

    _mixer_body(lambda: xin[slot].reshape(n_t * n_seq, D_MODEL), step, step + zero_ref[0] >= 0, lambda region: None,
                *states, *weights, *outs, *body_scratch, n_seq=n_seq, n_t=n_t, layer=layer)
    _cast_weight_blocks(_FFN_BIG_WEIGHTS, cast_src, cast_dst)

    @pl.when(step == n_steps - 1)
    def _():
        for c in write_back:
            c.wait()


def _ffn_body(x, *, g2_ref, wg_ref, wu_ref, wd_ref, gf_ref, final_norm, layer):
    v = _rmsnorm(x, g2_ref[layer:layer + 1, :]).astype(BF16)
    half = _dot(v, wg_ref[...])
    hid = ((half * jnp.tanh(half) + half) * _dot(v, wu_ref[...])).astype(BF16)
    y = x + _dot(hid, wd_ref[...])
    if final_norm:
        y = _rmsnorm(y, gf_ref[...])
    return y


def _cast_weight_blocks(names, src_refs, dst_refs):
    @pl.when(pl.program_id(0) < N_CONVERT_STEPS)
    def _():
        for name, src, dst in zip(names, src_refs, dst_refs):
            if name == "w_in":
                plain = (N_SPLITS - 2) * D_MODEL
                dst[:, :plain] = src[:, :plain].astype(BF16)
                dst[:, plain:] = (0.5 * src[:, plain:]).astype(BF16)
            elif name == "w_ff_gate":
                dst[...] = (0.5 * src[...]).astype(BF16)
            else:
                dst[...] = src[...].astype(BF16)


def _ffn_kernel(xp_ref, xs_ref, g2_ref, wg_ref, wu_ref, wd_ref, gf_ref, *rest, final_norm, layer, n_prompt_steps):
    n_big = (len(rest) - 2) // 2
    next_f32, (op_ref, os_ref), next_bf16 = rest[:n_big], rest[n_big:n_big + 2], rest[n_big + 2:]
    body = functools.partial(_ffn_body, g2_ref=g2_ref, wg_ref=wg_ref, wu_ref=wu_ref, wd_ref=wd_ref, gf_ref=gf_ref,
                             final_norm=final_norm, layer=layer)
    step = pl.program_id(0)

    @pl.when(step < n_prompt_steps)
    def _():
        op_ref[...] = body(xp_ref[...])

    @pl.when(step == n_prompt_steps)
    def _():
        os_ref[...] = body(xs_ref[...])

    if n_big:
        _cast_weight_blocks(_BIG_WEIGHTS, next_f32, next_bf16)


def _ffn_to_seq_major_kernel(xp_hbm, xs_ref, g2_ref, wg_ref, wu_ref, wd_ref, gf_ref, yp_ref, os_ref, xin, sem,
                             *, n_t, layer, n_prompt_steps):
    step = pl.program_id(0)
    slot = step % 2
    tiles_per_seq = (xp_hbm.shape[0] - N_META) // n_t
    body = functools.partial(_ffn_body, g2_ref=g2_ref, wg_ref=wg_ref, wu_ref=wu_ref, wd_ref=wd_ref, gf_ref=gf_ref,
                             final_norm=True, layer=layer)

    def fetch(tile, into):
        t0 = N_META + (tile % tiles_per_seq) * n_t
        return pltpu.make_async_copy(xp_hbm.at[pl.ds(t0, n_t), tile // tiles_per_seq, :], xin.at[into], sem.at[into])

    @pl.when(step == 0)
    def _():
        fetch(0, 0).start()

    @pl.when(step < n_prompt_steps)
    def _():
        @pl.when(step + 1 < n_prompt_steps)
        def _():
            fetch(step + 1, 1 - slot).start()
        fetch(step, slot).wait()
        yp_ref[...] = body(xin[slot])

    @pl.when(step == n_prompt_steps)
    def _():
        os_ref[...] = body(xs_ref[...])


def _resident(shape, index):
    return pl.BlockSpec(shape, index, pipeline_mode=pl.Buffered(1))


def _weight_operands(names, small, big, layer):
    specs, arrays = [], []
    for name in names:
        if name not in big and name not in small:
            specs.append(None)
            arrays.append(None)
            continue
        arr = big[name] if name in big else small[name]
        if name in big or arr.ndim == 2:
            specs.append(_resident(arr.shape, lambda i, nd=arr.ndim: (0,) * nd))
        else:
            specs.append(_resident((None,) + arr.shape[1:], lambda i, nd=arr.ndim: (layer,) + (0,) * (nd - 1)))
        arrays.append(arr)
    return specs, arrays


_MIXER_WEIGHTS = ("norm1_g", "w_in", "rnn_conv_w", "rnn_conv_b", "w_gate", "gate_a_b", "gate_x_b",
                  "lru_lambda", "w_branch_a", "sc_conv_w", "w_branch_b", "w_out")
_FFN_WEIGHTS = ("norm2_g", "w_ff_gate", "w_ff_up", "w_ff_down")
_BIG_WEIGHTS = ("w_in", "w_branch_a", "w_branch_b", "w_out", "w_ff_gate", "w_ff_up", "w_ff_down")
_MIXER_BIG_WEIGHTS, _FFN_BIG_WEIGHTS = _BIG_WEIGHTS[:4], _BIG_WEIGHTS[4:]
N_CONVERT_STEPS = 16
_COMPILER_PARAMS = pltpu.CompilerParams(dimension_semantics=("arbitrary",), vmem_limit_bytes=VMEM_LIMIT_BYTES)


def _cast_operands(names, stacked_f32, layer, n_steps):
    assert n_steps >= N_CONVERT_STEPS
    block = lambda i: jnp.minimum(i, N_CONVERT_STEPS - 1)
    in_specs, out_specs, out_shapes = [], [], []
    for name in names:
        _, n_in, n_out = stacked_f32[name].shape
        blk = n_in // N_CONVERT_STEPS
        assert blk * N_CONVERT_STEPS == n_in and blk % (2 * SUBLANES) == 0
        in_specs.append(pl.BlockSpec((None, blk, n_out), lambda i: (layer, block(i), 0)))
        out_specs.append(pl.BlockSpec((blk, n_out), lambda i: (block(i), 0)))
        out_shapes.append(jax.ShapeDtypeStruct((n_in, n_out), BF16))
    return in_specs, out_specs, out_shapes, [stacked_f32[name] for name in names]


def _mixer_call(x, h0, rc0, sc0, small, big, layer, *, n_seq, n_t, meta=None, big_f32=None):
    rows = n_seq * n_t
    rc_rows = (RNN_CONV_W - 1) * n_seq
    sc_rows = (SC_CONV_W - 1) * n_seq
    row_spec = pl.BlockSpec((rows, D_MODEL), lambda i: (i, 0))
    state_layer = layer if h0.shape[0] > 1 else 0
    state_spec = lambda r: _resident((None, r, D_MODEL), lambda i: (state_layer, 0, 0))
    scratch = [pltpu.VMEM((rows, D_MODEL), BF16),
               pltpu.VMEM((rc_rows + rows, D_MODEL), F32),
               pltpu.VMEM((sc_rows + rows, D_MODEL), F32),
               pltpu.VMEM((rows, D_MODEL), F32),
               pltpu.VMEM((rows, D_MODEL), F32),
               pltpu.VMEM((rows, D_MODEL), BF16),
               pltpu.VMEM((rows, D_MODEL), BF16),
               pltpu.VMEM((rows, D_MODEL), F32),
               pltpu.VMEM((rows, D_MODEL), F32)]
    w_specs, w_arrays = _weight_operands(_MIXER_WEIGHTS, small, big, layer)
    zero, smem_spec = jnp.zeros((1,), jnp.int32), pl.BlockSpec(memory_space=pltpu.SMEM)
    if meta is None:
        body, x_args, n_rows = _mixer_kernel, (x, zero), x.shape[0]
        x_specs = [row_spec, smem_spec]
        for name in _MIXER_BIG_WEIGHTS:
            w_specs[_MIXER_WEIGHTS.index(name)] = pl.BlockSpec(memory_space=pl.ANY)
            scratch.append(pltpu.VMEM(big[name].shape, BF16))
        scratch.append(pltpu.SemaphoreType.DMA((3,)))
    else:
        assert n_t % SUBLANES == 0 and n_t > N_META and n_seq == SUBLANES
        body, x_args, n_rows = _mixer_from_seq_major_kernel, (x, meta, zero), (x.shape[1] + N_META) * n_seq
        x_specs = [pl.BlockSpec(memory_space=pl.ANY), _resident(meta.shape, lambda i: (0, 0)), smem_spec]
        scratch += [pltpu.VMEM((2, n_t, n_seq, D_MODEL), F32), pltpu.SemaphoreType.DMA((2,))]
    assert n_rows % rows == 0
    cast_in, cast_out, cast_shapes, cast_arrays = [], [], [], []
    if meta is not None:
        cast_in, cast_out, cast_shapes, cast_arrays = _cast_operands(_FFN_BIG_WEIGHTS, big_f32, layer, n_rows // rows)
        for name in _MIXER_BIG_WEIGHTS:
            i, shape = _MIXER_WEIGHTS.index(name), big_f32[name].shape[1:]
            w_specs[i], w_arrays[i] = pl.BlockSpec(memory_space=pl.ANY), big_f32[name]
            cast_out.append(pl.BlockSpec(memory_space=pl.ANY))
            cast_shapes.append(jax.ShapeDtypeStruct(shape, BF16))
            scratch.append(pltpu.VMEM(shape, BF16))
        scratch += [pltpu.VMEM((2, D_MODEL, D_MODEL), F32), pltpu.SemaphoreType.DMA((2,)), pltpu.SemaphoreType.DMA((1,))]
    outs = pl.pallas_call(
        functools.partial(body, n_seq=n_seq, n_t=n_t, layer=layer),
        grid=(n_rows // rows,),
        in_specs=x_specs + [state_spec(n_seq), state_spec(rc_rows), state_spec(sc_rows)] + w_specs + cast_in,
        out_specs=[row_spec,
                   pl.BlockSpec((n_seq, D_MODEL), lambda i: (0, 0)),
                   pl.BlockSpec((rc_rows, D_MODEL), lambda i: (0, 0)),
                   pl.BlockSpec((sc_rows, D_MODEL), lambda i: (0, 0))] + cast_out,
        out_shape=[jax.ShapeDtypeStruct((n_rows, D_MODEL), F32),
                   jax.ShapeDtypeStruct((n_seq, D_MODEL), F32),
                   jax.ShapeDtypeStruct((rc_rows, D_MODEL), F32),
                   jax.ShapeDtypeStruct((sc_rows, D_MODEL), F32)] + cast_shapes,
        scratch_shapes=scratch,
        compiler_params=_COMPILER_PARAMS,
        name=f"mixer_s{n_seq}" + ("_in" if meta is not None else ""),
    )(*x_args, h0, rc0, sc0, *w_arrays, *cast_arrays)
    if meta is None:
        return outs
    n_ffn = len(_FFN_BIG_WEIGHTS)
    return (*outs[:4], dict(zip(_FFN_BIG_WEIGHTS, outs[4:4 + n_ffn])), dict(zip(_MIXER_BIG_WEIGHTS, outs[4 + n_ffn:])))


def _ffn_call(xp, xs, small, big, layer, *, final_norm, rows=None, seq_major_out=None, next_f32=None):
    n_rows = xp.shape[0]
    sample_spec = pl.BlockSpec(xs.shape, lambda i: (0, 0))
    gf = small["final_norm_g"]
    w_specs, w_arrays = _weight_operands(_FFN_WEIGHTS, small, big, layer)
    in_specs = [None, _resident(xs.shape, lambda i: (0, 0))] + w_specs + [_resident(gf.shape, lambda i: (0, 0))]
    args = (xp, xs, *w_arrays, gf)
    sample_shape = jax.ShapeDtypeStruct(xs.shape, F32)
    if seq_major_out is not None:
        n_seq, n_t = seq_major_out
        n_time = n_rows // n_seq
        assert final_norm and next_f32 is None and (n_time - N_META) % n_t == 0
        tiles_per_seq = (n_time - N_META) // n_t
        n_prompt_steps = n_seq * tiles_per_seq
        last = n_prompt_steps - 1
        y_spec = pl.BlockSpec((None, n_t, D_MODEL), lambda i: (jnp.minimum(i, last) // tiles_per_seq,
                                                               jnp.minimum(i, last) % tiles_per_seq, 0))
        in_specs[0] = pl.BlockSpec(memory_space=pl.ANY)
        return pl.pallas_call(
            functools.partial(_ffn_to_seq_major_kernel, n_t=n_t, layer=layer, n_prompt_steps=n_prompt_steps),
            grid=(n_prompt_steps + 1,), in_specs=in_specs, out_specs=[y_spec, sample_spec],
            out_shape=[jax.ShapeDtypeStruct((n_seq, n_time - N_META, D_MODEL), F32), sample_shape],
            scratch_shapes=[pltpu.VMEM((2, n_t, D_MODEL), F32), pltpu.SemaphoreType.DMA((2,))],
            compiler_params=_COMPILER_PARAMS, name=f"ffn_t{n_t}_out",
        )(xp.reshape(n_time, n_seq, D_MODEL), *args[1:])
    assert n_rows % rows == 0
    n_prompt_steps = n_rows // rows
    prompt_spec = pl.BlockSpec((rows, D_MODEL), lambda i: (jnp.minimum(i, n_prompt_steps - 1), 0))
    in_specs[0] = prompt_spec
    out_specs, out_shape = [prompt_spec, sample_spec], [jax.ShapeDtypeStruct((n_rows, D_MODEL), F32), sample_shape]
    if next_f32 is not None:
        cast_in, cast_out, cast_shapes, cast_arrays = _cast_operands(_BIG_WEIGHTS, next_f32, layer + 1,
                                                                     n_prompt_steps)
        in_specs, out_specs, out_shape = in_specs + cast_in, out_specs + cast_out, out_shape + cast_shapes
        args += tuple(cast_arrays)
    outs = pl.pallas_call(
        functools.partial(_ffn_kernel, final_norm=final_norm, layer=layer, n_prompt_steps=n_prompt_steps),
        grid=(n_prompt_steps + 1,), in_specs=in_specs, out_specs=out_specs, out_shape=out_shape,
        compiler_params=_COMPILER_PARAMS,
        name=f"ffn_r{rows}" + ("_cast" if next_f32 is not None else ""),
    )(*args)
    if next_f32 is None:
        return outs
    return outs[0], outs[1], dict(zip(_BIG_WEIGHTS, outs[2:]))


def _block_diag_gates(gate_a_w, gate_x_w):
    hpc = GATE_CHUNK // RNN_HEAD_DIM

    def bd(wt):
        wt = wt.reshape(DEPTH, N_GATE_CHUNKS, hpc, RNN_HEAD_DIM, RNN_HEAD_DIM)
        bands = [jnp.pad(wt[:, :, p], ((0, 0), (0, 0), (0, 0), (p * RNN_HEAD_DIM, (hpc - 1 - p) * RNN_HEAD_DIM)))
                 for p in range(hpc)]
        return jnp.stack(bands, axis=2).reshape(DEPTH, N_GATE_CHUNKS, GATE_CHUNK, GATE_CHUNK)

    return jnp.concatenate([bd(gate_a_w), bd(gate_x_w)], axis=-1).astype(BF16)


def _time_major(states):
    return jnp.swapaxes(states, 1, 2).reshape(states.shape[0], -1, D_MODEL)


def _seq_major(states, n_seq):
    return jnp.swapaxes(jnp.stack(states).reshape(len(states), -1, n_seq, D_MODEL), 1, 2)


PROMPT_N_T = 86
PROMPT_FFN_ROWS = 688
PROMPT_IO_N_T = 48
PROMPT_OUT_N_T = 1024


def kernel(x_prompt, x_sample, state_rnn_h, state_rnn_conv, state_sc_conv, meta_tokens, norm1_g, w_in, rnn_conv_w, rnn_conv_b, gate_a_w, gate_a_b, gate_x_w, gate_x_b, lru_lambda, w_branch_a, sc_conv_w, w_branch_b, w_out, norm2_g, w_ff_gate, w_ff_up, w_ff_down, final_norm_g):
    small = {
        "norm1_g": norm1_g, "rnn_conv_w": rnn_conv_w, "rnn_conv_b": rnn_conv_b,
        "w_gate": _block_diag_gates(0.5 * gate_a_w, 0.5 * gate_x_w),
        "gate_a_b": gate_a_b, "gate_x_b": gate_x_b, "lru_lambda": lru_lambda,
        "sc_conv_w": sc_conv_w, "norm2_g": norm2_g, "final_norm_g": final_norm_g.reshape(1, D_MODEL),
    }
    big_f32 = {"w_in": w_in, "w_branch_a": w_branch_a, "w_branch_b": w_branch_b, "w_out": w_out,
               "w_ff_gate": w_ff_gate, "w_ff_up": w_ff_up, "w_ff_down": w_ff_down}
    big = [{}]
    dt = x_prompt.dtype

    bp = x_prompt.shape[0]
    bs, t_s, _ = x_sample.shape
    zeros = lambda k: jnp.zeros((1, k * bp, D_MODEL), dt)
    prompt_states = (zeros(1), zeros(RNN_CONV_W - 1), zeros(SC_CONV_W - 1))
    sample_states = (state_rnn_h, _time_major(state_rnn_conv), _time_major(state_sc_conv))
    xp = x_prompt
    xs = jnp.swapaxes(x_sample, 0, 1).reshape(t_s * bs, D_MODEL)
    new_p, new_s = [], []
    for layer in range(DEPTH):
        if layer == 0:
            xp, *st_p, ffn_big, mixer_big = _mixer_call(xp, *prompt_states, small, big[layer], layer, n_seq=bp,
                                                        n_t=PROMPT_IO_N_T, meta=meta_tokens.astype(dt), big_f32=big_f32)
            big[layer] = {**mixer_big, **ffn_big}
        else:
            xp, *st_p = _mixer_call(xp, *prompt_states, small, big[layer], layer, n_seq=bp, n_t=PROMPT_N_T)
        xs, *st_s = _mixer_call(xs, *sample_states, small, big[layer], layer, n_seq=bs, n_t=t_s)
        new_p.append(st_p)
        new_s.append(st_s)
        if layer == DEPTH - 1:
            y_prompt, ys = _ffn_call(xp, xs, small, big[layer], layer, final_norm=True,
                                     seq_major_out=(bp, PROMPT_OUT_N_T))
        else:
            xp, xs, next_big = _ffn_call(xp, xs, small, big[layer], layer, rows=PROMPT_FFN_ROWS, final_norm=False,
                                         next_f32=big_f32)
            big.append(next_big)
    y_sample = jnp.swapaxes(ys.reshape(t_s, bs, D_MODEL), 0, 1)

    def collect(states, n_seq):
        hs, rcs, scs = zip(*states)
        return jnp.stack(hs), _seq_major(list(rcs), n_seq), _seq_major(list(scs), n_seq)

    rnn_h_p, rnn_conv_p, sc_conv_p = collect(new_p, bp)
    rnn_h_s, rnn_conv_s, sc_conv_s = collect(new_s, bs)

    return (y_prompt, y_sample, rnn_h_p, rnn_conv_p, sc_conv_p, rnn_h_s, rnn_conv_s, sc_conv_s)
```

```python
import functools
import math

import jax
import jax.numpy as jnp
from jax import lax
from jax.experimental import pallas as pl
from jax.experimental.pallas import tpu as pltpu

D_MODEL = 1024
DEPTH = 4
N_META = 16
N_RNN_HEADS = 16
RNN_HEAD_DIM = D_MODEL // N_RNN_HEADS
RNN_CONV_W = 4
SC_CONV_W = 3
LRU_C = 8.0
EPS = 1e-6
N_SPLITS = 7
GATE_CHUNK = 256
N_GATE_CHUNKS = D_MODEL // GATE_CHUNK
SUBLANES = 8

V7X_VMEM_BYTES = 64 * 1024 * 1024
VMEM_LIMIT_BYTES = V7X_VMEM_BYTES - 2 * 1024 * 1024

F32 = jnp.float32
BF16 = jnp.bfloat16


def _sigmoid_of_half(half_x):
    return 0.5 * jnp.tanh(half_x) + 0.5


def _gelu_tanh(x):
    c = math.sqrt(2.0 / math.pi)
    half_x = 0.5 * x
    return half_x * jnp.tanh(x * (c + (c * 0.044715) * (x * x))) + half_x


def _softplus(x):
    return jnp.maximum(x, 0.0) + jnp.log1p(jnp.exp(-jnp.abs(x)))


def _rmsnorm(x, g):
    ms = jnp.mean(x * x, axis=-1, keepdims=True)
    return x * lax.rsqrt(ms + EPS) * g


def _dot(a, b):
    return jnp.dot(a, b, preferred_element_type=F32)


def _mixer_body(read_x, step, region_gate, wait_weights, h0_ref, rc0_ref, sc0_ref, g1_ref, win_ref, cw_ref, cb_ref, wgate_ref,
                ba_ref, bx_ref, lam_ref, wba_ref, scw_ref, wbb_ref, wout_ref,
                xo_ref, h_ref, rc_ref, sc_ref,
                u_bf, xrbuf, chbuf, a_buf, h_buf, ya_buf, yb_buf, ga_buf, gb_buf, *, n_seq, n_t, layer):
    rows = n_seq * n_t
    rc_rows = (RNN_CONV_W - 1) * n_seq
    sc_rows = (SC_CONV_W - 1) * n_seq
    row = slice(layer, layer + 1)

    def proj(k):
        return _dot(u_bf[...], win_ref[:, k * D_MODEL:(k + 1) * D_MODEL])

    wait_weights(1)

    @pl.when(step == 0)
    def _():
        h_ref[...] = h0_ref[...]
        xrbuf[0:rc_rows, :] = rc0_ref[...]
        chbuf[0:sc_rows, :] = sc0_ref[...]

    u_bf[...] = _rmsnorm(read_x(), g1_ref[row, :]).astype(BF16)
    xrbuf[rc_rows:rc_rows + rows, :] = proj(0)
    chbuf[sc_rows:sc_rows + rows, :] = proj(3) * proj(4)
    xc = cb_ref[row, :] + xrbuf[0:rows, :] * cw_ref[0:1, :]
    for k in range(1, RNN_CONV_W):
        xc = xc + xrbuf[k * n_seq:k * n_seq + rows, :] * cw_ref[k:k + 1, :]
    h_buf[...] = xc
    xc_b = xc.astype(BF16)
    for j in range(N_GATE_CHUNKS):
        cols = slice(j * GATE_CHUNK, (j + 1) * GATE_CHUNK)
        gates = _dot(xc_b[:, cols], wgate_ref[j])
        ga_buf[:, cols] = gates[:, :GATE_CHUNK]
        gb_buf[:, cols] = gates[:, GATE_CHUNK:]

    @pl.when(region_gate)
    def _():
        wait_weights(2)
        half_log_a_max = (-0.5 * LRU_C) * _softplus(-lam_ref[row, :])
        t_r = jnp.tanh(ga_buf[...] + 0.5 * ba_ref[row, :])
        log_a = t_r * half_log_a_max + half_log_a_max
        a = jnp.exp(log_a)
        one_minus_a2 = jnp.maximum(jnp.tanh(log_a) * (-1.0 - a * a), 1e-12)
        mult = one_minus_a2 * lax.rsqrt(one_minus_a2)
        gate_i = _sigmoid_of_half(gb_buf[...] + 0.5 * bx_ref[row, :])
        a_buf[...] = a
        h_buf[...] = mult * gate_i * h_buf[...]

        if n_seq == SUBLANES:
            def scan_step(t, h):
                sl = pl.ds(pl.multiple_of(t * n_seq, n_seq), n_seq)
                h = a_buf[sl, :] * h + h_buf[sl, :]
                h_buf[sl, :] = h
                return h
            h_ref[...] = lax.fori_loop(0, n_t, scan_step, h_ref[...], unroll=True)
        else:
            for t in range(n_t):
                sl = slice(t * n_seq, (t + 1) * n_seq)
                prev = h_ref[...] if t == 0 else h_buf[(t - 1) * n_seq:t * n_seq, :]
                h_buf[sl, :] = a_buf[sl, :] * prev + h_buf[sl, :]
            h_ref[...] = h_buf[(n_t - 1) * n_seq:n_t * n_seq, :]

        ya_buf[...] = (h_buf[...] * _gelu_tanh(proj(1))).astype(BF16)

        vc = chbuf[0:rows, :] * scw_ref[0:1, :]
        for k in range(1, SC_CONV_W):
            vc = vc + chbuf[k * n_seq:k * n_seq + rows, :] * scw_ref[k:k + 1, :]
        yb_buf[...] = (proj(2) * vc).astype(BF16)

        gb_buf[...] = _sigmoid_of_half(proj(6))
        ga_buf[...] = _sigmoid_of_half(proj(5))

        new_rc = xrbuf[rows:rows + rc_rows, :]
        new_sc = chbuf[rows:rows + sc_rows, :]
        xrbuf[0:rc_rows, :] = new_rc
        chbuf[0:sc_rows, :] = new_sc

        @pl.when(step == pl.num_programs(0) - 1)
        def _():
            rc_ref[...] = new_rc
            sc_ref[...] = new_sc

    @pl.when(region_gate)
    def _():
        wait_weights(3)
        m = ga_buf[...] * _dot(ya_buf[...], wba_ref[...])
        m = m + gb_buf[...] * _dot(yb_buf[...], wbb_ref[...])
        xo_ref[...] = read_x() + _dot(m.astype(BF16), wout_ref[...])


_REGION_W_IN_GROUPS = {1: (0, 3, 4), 2: (1, 2, 5, 6)}


def _mixer_kernel(x_ref, zero_ref, *refs, n_seq, n_t, layer):
    refs = list(refs)
    n_w, n_big = len(_MIXER_WEIGHTS), len(_MIXER_BIG_WEIGHTS)
    *refs, sem = refs
    big_vmem = dict(zip(_MIXER_BIG_WEIGHTS, refs[-n_big:]))
    refs = refs[:-n_big]
    states, weights, rest = refs[:3], refs[3:3 + n_w], refs[3 + n_w:]
    big_hbm = {name: ref for name, ref in zip(_MIXER_WEIGHTS, weights) if name in big_vmem}
    weights = [big_vmem.get(name, ref) for name, ref in zip(_MIXER_WEIGHTS, weights)]
    step = pl.program_id(0)

    def region_copies(region):
        if region in _REGION_W_IN_GROUPS:
            pairs = [(big_hbm["w_in"].at[:, pl.ds(k * D_MODEL, D_MODEL)], big_vmem["w_in"].at[:, pl.ds(k * D_MODEL, D_MODEL)])
                     for k in _REGION_W_IN_GROUPS[region]]
        else:
            pairs = [(big_hbm[name], big_vmem[name]) for name in _MIXER_BIG_WEIGHTS[1:]]
        return [pltpu.make_async_copy(src, dst, sem.at[region - 1]) for src, dst in pairs]

    @pl.when(step == 0)
    def _():
        for region in (1, 2, 3):
            for c in region_copies(region):
                c.start()

    def wait_weights(region):
        @pl.when(step == 0)
        def _():
            for c in region_copies(region):
                c.wait()

    _mixer_body(lambda: x_ref[...], step, step + zero_ref[0] >= 0, wait_weights, *states, *weights, *rest,
                n_seq=n_seq, n_t=n_t, layer=layer)


def _seq_major_tile_copies(seq_hbm, tile_buf, sem, tile, slot, *, n_seq, n_t):
    copies = []
    for b in range(n_seq):
        if isinstance(tile, int):
            assert tile == 0
            hbm = seq_hbm.at[b, pl.ds(0, n_t - N_META), :]
            vmem = tile_buf.at[slot, pl.ds(N_META, n_t - N_META), b, :]
        else:
            t0 = pl.multiple_of(tile * n_t - N_META, SUBLANES)
            hbm = seq_hbm.at[b, pl.ds(t0, n_t), :]
            vmem = tile_buf.at[slot, :, b, :]
        copies.append(pltpu.make_async_copy(hbm, vmem, sem.at[slot]))
    return copies


def _load_and_cast_mixer_weights(f32_hbm, bf16_vmem, stage, sem, layer):
    chunks = []
    for k in range(N_SPLITS):
        cols = pl.ds(k * D_MODEL, D_MODEL)
        chunks.append((f32_hbm["w_in"].at[layer, :, cols], bf16_vmem["w_in"].at[:, cols], k >= N_SPLITS - 2))
    for name in _MIXER_BIG_WEIGHTS[1:]:
        chunks.append((f32_hbm[name].at[layer], bf16_vmem[name], False))
    copy = lambda c: pltpu.make_async_copy(chunks[c][0], stage.at[c % 2], sem.at[c % 2])
    copy(0).start()
    copy(1).start()
    for c, (_, dst, halve) in enumerate(chunks):
        copy(c).wait()
        value = stage[c % 2]
        dst[...] = (0.5 * value if halve else value).astype(BF16)
        if c + 2 < len(chunks):
            copy(c + 2).start()


def _mixer_from_seq_major_kernel(xs_hbm, meta_ref, zero_ref, *refs, n_seq, n_t, layer):
    refs = list(refs)
    n_w, n_out, n_cast, n_big = len(_MIXER_WEIGHTS), 4, len(_FFN_BIG_WEIGHTS), len(_MIXER_BIG_WEIGHTS)
    states, weights, cast_src = refs[:3], refs[3:3 + n_w], refs[3 + n_w:3 + n_w + n_cast]
    o = 3 + n_w + n_cast
    outs, cast_dst, big_out = refs[o:o + n_out], refs[o + n_out:o + n_out + n_cast], refs[o + n_out + n_cast:o + n_out + n_cast + n_big]
    sc = o + n_out + n_cast + n_big
    n_body_scratch = len(refs) - sc - 2 - n_big - 3
    body_scratch, (xin, sem) = refs[sc:sc + n_body_scratch], refs[sc + n_body_scratch:sc + n_body_scratch + 2]
    big_vmem = dict(zip(_MIXER_BIG_WEIGHTS, refs[sc + n_body_scratch + 2:sc + n_body_scratch + 2 + n_big]))
    stage, load_sem, store_sem = refs[sc + n_body_scratch + 2 + n_big:]
    big_f32 = {name: ref for name, ref in zip(_MIXER_WEIGHTS, weights) if name in big_vmem}
    weights = [big_vmem.get(name, ref) for name, ref in zip(_MIXER_WEIGHTS, weights)]

    step = pl.program_id(0)
    n_steps = pl.num_programs(0)
    slot = step % 2
    copies = functools.partial(_seq_major_tile_copies, xs_hbm, xin, sem, n_seq=n_seq, n_t=n_t)
    write_back = [pltpu.make_async_copy(big_vmem[name], dst, store_sem.at[0])
                  for name, dst in zip(_MIXER_BIG_WEIGHTS, big_out)]

    @pl.when(step == 0)
    def _():
        for c in copies(0, 0):
            c.start()
        xin[0, 0:N_META, :, :] = jnp.broadcast_to(meta_ref[...][:, None, :], (N_META, n_seq, D_MODEL))

        @pl.when(n_steps > 1)
        def _():
            for c in copies(step + 1, 1):
                c.start()
        _load_and_cast_mixer_weights(big_f32, big_vmem, stage, load_sem, layer)
        for c in write_back:
            c.start()
        for c in copies(0, 0):
            c.wait()

    @pl.when(step > 0)
    def _():
        @pl.when(step + 1 < n_steps)
        def _():
            for c in copies(step + 1, 1 - slot):
                c.start()
        for c in copies(step, slot):
            c.wait()

    _mixer_body(lambda: xin[slot].reshape(n_t * n_seq, D_MODEL), step, step + zero_ref[0] >= 0, lambda region: None,
                *states, *weights, *outs, *body_scratch, n_seq=n_seq, n_t=n_t, layer=layer)
    _cast_weight_blocks(_FFN_BIG_WEIGHTS, cast_src, cast_dst)

    @pl.when(step == n_steps - 1)
    def _():
        for c in write_back:
            c.wait()


def _ffn_body(x, *, g2_ref, wg_ref, wu_ref, wd_ref, gf_ref, final_norm, layer):
    v = _rmsnorm(x, g2_ref[layer:layer + 1, :]).astype(BF16)
    half = _dot(v, wg_ref[...])
    hid = ((half * jnp.tanh(half) + half) * _dot(v, wu_ref[...])).astype(BF16)
    y = x + _dot(hid, wd_ref[...])
    if final_norm:
        y = _rmsnorm(y, gf_ref[...])
    return y


def _cast_weight_blocks(names, src_refs, dst_refs):
    @pl.when(pl.program_id(0) < N_CONVERT_STEPS)
    def _():
        for name, src, dst in zip(names, src_refs, dst_refs):
            if name == "w_in":
                plain = (N_SPLITS - 2) * D_MODEL
                dst[:, :plain] = src[:, :plain].astype(BF16)
                dst[:, plain:] = (0.5 * src[:, plain:]).astype(BF16)
            elif name == "w_ff_gate":
                dst[...] = (0.5 * src[...]).astype(BF16)
            else:
                dst[...] = src[...].astype(BF16)


def _ffn_kernel(xp_ref, xs_ref, g2_ref, wg_ref, wu_ref, wd_ref, gf_ref, *rest, final_norm, layer, n_prompt_steps):
    n_big = (len(rest) - 2) // 2
    next_f32, (op_ref, os_ref), next_bf16 = rest[:n_big], rest[n_big:n_big + 2], rest[n_big + 2:]
    body = functools.partial(_ffn_body, g2_ref=g2_ref, wg_ref=wg_ref, wu_ref=wu_ref, wd_ref=wd_ref, gf_ref=gf_ref,
                             final_norm=final_norm, layer=layer)
    step = pl.program_id(0)

    @pl.when(step < n_prompt_steps)
    def _():
        op_ref[...] = body(xp_ref[...])

    @pl.when(step == n_prompt_steps)
    def _():
        os_ref[...] = body(xs_ref[...])

    if n_big:
        _cast_weight_blocks(_BIG_WEIGHTS, next_f32, next_bf16)


def _ffn_to_seq_major_kernel(xp_hbm, xs_ref, g2_ref, wg_ref, wu_ref, wd_ref, gf_ref, yp_ref, os_ref, xin, sem,
                             *, n_t, layer, n_prompt_steps):
    step = pl.program_id(0)
    slot = step % 2
    tiles_per_seq = (xp_hbm.shape[0] - N_META) // n_t
    body = functools.partial(_ffn_body, g2_ref=g2_ref, wg_ref=wg_ref, wu_ref=wu_ref, wd_ref=wd_ref, gf_ref=gf_ref,
                             final_norm=True, layer=layer)

    def fetch(tile, into):
        t0 = N_META + (tile % tiles_per_seq) * n_t
        return pltpu.make_async_copy(xp_hbm.at[pl.ds(t0, n_t), tile // tiles_per_seq, :], xin.at[into], sem.at[into])

    @pl.when(step == 0)
    def _():
        fetch(0, 0).start()

    @pl.when(step < n_prompt_steps)
    def _():
        @pl.when(step + 1 < n_prompt_steps)
        def _():
            fetch(step + 1, 1 - slot).start()
        fetch(step, slot).wait()
        yp_ref[...] = body(xin[slot])

    @pl.when(step == n_prompt_steps)
    def _():
        os_ref[...] = body(xs_ref[...])


def _resident(shape, index):
    return pl.BlockSpec(shape, index, pipeline_mode=pl.Buffered(1))


def _weight_operands(names, small, big, layer):
    specs, arrays = [], []
    for name in names:
        if name not in big and name not in small:
            specs.append(None)
            arrays.append(None)
            continue
        arr = big[name] if name in big else small[name]
        if name in big or arr.ndim == 2:
            specs.append(_resident(arr.shape, lambda i, nd=arr.ndim: (0,) * nd))
        else:
            specs.append(_resident((None,) + arr.shape[1:], lambda i, nd=arr.ndim: (layer,) + (0,) * (nd - 1)))
        arrays.append(arr)
    return specs, arrays


_MIXER_WEIGHTS = ("norm1_g", "w_in", "rnn_conv_w", "rnn_conv_b", "w_gate", "gate_a_b", "gate_x_b",
                  "lru_lambda", "w_branch_a", "sc_conv_w", "w_branch_b", "w_out")
_FFN_WEIGHTS = ("norm2_g", "w_ff_gate", "w_ff_up", "w_ff_down")
_BIG_WEIGHTS = ("w_in", "w_branch_a", "w_branch_b", "w_out", "w_ff_gate", "w_ff_up", "w_ff_down")
_MIXER_BIG_WEIGHTS, _FFN_BIG_WEIGHTS = _BIG_WEIGHTS[:4], _BIG_WEIGHTS[4:]
N_CONVERT_STEPS = 16
_COMPILER_PARAMS = pltpu.CompilerParams(dimension_semantics=("arbitrary",), vmem_limit_bytes=VMEM_LIMIT_BYTES)


def _cast_operands(names, stacked_f32, layer, n_steps):
    assert n_steps >= N_CONVERT_STEPS
    block = lambda i: jnp.minimum(i, N_CONVERT_STEPS - 1)
    in_specs, out_specs, out_shapes = [], [], []
    for name in names:
        _, n_in, n_out = stacked_f32[name].shape
        blk = n_in // N_CONVERT_STEPS
        assert blk * N_CONVERT_STEPS == n_in and blk % (2 * SUBLANES) == 0
        in_specs.append(pl.BlockSpec((None, blk, n_out), lambda i: (layer, block(i), 0)))
        out_specs.append(pl.BlockSpec((blk, n_out), lambda i: (block(i), 0)))
        out_shapes.append(jax.ShapeDtypeStruct((n_in, n_out), BF16))
    return in_specs, out_specs, out_shapes, [stacked_f32[name] for name in names]


def _mixer_call(x, h0, rc0, sc0, small, big, layer, *, n_seq, n_t, meta=None, big_f32=None):
    rows = n_seq * n_t
    rc_rows = (RNN_CONV_W - 1) * n_seq
    sc_rows = (SC_CONV_W - 1) * n_seq
    row_spec = pl.BlockSpec((rows, D_MODEL), lambda i: (i, 0))
    state_layer = layer if h0.shape[0] > 1 else 0
    state_spec = lambda r: _resident((None, r, D_MODEL), lambda i: (state_layer, 0, 0))
    scratch = [pltpu.VMEM((rows, D_MODEL), BF16),
               pltpu.VMEM((rc_rows + rows, D_MODEL), F32),
               pltpu.VMEM((sc_rows + rows, D_MODEL), F32),
               pltpu.VMEM((rows, D_MODEL), F32),
               pltpu.VMEM((rows, D_MODEL), F32),
               pltpu.VMEM((rows, D_MODEL), BF16),
               pltpu.VMEM((rows, D_MODEL), BF16),
               pltpu.VMEM((rows, D_MODEL), F32),
               pltpu.VMEM((rows, D_MODEL), F32)]
    w_specs, w_arrays = _weight_operands(_MIXER_WEIGHTS, small, big, layer)
    zero, smem_spec = jnp.zeros((1,), jnp.int32), pl.BlockSpec(memory_space=pltpu.SMEM)
    if meta is None:
        body, x_args, n_rows = _mixer_kernel, (x, zero), x.shape[0]
        x_specs = [row_spec, smem_spec]
        for name in _MIXER_BIG_WEIGHTS:
            w_specs[_MIXER_WEIGHTS.index(name)] = pl.BlockSpec(memory_space=pl.ANY)
            scratch.append(pltpu.VMEM(big[name].shape, BF16))
        scratch.append(pltpu.SemaphoreType.DMA((3,)))
    else:
        assert n_t % SUBLANES == 0 and n_t > N_META and n_seq == SUBLANES
        body, x_args, n_rows = _mixer_from_seq_major_kernel, (x, meta, zero), (x.shape[1] + N_META) * n_seq
        x_specs = [pl.BlockSpec(memory_space=pl.ANY), _resident(meta.shape, lambda i: (0, 0)), smem_spec]
        scratch += [pltpu.VMEM((2, n_t, n_seq, D_MODEL), F32), pltpu.SemaphoreType.DMA((2,))]
    assert n_rows % rows == 0
    cast_in, cast_out, cast_shapes, cast_arrays = [], [], [], []
    if meta is not None:
        cast_in, cast_out, cast_shapes, cast_arrays = _cast_operands(_FFN_BIG_WEIGHTS, big_f32, layer, n_rows // rows)
        for name in _MIXER_BIG_WEIGHTS:
            i, shape = _MIXER_WEIGHTS.index(name), big_f32[name].shape[1:]
            w_specs[i], w_arrays[i] = pl.BlockSpec(memory_space=pl.ANY), big_f32[name]
            cast_out.append(pl.BlockSpec(memory_space=pl.ANY))
            cast_shapes.append(jax.ShapeDtypeStruct(shape, BF16))
            scratch.append(pltpu.VMEM(shape, BF16))
        scratch += [pltpu.VMEM((2, D_MODEL, D_MODEL), F32), pltpu.SemaphoreType.DMA((2,)), pltpu.SemaphoreType.DMA((1,))]
    outs = pl.pallas_call(
        functools.partial(body, n_seq=n_seq, n_t=n_t, layer=layer),
        grid=(n_rows // rows,),
        in_specs=x_specs + [state_spec(n_seq), state_spec(rc_rows), state_spec(sc_rows)] + w_specs + cast_in,
        out_specs=[row_spec,
                   pl.BlockSpec((n_seq, D_MODEL), lambda i: (0, 0)),
                   pl.BlockSpec((rc_rows, D_MODEL), lambda i: (0, 0)),
                   pl.BlockSpec((sc_rows, D_MODEL), lambda i: (0, 0))] + cast_out,
        out_shape=[jax.ShapeDtypeStruct((n_rows, D_MODEL), F32),
                   jax.ShapeDtypeStruct((n_seq, D_MODEL), F32),
                   jax.ShapeDtypeStruct((rc_rows, D_MODEL), F32),
                   jax.ShapeDtypeStruct((sc_rows, D_MODEL), F32)] + cast_shapes,
        scratch_shapes=scratch,
        compiler_params=_COMPILER_PARAMS,
        name=f"mixer_s{n_seq}" + ("_in" if meta is not None else ""),
    )(*x_args, h0, rc0, sc0, *w_arrays, *cast_arrays)
    if meta is None:
        return outs
    n_ffn = len(_FFN_BIG_WEIGHTS)
    return (*outs[:4], dict(zip(_FFN_BIG_WEIGHTS, outs[4:4 + n_ffn])), dict(zip(_MIXER_BIG_WEIGHTS, outs[4 + n_ffn:])))


def _ffn_call(xp, xs, small, big, layer, *, final_norm, rows=None, seq_major_out=None, next_f32=None):
    n_rows = xp.shape[0]
    sample_spec = pl.BlockSpec(xs.shape, lambda i: (0, 0))
    gf = small["final_norm_g"]
    w_specs, w_arrays = _weight_operands(_FFN_WEIGHTS, small, big, layer)
    in_specs = [None, _resident(xs.shape, lambda i: (0, 0))] + w_specs + [_resident(gf.shape, lambda i: (0, 0))]
    args = (xp, xs, *w_arrays, gf)
    sample_shape = jax.ShapeDtypeStruct(xs.shape, F32)
    if seq_major_out is not None:
        n_seq, n_t = seq_major_out
        n_time = n_rows // n_seq
        assert final_norm and next_f32 is None and (n_time - N_META) % n_t == 0
        tiles_per_seq = (n_time - N_META) // n_t
        n_prompt_steps = n_seq * tiles_per_seq
        last = n_prompt_steps - 1
        y_spec = pl.BlockSpec((None, n_t, D_MODEL), lambda i: (jnp.minimum(i, last) // tiles_per_seq,
                                                               jnp.minimum(i, last) % tiles_per_seq, 0))
        in_specs[0] = pl.BlockSpec(memory_space=pl.ANY)
        return pl.pallas_call(
            functools.partial(_ffn_to_seq_major_kernel, n_t=n_t, layer=layer, n_prompt_steps=n_prompt_steps),
            grid=(n_prompt_steps + 1,), in_specs=in_specs, out_specs=[y_spec, sample_spec],
            out_shape=[jax.ShapeDtypeStruct((n_seq, n_time - N_META, D_MODEL), F32), sample_shape],
            scratch_shapes=[pltpu.VMEM((2, n_t, D_MODEL), F32), pltpu.SemaphoreType.DMA((2,))],
            compiler_params=_COMPILER_PARAMS, name=f"ffn_t{n_t}_out",
        )(xp.reshape(n_time, n_seq, D_MODEL), *args[1:])
    assert n_rows % rows == 0
    n_prompt_steps = n_rows // rows
    prompt_spec = pl.BlockSpec((rows, D_MODEL), lambda i: (jnp.minimum(i, n_prompt_steps - 1), 0))
    in_specs[0] = prompt_spec
    out_specs, out_shape = [prompt_spec, sample_spec], [jax.ShapeDtypeStruct((n_rows, D_MODEL), F32), sample_shape]
    if next_f32 is not None:
        cast_in, cast_out, cast_shapes, cast_arrays = _cast_operands(_BIG_WEIGHTS, next_f32, layer + 1,
                                                                     n_prompt_steps)
        in_specs, out_specs, out_shape = in_specs + cast_in, out_specs + cast_out, out_shape + cast_shapes
        args += tuple(cast_arrays)
    outs = pl.pallas_call(
        functools.partial(_ffn_kernel, final_norm=final_norm, layer=layer, n_prompt_steps=n_prompt_steps),
        grid=(n_prompt_steps + 1,), in_specs=in_specs, out_specs=out_specs, out_shape=out_shape,
        compiler_params=_COMPILER_PARAMS,
        name=f"ffn_r{rows}" + ("_cast" if next_f32 is not None else ""),
    )(*args)
    if next_f32 is None:
        return outs
    return outs[0], outs[1], dict(zip(_BIG_WEIGHTS, outs[2:]))


def _block_diag_gates(gate_a_w, gate_x_w):
    hpc = GATE_CHUNK // RNN_HEAD_DIM

    def bd(wt):
        wt = wt.reshape(DEPTH, N_GATE_CHUNKS, hpc, RNN_HEAD_DIM, RNN_HEAD_DIM)
        bands = [jnp.pad(wt[:, :, p], ((0, 0), (0, 0), (0, 0), (p * RNN_HEAD_DIM, (hpc - 1 - p) * RNN_HEAD_DIM)))
                 for p in range(hpc)]
        return jnp.stack(bands, axis=2).reshape(DEPTH, N_GATE_CHUNKS, GATE_CHUNK, GATE_CHUNK)

    return jnp.concatenate([bd(gate_a_w), bd(gate_x_w)], axis=-1).astype(BF16)


def _time_major(states):
    return jnp.swapaxes(states, 1, 2).reshape(states.shape[0], -1, D_MODEL)


def _seq_major(states, n_seq):
    return jnp.swapaxes(jnp.stack(states).reshape(len(states), -1, n_seq, D_MODEL), 1, 2)


PROMPT_N_T = 86
PROMPT_FFN_ROWS = 688
PROMPT_IO_N_T = 48
PROMPT_OUT_N_T = 1024


def kernel(x_prompt, x_sample, state_rnn_h, state_rnn_conv, state_sc_conv, meta_tokens, norm1_g, w_in, rnn_conv_w, rnn_conv_b, gate_a_w, gate_a_b, gate_x_w, gate_x_b, lru_lambda, w_branch_a, sc_conv_w, w_branch_b, w_out, norm2_g, w_ff_gate, w_ff_up, w_ff_down, final_norm_g):
    small = {
        "norm1_g": norm1_g, "rnn_conv_w": rnn_conv_w, "rnn_conv_b": rnn_conv_b,
        "w_gate": _block_diag_gates(0.5 * gate_a_w, 0.5 * gate_x_w),
        "gate_a_b": gate_a_b, "gate_x_b": gate_x_b, "lru_lambda": lru_lambda,
        "sc_conv_w": sc_conv_w, "norm2_g": norm2_g, "final_norm_g": final_norm_g.reshape(1, D_MODEL),
    }
    big_f32 = {"w_in": w_in, "w_branch_a": w_branch_a, "w_branch_b": w_branch_b, "w_out": w_out,
               "w_ff_gate": w_ff_gate, "w_ff_up": w_ff_up, "w_ff_down": w_ff_down}
    big = [{}]
    dt = x_prompt.dtype

    bp = x_prompt.shape[0]
    bs, t_s, _ = x_sample.shape
    zeros = lambda k: jnp.zeros((1, k * bp, D_MODEL), dt)
    prompt_states = (zeros(1), zeros(RNN_CONV_W - 1), zeros(SC_CONV_W - 1))
    sample_states = (state_rnn_h, _time_major(state_rnn_conv), _time_major(state_sc_conv))
    xp = x_prompt
    xs = jnp.swapaxes(x_sample, 0, 1).reshape(t_s * bs, D_MODEL)
    new_p, new_s = [], []
    for layer in range(DEPTH):
        if layer == 0:
            xp, *st_p, ffn_big, mixer_big = _mixer_call(xp, *prompt_states, small, big[layer], layer, n_seq=bp,
                                                        n_t=PROMPT_IO_N_T, meta=meta_tokens.astype(dt), big_f32=big_f32)
            big[layer] = {**mixer_big, **ffn_big}
        else:
            xp, *st_p = _mixer_call(xp, *prompt_states, small, big[layer], layer, n_seq=bp, n_t=PROMPT_N_T)
        xs, *st_s = _mixer_call(xs, *sample_states, small, big[layer], layer, n_seq=bs, n_t=t_s)
        new_p.append(st_p)
        new_s.append(st_s)
        if layer == DEPTH - 1:
            y_prompt, ys = _ffn_call(xp, xs, small, big[layer], layer, final_norm=True,
                                     seq_major_out=(bp, PROMPT_OUT_N_T))
        else:
            xp, xs, next_big = _ffn_call(xp, xs, small, big[layer], layer, rows=PROMPT_FFN_ROWS, final_norm=False,
                                         next_f32=big_f32)
            big.append(next_big)
    y_sample = jnp.swapaxes(ys.reshape(t_s, bs, D_MODEL), 0, 1)

    def collect(states, n_seq):
        hs, rcs, scs = zip(*states)
        return jnp.stack(hs), _seq_major(list(rcs), n_seq), _seq_major(list(scs), n_seq)

    rnn_h_p, rnn_conv_p, sc_conv_p = collect(new_p, bp)
    rnn_h_s, rnn_conv_s, sc_conv_s = collect(new_s, bs)

    return (y_prompt, y_sample, rnn_h_p, rnn_conv_p, sc_conv_p, rnn_h_s, rnn_conv_s, sc_conv_s)
```

```python
import functools
import math

import jax
import jax.numpy as jnp
from jax import lax
from jax.experimental import pallas as pl
from jax.experimental.pallas import tpu as pltpu

D_MODEL = 1024
DEPTH = 4
N_META = 16
N_RNN_HEADS = 16
RNN_HEAD_DIM = D_MODEL // N_RNN_HEADS
RNN_CONV_W = 4
SC_CONV_W = 3
LRU_C = 8.0
EPS = 1e-6
N_SPLITS = 7
GATE_CHUNK = 256
N_GATE_CHUNKS = D_MODEL // GATE_CHUNK
SUBLANES = 8

V7X_VMEM_BYTES = 64 * 1024 * 1024
VMEM_LIMIT_BYTES = V7X_VMEM_BYTES - 2 * 1024 * 1024

F32 = jnp.float32
BF16 = jnp.bfloat16


def _sigmoid_of_half(half_x):
    return 0.5 * jnp.tanh(half_x) + 0.5


def _gelu_tanh(x):
    c = math.sqrt(2.0 / math.pi)
    half_x = 0.5 * x
    return half_x * jnp.tanh(x * (c + (c * 0.044715) * (x * x))) + half_x


def _softplus(x):
    return jnp.maximum(x, 0.0) + jnp.log1p(jnp.exp(-jnp.abs(x)))


def _rmsnorm(x, g):
    ms = jnp.mean(x * x, axis=-1, keepdims=True)
    return x * lax.rsqrt(ms + EPS) * g


def _dot(a, b):
    return jnp.dot(a, b, preferred_element_type=F32)


def _mixer_body(read_x, step, region_gate, wait_weights, h0_ref, rc0_ref, sc0_ref, g1_ref, win_ref, cw_ref, cb_ref, wgate_ref,
                ba_ref, bx_ref, lam_ref, wba_ref, scw_ref, wbb_ref, wout_ref,
                xo_ref, h_ref, rc_ref, sc_ref,
                u_bf, xrbuf, chbuf, a_buf, h_buf, ya_buf, yb_buf, ga_buf, gb_buf, *, n_seq, n_t, layer):
    rows = n_seq * n_t
    rc_rows = (RNN_CONV_W - 1) * n_seq
    sc_rows = (SC_CONV_W - 1) * n_seq
    row = slice(layer, layer + 1)

    def proj(k):
        return _dot(u_bf[...], win_ref[:, k * D_MODEL:(k + 1) * D_MODEL])

    wait_weights(1)

    @pl.when(step == 0)
    def _():
        h_ref[...] = h0_ref[...]
        xrbuf[0:rc_rows, :] = rc0_ref[...]
        chbuf[0:sc_rows, :] = sc0_ref[...]

    u_bf[...] = _rmsnorm(read_x(), g1_ref[row, :]).astype(BF16)
    xrbuf[rc_rows:rc_rows + rows, :] = proj(0)
    chbuf[sc_rows:sc_rows + rows, :] = proj(3) * proj(4)
    xc = cb_ref[row, :] + xrbuf[0:rows, :] * cw_ref[0:1, :]
    for k in range(1, RNN_CONV_W):
        xc = xc + xrbuf[k * n_seq:k * n_seq + rows, :] * cw_ref[k:k + 1, :]
    h_buf[...] = xc
    xc_b = xc.astype(BF16)
    for j in range(N_GATE_CHUNKS):
        cols = slice(j * GATE_CHUNK, (j + 1) * GATE_CHUNK)
        gates = _dot(xc_b[:, cols], wgate_ref[j])
        ga_buf[:, cols] = gates[:, :GATE_CHUNK]
        gb_buf[:, cols] = gates[:, GATE_CHUNK:]

    @pl.when(region_gate)
    def _():
        wait_weights(2)
        half_log_a_max = (-0.5 * LRU_C) * _softplus(-lam_ref[row, :])
        t_r = jnp.tanh(ga_buf[...] + 0.5 * ba_ref[row, :])
        log_a = t_r * half_log_a_max + half_log_a_max
        a = jnp.exp(log_a)
        one_minus_a2 = jnp.maximum(jnp.tanh(log_a) * (-1.0 - a * a), 1e-12)
        mult = one_minus_a2 * lax.rsqrt(one_minus_a2)
        gate_i = _sigmoid_of_half(gb_buf[...] + 0.5 * bx_ref[row, :])
        a_buf[...] = a
        h_buf[...] = mult * gate_i * h_buf[...]

        if n_seq == SUBLANES:
            def scan_step(t, h):
                sl = pl.ds(pl.multiple_of(t * n_seq, n_seq), n_seq)
                h = a_buf[sl, :] * h + h_buf[sl, :]
                h_buf[sl, :] = h
                return h
            h_ref[...] = lax.fori_loop(0, n_t, scan_step, h_ref[...], unroll=True)
        else:
            for t in range(n_t):
                sl = slice(t * n_seq, (t + 1) * n_seq)
                prev = h_ref[...] if t == 0 else h_buf[(t - 1) * n_seq:t * n_seq, :]
                h_buf[sl, :] = a_buf[sl, :] * prev + h_buf[sl, :]
            h_ref[...] = h_buf[(n_t - 1) * n_seq:n_t * n_seq, :]

        ya_buf[...] = (h_buf[...] * _gelu_tanh(proj(1))).astype(BF16)

        vc = chbuf[0:rows, :] * scw_ref[0:1, :]
        for k in range(1, SC_CONV_W):
            vc = vc + chbuf[k * n_seq:k * n_seq + rows, :] * scw_ref[k:k + 1, :]
        yb_buf[...] = (proj(2) * vc).astype(BF16)

        gb_buf[...] = _sigmoid_of_half(proj(6))
        ga_buf[...] = _sigmoid_of_half(proj(5))

        new_rc = xrbuf[rows:rows + rc_rows, :]
        new_sc = chbuf[rows:rows + sc_rows, :]
        xrbuf[0:rc_rows, :] = new_rc
        chbuf[0:sc_rows, :] = new_sc

        @pl.when(step == pl.num_programs(0) - 1)
        def _():
            rc_ref[...] = new_rc
            sc_ref[...] = new_sc

    @pl.when(region_gate)
    def _():
        wait_weights(3)
        m = ga_buf[...] * _dot(ya_buf[...], wba_ref[...])
        m = m + gb_buf[...] * _dot(yb_buf[...], wbb_ref[...])
        xo_ref[...] = read_x() + _dot(m.astype(BF16), wout_ref[...])


_REGION_W_IN_GROUPS = {1: (0, 3, 4), 2: (1, 2, 5, 6)}


def _mixer_kernel(x_ref, zero_ref, *refs, n_seq, n_t, layer):
    refs = list(refs)
    n_w, n_big = len(_MIXER_WEIGHTS), len(_MIXER_BIG_WEIGHTS)
    *refs, sem = refs
    big_vmem = dict(zip(_MIXER_BIG_WEIGHTS, refs[-n_big:]))
    refs = refs[:-n_big]
    states, weights, rest = refs[:3], refs[3:3 + n_w], refs[3 + n_w:]
    big_hbm = {name: ref for name, ref in zip(_MIXER_WEIGHTS, weights) if name in big_vmem}
    weights = [big_vmem.get(name, ref) for name, ref in zip(_MIXER_WEIGHTS, weights)]
    step = pl.program_id(0)

    def region_copies(region):
        if region in _REGION_W_IN_GROUPS:
            pairs = [(big_hbm["w_in"].at[:, pl.ds(k * D_MODEL, D_MODEL)], big_vmem["w_in"].at[:, pl.ds(k * D_MODEL, D_MODEL)])
                     for k in _REGION_W_IN_GROUPS[region]]
        else:
            pairs = [(big_hbm[name], big_vmem[name]) for name in _MIXER_BIG_WEIGHTS[1:]]
        return [pltpu.make_async_copy(src, dst, sem.at[region - 1]) for src, dst in pairs]

    @pl.when(step == 0)
    def _():
        for region in (1, 2, 3):
            for c in region_copies(region):
                c.start()

    def wait_weights(region):
        @pl.when(step == 0)
        def _():
            for c in region_copies(region):
                c.wait()

    _mixer_body(lambda: x_ref[...], step, step + zero_ref[0] >= 0, wait_weights, *states, *weights, *rest,
                n_seq=n_seq, n_t=n_t, layer=layer)


def _seq_major_tile_copies(seq_hbm, tile_buf, sem, tile, slot, *, n_seq, n_t):
    copies = []
    for b in range(n_seq):
        if isinstance(tile, int):
            assert tile == 0
            hbm = seq_hbm.at[b, pl.ds(0, n_t - N_META), :]
            vmem = tile_buf.at[slot, pl.ds(N_META, n_t - N_META), b, :]
        else:
            t0 = pl.multiple_of(tile * n_t - N_META, SUBLANES)
            hbm = seq_hbm.at[b, pl.ds(t0, n_t), :]
            vmem = tile_buf.at[slot, :, b, :]
        copies.append(pltpu.make_async_copy(hbm, vmem, sem.at[slot]))
    return copies


_CAST_CHUNKS = {1: (("w_in", 0), ("w_in", 3), ("w_in", 4)),
                2: (("w_in", 1), ("w_in", 2), ("w_in", 5), ("w_in", 6)),
                3: (("w_branch_a", None), ("w_branch_b", None), ("w_out", None))}
_CAST_ORDER = _CAST_CHUNKS[1] + _CAST_CHUNKS[2] + _CAST_CHUNKS[3]


def _cast_chunk_copy(c, f32_hbm, stage, sem, layer):
    name, group = _CAST_ORDER[c]
    src = f32_hbm[name].at[layer] if group is None else f32_hbm[name].at[layer, :, pl.ds(group * D_MODEL, D_MODEL)]
    return pltpu.make_async_copy(src, stage.at[c % 2], sem.at[c % 2])


def _cast_region_chunks(region, f32_hbm, bf16_vmem, stage, sem, layer):
    first = sum(len(_CAST_CHUNKS[r]) for r in range(1, region))
    for c in range(first, first + len(_CAST_CHUNKS[region])):
        name, group = _CAST_ORDER[c]
        _cast_chunk_copy(c, f32_hbm, stage, sem, layer).wait()
        value = stage[c % 2]
        if group is None:
            bf16_vmem[name][...] = value.astype(BF16)
        else:
            halve = group >= N_SPLITS - 2
            bf16_vmem[name][:, group * D_MODEL:(group + 1) * D_MODEL] = (0.5 * value if halve else value).astype(BF16)
        if c + 2 < len(_CAST_ORDER):
            _cast_chunk_copy(c + 2, f32_hbm, stage, sem, layer).start()


def _mixer_from_seq_major_kernel(xs_hbm, meta_ref, zero_ref, *refs, n_seq, n_t, layer):
    refs = list(refs)
    n_w, n_out, n_cast, n_big = len(_MIXER_WEIGHTS), 4, len(_FFN_BIG_WEIGHTS), len(_MIXER_BIG_WEIGHTS)
    states, weights, cast_src = refs[:3], refs[3:3 + n_w], refs[3 + n_w:3 + n_w + n_cast]
    o = 3 + n_w + n_cast
    outs, cast_dst, big_out = refs[o:o + n_out], refs[o + n_out:o + n_out + n_cast], refs[o + n_out + n_cast:o + n_out + n_cast + n_big]
    sc = o + n_out + n_cast + n_big
    n_body_scratch = len(refs) - sc - 2 - n_big - 3
    body_scratch, (xin, sem) = refs[sc:sc + n_body_scratch], refs[sc + n_body_scratch:sc + n_body_scratch + 2]
    big_vmem = dict(zip(_MIXER_BIG_WEIGHTS, refs[sc + n_body_scratch + 2:sc + n_body_scratch + 2 + n_big]))
    stage, load_sem, store_sem = refs[sc + n_body_scratch + 2 + n_big:]
    big_f32 = {name: ref for name, ref in zip(_MIXER_WEIGHTS, weights) if name in big_vmem}
    weights = [big_vmem.get(name, ref) for name, ref in zip(_MIXER_WEIGHTS, weights)]

    step = pl.program_id(0)
    n_steps = pl.num_programs(0)
    slot = step % 2
    copies = functools.partial(_seq_major_tile_copies, xs_hbm, xin, sem, n_seq=n_seq, n_t=n_t)
    write_back = [pltpu.make_async_copy(big_vmem[name], dst, store_sem.at[0])
                  for name, dst in zip(_MIXER_BIG_WEIGHTS, big_out)]

    @pl.when(step == 0)
    def _():
        for c in copies(0, 0):
            c.start()
        xin[0, 0:N_META, :, :] = jnp.broadcast_to(meta_ref[...][:, None, :], (N_META, n_seq, D_MODEL))

        @pl.when(n_steps > 1)
        def _():
            for c in copies(step + 1, 1):
                c.start()
        _cast_chunk_copy(0, big_f32, stage, load_sem, layer).start()
        _cast_chunk_copy(1, big_f32, stage, load_sem, layer).start()
        for c in copies(0, 0):
            c.wait()

    @pl.when(step > 0)
    def _():
        @pl.when(step + 1 < n_steps)
        def _():
            for c in copies(step + 1, 1 - slot):
                c.start()
        for c in copies(step, slot):
            c.wait()

    def wait_weights(region):
        @pl.when(step == 0)
        def _():
            _cast_region_chunks(region, big_f32, big_vmem, stage, load_sem, layer)
            if region == 3:
                for c in write_back:
                    c.start()

    _mixer_body(lambda: xin[slot].reshape(n_t * n_seq, D_MODEL), step, step + zero_ref[0] >= 0, wait_weights,
                *states, *weights, *outs, *body_scratch, n_seq=n_seq, n_t=n_t, layer=layer)
    _cast_weight_blocks(_FFN_BIG_WEIGHTS, cast_src, cast_dst)

    @pl.when(step == n_steps - 1)
    def _():
        for c in write_back:
            c.wait()


def _ffn_body(x, *, g2_ref, wg_ref, wu_ref, wd_ref, gf_ref, final_norm, layer):
    v = _rmsnorm(x, g2_ref[layer:layer + 1, :]).astype(BF16)
    half = _dot(v, wg_ref[...])
    hid = ((half * jnp.tanh(half) + half) * _dot(v, wu_ref[...])).astype(BF16)
    y = x + _dot(hid, wd_ref[...])
    if final_norm:
        y = _rmsnorm(y, gf_ref[...])
    return y


def _cast_weight_blocks(names, src_refs, dst_refs):
    @pl.when(pl.program_id(0) < N_CONVERT_STEPS)
    def _():
        for name, src, dst in zip(names, src_refs, dst_refs):
            if name == "w_in":
                plain = (N_SPLITS - 2) * D_MODEL
                dst[:, :plain] = src[:, :plain].astype(BF16)
                dst[:, plain:] = (0.5 * src[:, plain:]).astype(BF16)
            elif name == "w_ff_gate":
                dst[...] = (0.5 * src[...]).astype(BF16)
            else:
                dst[...] = src[...].astype(BF16)


def _ffn_kernel(xp_ref, xs_ref, g2_ref, wg_ref, wu_ref, wd_ref, gf_ref, *rest, final_norm, layer, n_prompt_steps):
    n_big = (len(rest) - 2) // 2
    next_f32, (op_ref, os_ref), next_bf16 = rest[:n_big], rest[n_big:n_big + 2], rest[n_big + 2:]
    body = functools.partial(_ffn_body, g2_ref=g2_ref, wg_ref=wg_ref, wu_ref=wu_ref, wd_ref=wd_ref, gf_ref=gf_ref,
                             final_norm=final_norm, layer=layer)
    step = pl.program_id(0)

    @pl.when(step < n_prompt_steps)
    def _():
        op_ref[...] = body(xp_ref[...])

    @pl.when(step == n_prompt_steps)
    def _():
        os_ref[...] = body(xs_ref[...])

    if n_big:
        _cast_weight_blocks(_BIG_WEIGHTS, next_f32, next_bf16)


def _ffn_to_seq_major_kernel(xp_hbm, xs_ref, g2_ref, wg_ref, wu_ref, wd_ref, gf_ref, yp_ref, os_ref, xin, sem,
                             *, n_t, layer, n_prompt_steps):
    step = pl.program_id(0)
    slot = step % 2
    tiles_per_seq = (xp_hbm.shape[0] - N_META) // n_t
    body = functools.partial(_ffn_body, g2_ref=g2_ref, wg_ref=wg_ref, wu_ref=wu_ref, wd_ref=wd_ref, gf_ref=gf_ref,
                             final_norm=True, layer=layer)

    def fetch(tile, into):
        t0 = N_META + (tile % tiles_per_seq) * n_t
        return pltpu.make_async_copy(xp_hbm.at[pl.ds(t0, n_t), tile // tiles_per_seq, :], xin.at[into], sem.at[into])

    @pl.when(step == 0)
    def _():
        fetch(0, 0).start()

    @pl.when(step < n_prompt_steps)
    def _():
        @pl.when(step + 1 < n_prompt_steps)
        def _():
            fetch(step + 1, 1 - slot).start()
        fetch(step, slot).wait()
        yp_ref[...] = body(xin[slot])

    @pl.when(step == n_prompt_steps)
    def _():
        os_ref[...] = body(xs_ref[...])


def _resident(shape, index):
    return pl.BlockSpec(shape, index, pipeline_mode=pl.Buffered(1))


def _weight_operands(names, small, big, layer):
    specs, arrays = [], []
    for name in names:
        if name not in big and name not in small:
            specs.append(None)
            arrays.append(None)
            continue
        arr = big[name] if name in big else small[name]
        if name in big or arr.ndim == 2:
            specs.append(_resident(arr.shape, lambda i, nd=arr.ndim: (0,) * nd))
        else:
            specs.append(_resident((None,) + arr.shape[1:], lambda i, nd=arr.ndim: (layer,) + (0,) * (nd - 1)))
        arrays.append(arr)
    return specs, arrays


_MIXER_WEIGHTS = ("norm1_g", "w_in", "rnn_conv_w", "rnn_conv_b", "w_gate", "gate_a_b", "gate_x_b",
                  "lru_lambda", "w_branch_a", "sc_conv_w", "w_branch_b", "w_out")
_FFN_WEIGHTS = ("norm2_g", "w_ff_gate", "w_ff_up", "w_ff_down")
_BIG_WEIGHTS = ("w_in", "w_branch_a", "w_branch_b", "w_out", "w_ff_gate", "w_ff_up", "w_ff_down")
_MIXER_BIG_WEIGHTS, _FFN_BIG_WEIGHTS = _BIG_WEIGHTS[:4], _BIG_WEIGHTS[4:]
N_CONVERT_STEPS = 16
_COMPILER_PARAMS = pltpu.CompilerParams(dimension_semantics=("arbitrary",), vmem_limit_bytes=VMEM_LIMIT_BYTES)


def _cast_operands(names, stacked_f32, layer, n_steps):
    assert n_steps >= N_CONVERT_STEPS
    block = lambda i: jnp.minimum(i, N_CONVERT_STEPS - 1)
    in_specs, out_specs, out_shapes = [], [], []
    for name in names:
        _, n_in, n_out = stacked_f32[name].shape
        blk = n_in // N_CONVERT_STEPS
        assert blk * N_CONVERT_STEPS == n_in and blk % (2 * SUBLANES) == 0
        in_specs.append(pl.BlockSpec((None, blk, n_out), lambda i: (layer, block(i), 0)))
        out_specs.append(pl.BlockSpec((blk, n_out), lambda i: (block(i), 0)))
        out_shapes.append(jax.ShapeDtypeStruct((n_in, n_out), BF16))
    return in_specs, out_specs, out_shapes, [stacked_f32[name] for name in names]


def _mixer_call(x, h0, rc0, sc0, small, big, layer, *, n_seq, n_t, meta=None, big_f32=None):
    rows = n_seq * n_t
    rc_rows = (RNN_CONV_W - 1) * n_seq
    sc_rows = (SC_CONV_W - 1) * n_seq
    row_spec = pl.BlockSpec((rows, D_MODEL), lambda i: (i, 0))
    state_layer = layer if h0.shape[0] > 1 else 0
    state_spec = lambda r: _resident((None, r, D_MODEL), lambda i: (state_layer, 0, 0))
    scratch = [pltpu.VMEM((rows, D_MODEL), BF16),
               pltpu.VMEM((rc_rows + rows, D_MODEL), F32),
               pltpu.VMEM((sc_rows + rows, D_MODEL), F32),
               pltpu.VMEM((rows, D_MODEL), F32),
               pltpu.VMEM((rows, D_MODEL), F32),
               pltpu.VMEM((rows, D_MODEL), BF16),
               pltpu.VMEM((rows, D_MODEL), BF16),
               pltpu.VMEM((rows, D_MODEL), F32),
               pltpu.VMEM((rows, D_MODEL), F32)]
    w_specs, w_arrays = _weight_operands(_MIXER_WEIGHTS, small, big, layer)
    zero, smem_spec = jnp.zeros((1,), jnp.int32), pl.BlockSpec(memory_space=pltpu.SMEM)
    if meta is None:
        body, x_args, n_rows = _mixer_kernel, (x, zero), x.shape[0]
        x_specs = [row_spec, smem_spec]
        for name in _MIXER_BIG_WEIGHTS:
            w_specs[_MIXER_WEIGHTS.index(name)] = pl.BlockSpec(memory_space=pl.ANY)
            scratch.append(pltpu.VMEM(big[name].shape, BF16))
        scratch.append(pltpu.SemaphoreType.DMA((3,)))
    else:
        assert n_t % SUBLANES == 0 and n_t > N_META and n_seq == SUBLANES
        body, x_args, n_rows = _mixer_from_seq_major_kernel, (x, meta, zero), (x.shape[1] + N_META) * n_seq
        x_specs = [pl.BlockSpec(memory_space=pl.ANY), _resident(meta.shape, lambda i: (0, 0)), smem_spec]
        scratch += [pltpu.VMEM((2, n_t, n_seq, D_MODEL), F32), pltpu.SemaphoreType.DMA((2,))]
    assert n_rows % rows == 0
    cast_in, cast_out, cast_shapes, cast_arrays = [], [], [], []
    if meta is not None:
        cast_in, cast_out, cast_shapes, cast_arrays = _cast_operands(_FFN_BIG_WEIGHTS, big_f32, layer, n_rows // rows)
        for name in _MIXER_BIG_WEIGHTS:
            i, shape = _MIXER_WEIGHTS.index(name), big_f32[name].shape[1:]
            w_specs[i], w_arrays[i] = pl.BlockSpec(memory_space=pl.ANY), big_f32[name]
            cast_out.append(pl.BlockSpec(memory_space=pl.ANY))
            cast_shapes.append(jax.ShapeDtypeStruct(shape, BF16))
            scratch.append(pltpu.VMEM(shape, BF16))
        scratch += [pltpu.VMEM((2, D_MODEL, D_MODEL), F32), pltpu.SemaphoreType.DMA((2,)), pltpu.SemaphoreType.DMA((1,))]
    outs = pl.pallas_call(
        functools.partial(body, n_seq=n_seq, n_t=n_t, layer=layer),
        grid=(n_rows // rows,),
        in_specs=x_specs + [state_spec(n_seq), state_spec(rc_rows), state_spec(sc_rows)] + w_specs + cast_in,
        out_specs=[row_spec,
                   pl.BlockSpec((n_seq, D_MODEL), lambda i: (0, 0)),
                   pl.BlockSpec((rc_rows, D_MODEL), lambda i: (0, 0)),
                   pl.BlockSpec((sc_rows, D_MODEL), lambda i: (0, 0))] + cast_out,
        out_shape=[jax.ShapeDtypeStruct((n_rows, D_MODEL), F32),
                   jax.ShapeDtypeStruct((n_seq, D_MODEL), F32),
                   jax.ShapeDtypeStruct((rc_rows, D_MODEL), F32),
                   jax.ShapeDtypeStruct((sc_rows, D_MODEL), F32)] + cast_shapes,
        scratch_shapes=scratch,
        compiler_params=_COMPILER_PARAMS,
        name=f"mixer_s{n_seq}" + ("_in" if meta is not None else ""),
    )(*x_args, h0, rc0, sc0, *w_arrays, *cast_arrays)
    if meta is None:
        return outs
    n_ffn = len(_FFN_BIG_WEIGHTS)
    return (*outs[:4], dict(zip(_FFN_BIG_WEIGHTS, outs[4:4 + n_ffn])), dict(zip(_MIXER_BIG_WEIGHTS, outs[4 + n_ffn:])))


def _ffn_call(xp, xs, small, big, layer, *, final_norm, rows=None, seq_major_out=None, next_f32=None):
    n_rows = xp.shape[0]
    sample_spec = pl.BlockSpec(xs.shape, lambda i: (0, 0))
    gf = small["final_norm_g"]
    w_specs, w_arrays = _weight_operands(_FFN_WEIGHTS, small, big, layer)
    in_specs = [None, _resident(xs.shape, lambda i: (0, 0))] + w_specs + [_resident(gf.shape, lambda i: (0, 0))]
    args = (xp, xs, *w_arrays, gf)
    sample_shape = jax.ShapeDtypeStruct(xs.shape, F32)
    if seq_major_out is not None:
        n_seq, n_t = seq_major_out
        n_time = n_rows // n_seq
        assert final_norm and next_f32 is None and (n_time - N_META) % n_t == 0
        tiles_per_seq = (n_time - N_META) // n_t
        n_prompt_steps = n_seq * tiles_per_seq
        last = n_prompt_steps - 1
        y_spec = pl.BlockSpec((None, n_t, D_MODEL), lambda i: (jnp.minimum(i, last) // tiles_per_seq,
                                                               jnp.minimum(i, last) % tiles_per_seq, 0))
        in_specs[0] = pl.BlockSpec(memory_space=pl.ANY)
        return pl.pallas_call(
            functools.partial(_ffn_to_seq_major_kernel, n_t=n_t, layer=layer, n_prompt_steps=n_prompt_steps),
            grid=(n_prompt_steps + 1,), in_specs=in_specs, out_specs=[y_spec, sample_spec],
            out_shape=[jax.ShapeDtypeStruct((n_seq, n_time - N_META, D_MODEL), F32), sample_shape],
            scratch_shapes=[pltpu.VMEM((2, n_t, D_MODEL), F32), pltpu.SemaphoreType.DMA((2,))],
            compiler_params=_COMPILER_PARAMS, name=f"ffn_t{n_t}_out",
        )(xp.reshape(n_time, n_seq, D_MODEL), *args[1:])
    assert n_rows % rows == 0
    n_prompt_steps = n_rows // rows
    prompt_spec = pl.BlockSpec((rows, D_MODEL), lambda i: (jnp.minimum(i, n_prompt_steps - 1), 0))
    in_specs[0] = prompt_spec
    out_specs, out_shape = [prompt_spec, sample_spec], [jax.ShapeDtypeStruct((n_rows, D_MODEL), F32), sample_shape]
    if next_f32 is not None:
        cast_in, cast_out, cast_shapes, cast_arrays = _cast_operands(_BIG_WEIGHTS, next_f32, layer + 1,
                                                                     n_prompt_steps)
        in_specs, out_specs, out_shape = in_specs + cast_in, out_specs + cast_out, out_shape + cast_shapes
        args += tuple(cast_arrays)
    outs = pl.pallas_call(
        functools.partial(_ffn_kernel, final_norm=final_norm, layer=layer, n_prompt_steps=n_prompt_steps),
        grid=(n_prompt_steps + 1,), in_specs=in_specs, out_specs=out_specs, out_shape=out_shape,
        compiler_params=_COMPILER_PARAMS,
        name=f"ffn_r{rows}" + ("_cast" if next_f32 is not None else ""),
    )(*args)
    if next_f32 is None:
        return outs
    return outs[0], outs[1], dict(zip(_BIG_WEIGHTS, outs[2:]))


def _block_diag_gates(gate_a_w, gate_x_w):
    hpc = GATE_CHUNK // RNN_HEAD_DIM

    def bd(wt):
        wt = wt.reshape(DEPTH, N_GATE_CHUNKS, hpc, RNN_HEAD_DIM, RNN_HEAD_DIM)
        bands = [jnp.pad(wt[:, :, p], ((0, 0), (0, 0), (0, 0), (p * RNN_HEAD_DIM, (hpc - 1 - p) * RNN_HEAD_DIM)))
                 for p in range(hpc)]
        return jnp.stack(bands, axis=2).reshape(DEPTH, N_GATE_CHUNKS, GATE_CHUNK, GATE_CHUNK)

    return jnp.concatenate([bd(gate_a_w), bd(gate_x_w)], axis=-1).astype(BF16)


def _time_major(states):
    return jnp.swapaxes(states, 1, 2).reshape(states.shape[0], -1, D_MODEL)


def _seq_major(states, n_seq):
    return jnp.swapaxes(jnp.stack(states).reshape(len(states), -1, n_seq, D_MODEL), 1, 2)


PROMPT_N_T = 86
PROMPT_FFN_ROWS = 688
PROMPT_IO_N_T = 48
PROMPT_OUT_N_T = 1024


def kernel(x_prompt, x_sample, state_rnn_h, state_rnn_conv, state_sc_conv, meta_tokens, norm1_g, w_in, rnn_conv_w, rnn_conv_b, gate_a_w, gate_a_b, gate_x_w, gate_x_b, lru_lambda, w_branch_a, sc_conv_w, w_branch_b, w_out, norm2_g, w_ff_gate, w_ff_up, w_ff_down, final_norm_g):
    small = {
        "norm1_g": norm1_g, "rnn_conv_w": rnn_conv_w, "rnn_conv_b": rnn_conv_b,
        "w_gate": _block_diag_gates(0.5 * gate_a_w, 0.5 * gate_x_w),
        "gate_a_b": gate_a_b, "gate_x_b": gate_x_b, "lru_lambda": lru_lambda,
        "sc_conv_w": sc_conv_w, "norm2_g": norm2_g, "final_norm_g": final_norm_g.reshape(1, D_MODEL),
    }
    big_f32 = {"w_in": w_in, "w_branch_a": w_branch_a, "w_branch_b": w_branch_b, "w_out": w_out,
               "w_ff_gate": w_ff_gate, "w_ff_up": w_ff_up, "w_ff_down": w_ff_down}
    big = [{}]
    dt = x_prompt.dtype

    bp = x_prompt.shape[0]
    bs, t_s, _ = x_sample.shape
    zeros = lambda k: jnp.zeros((1, k * bp, D_MODEL), dt)
    prompt_states = (zeros(1), zeros(RNN_CONV_W - 1), zeros(SC_CONV_W - 1))
    sample_states = (state_rnn_h, _time_major(state_rnn_conv), _time_major(state_sc_conv))
    xp = x_prompt
    xs = jnp.swapaxes(x_sample, 0, 1).reshape(t_s * bs, D_MODEL)
    new_p, new_s = [], []
    for layer in range(DEPTH):
        if layer == 0:
            xp, *st_p, ffn_big, mixer_big = _mixer_call(xp, *prompt_states, small, big[layer], layer, n_seq=bp,
                                                        n_t=PROMPT_IO_N_T, meta=meta_tokens.astype(dt), big_f32=big_f32)
            big[layer] = {**mixer_big, **ffn_big}
        else:
            xp, *st_p = _mixer_call(xp, *prompt_states, small, big[layer], layer, n_seq=bp, n_t=PROMPT_N_T)
        xs, *st_s = _mixer_call(xs, *sample_states, small, big[layer], layer, n_seq=bs, n_t=t_s)
        new_p.append(st_p)
        new_s.append(st_s)
        if layer == DEPTH - 1:
            y_prompt, ys = _ffn_call(xp, xs, small, big[layer], layer, final_norm=True,
                                     seq_major_out=(bp, PROMPT_OUT_N_T))
        else:
            xp, xs, next_big = _ffn_call(xp, xs, small, big[layer], layer, rows=PROMPT_FFN_ROWS, final_norm=False,
                                         next_f32=big_f32)
            big.append(next_big)
    y_sample = jnp.swapaxes(ys.reshape(t_s, bs, D_MODEL), 0, 1)

    def collect(states, n_seq):
        hs, rcs, scs = zip(*states)
        return jnp.stack(hs), _seq_major(list(rcs), n_seq), _seq_major(list(scs), n_seq)

    rnn_h_p, rnn_conv_p, sc_conv_p = collect(new_p, bp)
    rnn_h_s, rnn_conv_s, sc_conv_s = collect(new_s, bs)

    return (y_prompt, y_sample, rnn_h_p, rnn_conv_p, sc_conv_p, rnn_h_s, rnn_conv_s, sc_conv_s)
```

```python
import functools
import math

import jax
import jax.numpy as jnp
from jax import lax
from jax.experimental import pallas as pl
from jax.experimental.pallas import tpu as pltpu

D_MODEL = 1024
DEPTH = 4
N_META = 16
N_RNN_HEADS = 16
RNN_HEAD_DIM = D_MODEL // N_RNN_HEADS
RNN_CONV_W = 4
SC_CONV_W = 3
LRU_C = 8.0
EPS = 1e-6
N_SPLITS = 7
GATE_CHUNK = 256
N_GATE_CHUNKS = D_MODEL // GATE_CHUNK
SUBLANES = 8

V7X_VMEM_BYTES = 64 * 1024 * 1024
VMEM_LIMIT_BYTES = V7X_VMEM_BYTES - 2 * 1024 * 1024

F32 = jnp.float32
BF16 = jnp.bfloat16


def _sigmoid_of_half(half_x):
    return 0.5 * jnp.tanh(half_x) + 0.5


def _gelu_tanh(x):
    c = math.sqrt(2.0 / math.pi)
    half_x = 0.5 * x
    return half_x * jnp.tanh(x * (c + (c * 0.044715) * (x * x))) + half_x


def _softplus(x):
    return jnp.maximum(x, 0.0) + jnp.log1p(jnp.exp(-jnp.abs(x)))


def _rmsnorm(x, g):
    ms = jnp.mean(x * x, axis=-1, keepdims=True)
    return x * lax.rsqrt(ms + EPS) * g


def _dot(a, b):
    return jnp.dot(a, b, preferred_element_type=F32)


def _mixer_body(read_x, step, region_gate, wait_weights, h0_ref, rc0_ref, sc0_ref, g1_ref, win_ref, cw_ref, cb_ref, wgate_ref,
                ba_ref, bx_ref, lam_ref, wba_ref, scw_ref, wbb_ref, wout_ref,
                xo_ref, h_ref, rc_ref, sc_ref,
                u_bf, xrbuf, chbuf, a_buf, h_buf, ya_buf, yb_buf, ga_buf, gb_buf, *, n_seq, n_t, layer):
    rows = n_seq * n_t
    rc_rows = (RNN_CONV_W - 1) * n_seq
    sc_rows = (SC_CONV_W - 1) * n_seq
    row = slice(layer, layer + 1)

    def proj(k):
        return _dot(u_bf[...], win_ref[:, k * D_MODEL:(k + 1) * D_MODEL])

    wait_weights(1)

    @pl.when(step == 0)
    def _():
        h_ref[...] = h0_ref[...]
        xrbuf[0:rc_rows, :] = rc0_ref[...]
        chbuf[0:sc_rows, :] = sc0_ref[...]

    u_bf[...] = _rmsnorm(read_x(), g1_ref[row, :]).astype(BF16)
    xrbuf[rc_rows:rc_rows + rows, :] = proj(0)
    chbuf[sc_rows:sc_rows + rows, :] = proj(3) * proj(4)
    xc = cb_ref[row, :] + xrbuf[0:rows, :] * cw_ref[0:1, :]
    for k in range(1, RNN_CONV_W):
        xc = xc + xrbuf[k * n_seq:k * n_seq + rows, :] * cw_ref[k:k + 1, :]
    h_buf[...] = xc
    xc_b = xc.astype(BF16)
    for j in range(N_GATE_CHUNKS):
        cols = slice(j * GATE_CHUNK, (j + 1) * GATE_CHUNK)
        gates = _dot(xc_b[:, cols], wgate_ref[j])
        ga_buf[:, cols] = gates[:, :GATE_CHUNK]
        gb_buf[:, cols] = gates[:, GATE_CHUNK:]

    @pl.when(region_gate)
    def _():
        wait_weights(2)
        half_log_a_max = (-0.5 * LRU_C) * _softplus(-lam_ref[row, :])
        t_r = jnp.tanh(ga_buf[...] + 0.5 * ba_ref[row, :])
        log_a = t_r * half_log_a_max + half_log_a_max
        a = jnp.exp(log_a)
        one_minus_a2 = jnp.maximum(jnp.tanh(log_a) * (-1.0 - a * a), 1e-12)
        mult = one_minus_a2 * lax.rsqrt(one_minus_a2)
        gate_i = _sigmoid_of_half(gb_buf[...] + 0.5 * bx_ref[row, :])
        a_buf[...] = a
        h_buf[...] = mult * gate_i * h_buf[...]

        if n_seq == SUBLANES:
            def scan_step(t, h):
                sl = pl.ds(pl.multiple_of(t * n_seq, n_seq), n_seq)
                h = a_buf[sl, :] * h + h_buf[sl, :]
                h_buf[sl, :] = h
                return h
            h_ref[...] = lax.fori_loop(0, n_t, scan_step, h_ref[...], unroll=True)
        else:
            for t in range(n_t):
                sl = slice(t * n_seq, (t + 1) * n_seq)
                prev = h_ref[...] if t == 0 else h_buf[(t - 1) * n_seq:t * n_seq, :]
                h_buf[sl, :] = a_buf[sl, :] * prev + h_buf[sl, :]
            h_ref[...] = h_buf[(n_t - 1) * n_seq:n_t * n_seq, :]

        ya_buf[...] = (h_buf[...] * _gelu_tanh(proj(1))).astype(BF16)

        vc = chbuf[0:rows, :] * scw_ref[0:1, :]
        for k in range(1, SC_CONV_W):
            vc = vc + chbuf[k * n_seq:k * n_seq + rows, :] * scw_ref[k:k + 1, :]
        yb_buf[...] = (proj(2) * vc).astype(BF16)

        gb_buf[...] = _sigmoid_of_half(proj(6))
        ga_buf[...] = _sigmoid_of_half(proj(5))

        new_rc = xrbuf[rows:rows + rc_rows, :]
        new_sc = chbuf[rows:rows + sc_rows, :]
        xrbuf[0:rc_rows, :] = new_rc
        chbuf[0:sc_rows, :] = new_sc

        @pl.when(step == pl.num_programs(0) - 1)
        def _():
            rc_ref[...] = new_rc
            sc_ref[...] = new_sc

    @pl.when(region_gate)
    def _():
        wait_weights(3)
        m = ga_buf[...] * _dot(ya_buf[...], wba_ref[...])
        m = m + gb_buf[...] * _dot(yb_buf[...], wbb_ref[...])
        xo_ref[...] = read_x() + _dot(m.astype(BF16), wout_ref[...])


_REGION_W_IN_GROUPS = {1: (0, 3, 4), 2: (1, 2, 5, 6)}


def _mixer_kernel(x_ref, zero_ref, *refs, n_seq, n_t, layer):
    refs = list(refs)
    n_w, n_big = len(_MIXER_WEIGHTS), len(_MIXER_BIG_WEIGHTS)
    *refs, sem = refs
    big_vmem = dict(zip(_MIXER_BIG_WEIGHTS, refs[-n_big:]))
    refs = refs[:-n_big]
    states, weights, rest = refs[:3], refs[3:3 + n_w], refs[3 + n_w:]
    big_hbm = {name: ref for name, ref in zip(_MIXER_WEIGHTS, weights) if name in big_vmem}
    weights = [big_vmem.get(name, ref) for name, ref in zip(_MIXER_WEIGHTS, weights)]
    step = pl.program_id(0)

    def region_copies(region):
        if region in _REGION_W_IN_GROUPS:
            pairs = [(big_hbm["w_in"].at[:, pl.ds(k * D_MODEL, D_MODEL)], big_vmem["w_in"].at[:, pl.ds(k * D_MODEL, D_MODEL)])
                     for k in _REGION_W_IN_GROUPS[region]]
        else:
            pairs = [(big_hbm[name], big_vmem[name]) for name in _MIXER_BIG_WEIGHTS[1:]]
        return [pltpu.make_async_copy(src, dst, sem.at[region - 1]) for src, dst in pairs]

    @pl.when(step == 0)
    def _():
        for region in (1, 2, 3):
            for c in region_copies(region):
                c.start()

    def wait_weights(region):
        @pl.when(step == 0)
        def _():
            for c in region_copies(region):
                c.wait()

    _mixer_body(lambda: x_ref[...], step, step + zero_ref[0] >= 0, wait_weights, *states, *weights, *rest,
                n_seq=n_seq, n_t=n_t, layer=layer)


def _seq_major_tile_copies(seq_hbm, tile_buf, sem, tile, slot, *, n_seq, n_t):
    copies = []
    for b in range(n_seq):
        if isinstance(tile, int):
            assert tile == 0
            hbm = seq_hbm.at[b, pl.ds(0, n_t - N_META), :]
            vmem = tile_buf.at[slot, pl.ds(N_META, n_t - N_META), b, :]
        else:
            t0 = pl.multiple_of(tile * n_t - N_META, SUBLANES)
            hbm = seq_hbm.at[b, pl.ds(t0, n_t), :]
            vmem = tile_buf.at[slot, :, b, :]
        copies.append(pltpu.make_async_copy(hbm, vmem, sem.at[slot]))
    return copies


_CAST_CHUNKS = {1: (("w_in", 0), ("w_in", 3), ("w_in", 4)),
                2: (("w_in", 1), ("w_in", 2), ("w_in", 5), ("w_in", 6)),
                3: (("w_branch_a", None), ("w_branch_b", None), ("w_out", None))}
_CAST_ORDER = _CAST_CHUNKS[1] + _CAST_CHUNKS[2] + _CAST_CHUNKS[3]


def _cast_chunk_copy(c, f32_hbm, stage, sem, layer):
    name, group = _CAST_ORDER[c]
    src = f32_hbm[name].at[layer] if group is None else f32_hbm[name].at[layer, :, pl.ds(group * D_MODEL, D_MODEL)]
    return pltpu.make_async_copy(src, stage.at[c % 2], sem.at[c % 2])


def _cast_region_chunks(region, f32_hbm, bf16_vmem, stage, sem, layer):
    first = sum(len(_CAST_CHUNKS[r]) for r in range(1, region))
    for c in range(first, first + len(_CAST_CHUNKS[region])):
        name, group = _CAST_ORDER[c]
        _cast_chunk_copy(c, f32_hbm, stage, sem, layer).wait()
        value = stage[c % 2]
        if group is None:
            bf16_vmem[name][...] = value.astype(BF16)
        else:
            halve = group >= N_SPLITS - 2
            bf16_vmem[name][:, group * D_MODEL:(group + 1) * D_MODEL] = (0.5 * value if halve else value).astype(BF16)
        if c + 2 < len(_CAST_ORDER):
            _cast_chunk_copy(c + 2, f32_hbm, stage, sem, layer).start()


def _mixer_from_seq_major_kernel(xs_hbm, meta_ref, zero_ref, *refs, n_seq, n_t, layer):
    refs = list(refs)
    n_w, n_out, n_cast, n_big = len(_MIXER_WEIGHTS), 4, len(_FFN_BIG_WEIGHTS), len(_MIXER_BIG_WEIGHTS)
    states, weights, cast_src = refs[:3], refs[3:3 + n_w], refs[3 + n_w:3 + n_w + n_cast]
    o = 3 + n_w + n_cast
    outs, cast_dst, big_out = refs[o:o + n_out], refs[o + n_out:o + n_out + n_cast], refs[o + n_out + n_cast:o + n_out + n_cast + n_big]
    sc = o + n_out + n_cast + n_big
    n_body_scratch = len(refs) - sc - 2 - n_big - 3
    body_scratch, (xin, sem) = refs[sc:sc + n_body_scratch], refs[sc + n_body_scratch:sc + n_body_scratch + 2]
    big_vmem = dict(zip(_MIXER_BIG_WEIGHTS, refs[sc + n_body_scratch + 2:sc + n_body_scratch + 2 + n_big]))
    stage, load_sem, store_sem = refs[sc + n_body_scratch + 2 + n_big:]
    big_f32 = {name: ref for name, ref in zip(_MIXER_WEIGHTS, weights) if name in big_vmem}
    weights = [big_vmem.get(name, ref) for name, ref in zip(_MIXER_WEIGHTS, weights)]

    step = pl.program_id(0)
    n_steps = pl.num_programs(0)
    slot = step % 2
    copies = functools.partial(_seq_major_tile_copies, xs_hbm, xin, sem, n_seq=n_seq, n_t=n_t)
    write_back = [pltpu.make_async_copy(big_vmem[name], dst, store_sem.at[0])
                  for name, dst in zip(_MIXER_BIG_WEIGHTS, big_out)]

    def start_all(tile_copies):
        for b, c in enumerate(tile_copies):
            c.start(priority=b % 2)

    @pl.when(step == 0)
    def _():
        start_all(copies(0, 0))
        xin[0, 0:N_META, :, :] = jnp.broadcast_to(meta_ref[...][:, None, :], (N_META, n_seq, D_MODEL))

        @pl.when(n_steps > 1)
        def _():
            start_all(copies(step + 1, 1))
        _cast_chunk_copy(0, big_f32, stage, load_sem, layer).start()
        _cast_chunk_copy(1, big_f32, stage, load_sem, layer).start()
        for c in copies(0, 0):
            c.wait()

    @pl.when(step > 0)
    def _():
        @pl.when(step + 1 < n_steps)
        def _():
            start_all(copies(step + 1, 1 - slot))
        for c in copies(step, slot):
            c.wait()

    def wait_weights(region):
        @pl.when(step == 0)
        def _():
            _cast_region_chunks(region, big_f32, big_vmem, stage, load_sem, layer)
            if region == 3:
                for c in write_back:
                    c.start()

    _mixer_body(lambda: xin[slot].reshape(n_t * n_seq, D_MODEL), step, step + zero_ref[0] >= 0, wait_weights,
                *states, *weights, *outs, *body_scratch, n_seq=n_seq, n_t=n_t, layer=layer)
    _cast_weight_blocks(_FFN_BIG_WEIGHTS, cast_src, cast_dst)

    @pl.when(step == n_steps - 1)
    def _():
        for c in write_back:
            c.wait()


def _ffn_body(x, *, g2_ref, wg_ref, wu_ref, wd_ref, gf_ref, final_norm, layer):
    v = _rmsnorm(x, g2_ref[layer:layer + 1, :]).astype(BF16)
    half = _dot(v, wg_ref[...])
    hid = ((half * jnp.tanh(half) + half) * _dot(v, wu_ref[...])).astype(BF16)
    y = x + _dot(hid, wd_ref[...])
    if final_norm:
        y = _rmsnorm(y, gf_ref[...])
    return y


def _cast_weight_blocks(names, src_refs, dst_refs):
    @pl.when(pl.program_id(0) < N_CONVERT_STEPS)
    def _():
        for name, src, dst in zip(names, src_refs, dst_refs):
            if name == "w_in":
                plain = (N_SPLITS - 2) * D_MODEL
                dst[:, :plain] = src[:, :plain].astype(BF16)
                dst[:, plain:] = (0.5 * src[:, plain:]).astype(BF16)
            elif name == "w_ff_gate":
                dst[...] = (0.5 * src[...]).astype(BF16)
            else:
                dst[...] = src[...].astype(BF16)


def _ffn_kernel(xp_ref, xs_ref, g2_ref, wg_ref, wu_ref, wd_ref, gf_ref, *rest, final_norm, layer, n_prompt_steps):
    n_big = (len(rest) - 2) // 2
    next_f32, (op_ref, os_ref), next_bf16 = rest[:n_big], rest[n_big:n_big + 2], rest[n_big + 2:]
    body = functools.partial(_ffn_body, g2_ref=g2_ref, wg_ref=wg_ref, wu_ref=wu_ref, wd_ref=wd_ref, gf_ref=gf_ref,
                             final_norm=final_norm, layer=layer)
    step = pl.program_id(0)

    @pl.when(step < n_prompt_steps)
    def _():
        op_ref[...] = body(xp_ref[...])

    @pl.when(step == n_prompt_steps)
    def _():
        os_ref[...] = body(xs_ref[...])

    if n_big:
        _cast_weight_blocks(_BIG_WEIGHTS, next_f32, next_bf16)


def _ffn_to_seq_major_kernel(xp_hbm, xs_ref, g2_ref, wg_ref, wu_ref, wd_ref, gf_ref, yp_ref, os_ref, xin, sem,
                             *, n_t, layer, n_prompt_steps):
    step = pl.program_id(0)
    slot = step % 2
    tiles_per_seq = (xp_hbm.shape[0] - N_META) // n_t
    body = functools.partial(_ffn_body, g2_ref=g2_ref, wg_ref=wg_ref, wu_ref=wu_ref, wd_ref=wd_ref, gf_ref=gf_ref,
                             final_norm=True, layer=layer)

    def fetch(tile, into):
        t0 = N_META + (tile % tiles_per_seq) * n_t
        return pltpu.make_async_copy(xp_hbm.at[pl.ds(t0, n_t), tile // tiles_per_seq, :], xin.at[into], sem.at[into])

    @pl.when(step == 0)
    def _():
        fetch(0, 0).start()

    @pl.when(step < n_prompt_steps)
    def _():
        @pl.when(step + 1 < n_prompt_steps)
        def _():
            fetch(step + 1, 1 - slot).start()
        fetch(step, slot).wait()
        yp_ref[...] = body(xin[slot])

    @pl.when(step == n_prompt_steps)
    def _():
        os_ref[...] = body(xs_ref[...])


def _resident(shape, index):
    return pl.BlockSpec(shape, index, pipeline_mode=pl.Buffered(1))


def _weight_operands(names, small, big, layer):
    specs, arrays = [], []
    for name in names:
        if name not in big and name not in small:
            specs.append(None)
            arrays.append(None)
            continue
        arr = big[name] if name in big else small[name]
        if name in big or arr.ndim == 2:
            specs.append(_resident(arr.shape, lambda i, nd=arr.ndim: (0,) * nd))
        else:
            specs.append(_resident((None,) + arr.shape[1:], lambda i, nd=arr.ndim: (layer,) + (0,) * (nd - 1)))
        arrays.append(arr)
    return specs, arrays


_MIXER_WEIGHTS = ("norm1_g", "w_in", "rnn_conv_w", "rnn_conv_b", "w_gate", "gate_a_b", "gate_x_b",
                  "lru_lambda", "w_branch_a", "sc_conv_w", "w_branch_b", "w_out")
_FFN_WEIGHTS = ("norm2_g", "w_ff_gate", "w_ff_up", "w_ff_down")
_BIG_WEIGHTS = ("w_in", "w_branch_a", "w_branch_b", "w_out", "w_ff_gate", "w_ff_up", "w_ff_down")
_MIXER_BIG_WEIGHTS, _FFN_BIG_WEIGHTS = _BIG_WEIGHTS[:4], _BIG_WEIGHTS[4:]
N_CONVERT_STEPS = 16
_COMPILER_PARAMS = pltpu.CompilerParams(dimension_semantics=("arbitrary",), vmem_limit_bytes=VMEM_LIMIT_BYTES)


def _cast_operands(names, stacked_f32, layer, n_steps):
    assert n_steps >= N_CONVERT_STEPS
    block = lambda i: jnp.minimum(i, N_CONVERT_STEPS - 1)
    in_specs, out_specs, out_shapes = [], [], []
    for name in names:
        _, n_in, n_out = stacked_f32[name].shape
        blk = n_in // N_CONVERT_STEPS
        assert blk * N_CONVERT_STEPS == n_in and blk % (2 * SUBLANES) == 0
        in_specs.append(pl.BlockSpec((None, blk, n_out), lambda i: (layer, block(i), 0)))
        out_specs.append(pl.BlockSpec((blk, n_out), lambda i: (block(i), 0)))
        out_shapes.append(jax.ShapeDtypeStruct((n_in, n_out), BF16))
    return in_specs, out_specs, out_shapes, [stacked_f32[name] for name in names]


def _mixer_call(x, h0, rc0, sc0, small, big, layer, *, n_seq, n_t, meta=None, big_f32=None):
    rows = n_seq * n_t
    rc_rows = (RNN_CONV_W - 1) * n_seq
    sc_rows = (SC_CONV_W - 1) * n_seq
    row_spec = pl.BlockSpec((rows, D_MODEL), lambda i: (i, 0))
    state_layer = layer if h0.shape[0] > 1 else 0
    state_spec = lambda r: _resident((None, r, D_MODEL), lambda i: (state_layer, 0, 0))
    scratch = [pltpu.VMEM((rows, D_MODEL), BF16),
               pltpu.VMEM((rc_rows + rows, D_MODEL), F32),
               pltpu.VMEM((sc_rows + rows, D_MODEL), F32),
               pltpu.VMEM((rows, D_MODEL), F32),
               pltpu.VMEM((rows, D_MODEL), F32),
               pltpu.VMEM((rows, D_MODEL), BF16),
               pltpu.VMEM((rows, D_MODEL), BF16),
               pltpu.VMEM((rows, D_MODEL), F32),
               pltpu.VMEM((rows, D_MODEL), F32)]
    w_specs, w_arrays = _weight_operands(_MIXER_WEIGHTS, small, big, layer)
    zero, smem_spec = jnp.zeros((1,), jnp.int32), pl.BlockSpec(memory_space=pltpu.SMEM)
    if meta is None:
        body, x_args, n_rows = _mixer_kernel, (x, zero), x.shape[0]
        x_specs = [row_spec, smem_spec]
        for name in _MIXER_BIG_WEIGHTS:
            w_specs[_MIXER_WEIGHTS.index(name)] = pl.BlockSpec(memory_space=pl.ANY)
            scratch.append(pltpu.VMEM(big[name].shape, BF16))
        scratch.append(pltpu.SemaphoreType.DMA((3,)))
    else:
        assert n_t % SUBLANES == 0 and n_t > N_META and n_seq == SUBLANES
        body, x_args, n_rows = _mixer_from_seq_major_kernel, (x, meta, zero), (x.shape[1] + N_META) * n_seq
        x_specs = [pl.BlockSpec(memory_space=pl.ANY), _resident(meta.shape, lambda i: (0, 0)), smem_spec]
        scratch += [pltpu.VMEM((2, n_t, n_seq, D_MODEL), F32), pltpu.SemaphoreType.DMA((2,))]
    assert n_rows % rows == 0
    cast_in, cast_out, cast_shapes, cast_arrays = [], [], [], []
    if meta is not None:
        cast_in, cast_out, cast_shapes, cast_arrays = _cast_operands(_FFN_BIG_WEIGHTS, big_f32, layer, n_rows // rows)
        for name in _MIXER_BIG_WEIGHTS:
            i, shape = _MIXER_WEIGHTS.index(name), big_f32[name].shape[1:]
            w_specs[i], w_arrays[i] = pl.BlockSpec(memory_space=pl.ANY), big_f32[name]
            cast_out.append(pl.BlockSpec(memory_space=pl.ANY))
            cast_shapes.append(jax.ShapeDtypeStruct(shape, BF16))
            scratch.append(pltpu.VMEM(shape, BF16))
        scratch += [pltpu.VMEM((2, D_MODEL, D_MODEL), F32), pltpu.SemaphoreType.DMA((2,)), pltpu.SemaphoreType.DMA((1,))]
    outs = pl.pallas_call(
        functools.partial(body, n_seq=n_seq, n_t=n_t, layer=layer),
        grid=(n_rows // rows,),
        in_specs=x_specs + [state_spec(n_seq), state_spec(rc_rows), state_spec(sc_rows)] + w_specs + cast_in,
        out_specs=[row_spec,
                   pl.BlockSpec((n_seq, D_MODEL), lambda i: (0, 0)),
                   pl.BlockSpec((rc_rows, D_MODEL), lambda i: (0, 0)),
                   pl.BlockSpec((sc_rows, D_MODEL), lambda i: (0, 0))] + cast_out,
        out_shape=[jax.ShapeDtypeStruct((n_rows, D_MODEL), F32),
                   jax.ShapeDtypeStruct((n_seq, D_MODEL), F32),
                   jax.ShapeDtypeStruct((rc_rows, D_MODEL), F32),
                   jax.ShapeDtypeStruct((sc_rows, D_MODEL), F32)] + cast_shapes,
        scratch_shapes=scratch,
        compiler_params=_COMPILER_PARAMS,
        name=f"mixer_s{n_seq}" + ("_in" if meta is not None else ""),
    )(*x_args, h0, rc0, sc0, *w_arrays, *cast_arrays)
    if meta is None:
        return outs
    n_ffn = len(_FFN_BIG_WEIGHTS)
    return (*outs[:4], dict(zip(_FFN_BIG_WEIGHTS, outs[4:4 + n_ffn])), dict(zip(_MIXER_BIG_WEIGHTS, outs[4 + n_ffn:])))


def _ffn_call(xp, xs, small, big, layer, *, final_norm, rows=None, seq_major_out=None, next_f32=None):
    n_rows = xp.shape[0]
    sample_spec = pl.BlockSpec(xs.shape, lambda i: (0, 0))
    gf = small["final_norm_g"]
    w_specs, w_arrays = _weight_operands(_FFN_WEIGHTS, small, big, layer)
    in_specs = [None, _resident(xs.shape, lambda i: (0, 0))] + w_specs + [_resident(gf.shape, lambda i: (0, 0))]
    args = (xp, xs, *w_arrays, gf)
    sample_shape = jax.ShapeDtypeStruct(xs.shape, F32)
    if seq_major_out is not None:
        n_seq, n_t = seq_major_out
        n_time = n_rows // n_seq
        assert final_norm and next_f32 is None and (n_time - N_META) % n_t == 0
        tiles_per_seq = (n_time - N_META) // n_t
        n_prompt_steps = n_seq * tiles_per_seq
        last = n_prompt_steps - 1
        y_spec = pl.BlockSpec((None, n_t, D_MODEL), lambda i: (jnp.minimum(i, last) // tiles_per_seq,
                                                               jnp.minimum(i, last) % tiles_per_seq, 0))
        in_specs[0] = pl.BlockSpec(memory_space=pl.ANY)
        return pl.pallas_call(
            functools.partial(_ffn_to_seq_major_kernel, n_t=n_t, layer=layer, n_prompt_steps=n_prompt_steps),
            grid=(n_prompt_steps + 1,), in_specs=in_specs, out_specs=[y_spec, sample_spec],
            out_shape=[jax.ShapeDtypeStruct((n_seq, n_time - N_META, D_MODEL), F32), sample_shape],
            scratch_shapes=[pltpu.VMEM((2, n_t, D_MODEL), F32), pltpu.SemaphoreType.DMA((2,))],
            compiler_params=_COMPILER_PARAMS, name=f"ffn_t{n_t}_out",
        )(xp.reshape(n_time, n_seq, D_MODEL), *args[1:])
    assert n_rows % rows == 0
    n_prompt_steps = n_rows // rows
    prompt_spec = pl.BlockSpec((rows, D_MODEL), lambda i: (jnp.minimum(i, n_prompt_steps - 1), 0))
    in_specs[0] = prompt_spec
    out_specs, out_shape = [prompt_spec, sample_spec], [jax.ShapeDtypeStruct((n_rows, D_MODEL), F32), sample_shape]
    if next_f32 is not None:
        cast_in, cast_out, cast_shapes, cast_arrays = _cast_operands(_BIG_WEIGHTS, next_f32, layer + 1,
                                                                     n_prompt_steps)
        in_specs, out_specs, out_shape = in_specs + cast_in, out_specs + cast_out, out_shape + cast_shapes
        args += tuple(cast_arrays)
    outs = pl.pallas_call(
        functools.partial(_ffn_kernel, final_norm=final_norm, layer=layer, n_prompt_steps=n_prompt_steps),
        grid=(n_prompt_steps + 1,), in_specs=in_specs, out_specs=out_specs, out_shape=out_shape,
        compiler_params=_COMPILER_PARAMS,
        name=f"ffn_r{rows}" + ("_cast" if next_f32 is not None else ""),
    )(*args)
    if next_f32 is None:
        return outs
    return outs[0], outs[1], dict(zip(_BIG_WEIGHTS, outs[2:]))


def _block_diag_gates(gate_a_w, gate_x_w):
    hpc = GATE_CHUNK // RNN_HEAD_DIM

    def bd(wt):
        wt = wt.reshape(DEPTH, N_GATE_CHUNKS, hpc, RNN_HEAD_DIM, RNN_HEAD_DIM)
        bands = [jnp.pad(wt[:, :, p], ((0, 0), (0, 0), (0, 0), (p * RNN_HEAD_DIM, (hpc - 1 - p) * RNN_HEAD_DIM)))
                 for p in range(hpc)]
        return jnp.stack(bands, axis=2).reshape(DEPTH, N_GATE_CHUNKS, GATE_CHUNK, GATE_CHUNK)

    return jnp.concatenate([bd(gate_a_w), bd(gate_x_w)], axis=-1).astype(BF16)


def _time_major(states):
    return jnp.swapaxes(states, 1, 2).reshape(states.shape[0], -1, D_MODEL)


def _seq_major(states, n_seq):
    return jnp.swapaxes(jnp.stack(states).reshape(len(states), -1, n_seq, D_MODEL), 1, 2)


PROMPT_N_T = 86
PROMPT_FFN_ROWS = 688
PROMPT_IO_N_T = 48
PROMPT_OUT_N_T = 1024


def kernel(x_prompt, x_sample, state_rnn_h, state_rnn_conv, state_sc_conv, meta_tokens, norm1_g, w_in, rnn_conv_w, rnn_conv_b, gate_a_w, gate_a_b, gate_x_w, gate_x_b, lru_lambda, w_branch_a, sc_conv_w, w_branch_b, w_out, norm2_g, w_ff_gate, w_ff_up, w_ff_down, final_norm_g):
    small = {
        "norm1_g": norm1_g, "rnn_conv_w": rnn_conv_w, "rnn_conv_b": rnn_conv_b,
        "w_gate": _block_diag_gates(0.5 * gate_a_w, 0.5 * gate_x_w),
        "gate_a_b": gate_a_b, "gate_x_b": gate_x_b, "lru_lambda": lru_lambda,
        "sc_conv_w": sc_conv_w, "norm2_g": norm2_g, "final_norm_g": final_norm_g.reshape(1, D_MODEL),
    }
    big_f32 = {"w_in": w_in, "w_branch_a": w_branch_a, "w_branch_b": w_branch_b, "w_out": w_out,
               "w_ff_gate": w_ff_gate, "w_ff_up": w_ff_up, "w_ff_down": w_ff_down}
    big = [{}]
    dt = x_prompt.dtype

    bp = x_prompt.shape[0]
    bs, t_s, _ = x_sample.shape
    zeros = lambda k: jnp.zeros((1, k * bp, D_MODEL), dt)
    prompt_states = (zeros(1), zeros(RNN_CONV_W - 1), zeros(SC_CONV_W - 1))
    sample_states = (state_rnn_h, _time_major(state_rnn_conv), _time_major(state_sc_conv))
    xp = x_prompt
    xs = jnp.swapaxes(x_sample, 0, 1).reshape(t_s * bs, D_MODEL)
    new_p, new_s = [], []
    for layer in range(DEPTH):
        if layer == 0:
            xp, *st_p, ffn_big, mixer_big = _mixer_call(xp, *prompt_states, small, big[layer], layer, n_seq=bp,
                                                        n_t=PROMPT_IO_N_T, meta=meta_tokens.astype(dt), big_f32=big_f32)
            big[layer] = {**mixer_big, **ffn_big}
        else:
            xp, *st_p = _mixer_call(xp, *prompt_states, small, big[layer], layer, n_seq=bp, n_t=PROMPT_N_T)
        xs, *st_s = _mixer_call(xs, *sample_states, small, big[layer], layer, n_seq=bs, n_t=t_s)
        new_p.append(st_p)
        new_s.append(st_s)
        if layer == DEPTH - 1:
            y_prompt, ys = _ffn_call(xp, xs, small, big[layer], layer, final_norm=True,
                                     seq_major_out=(bp, PROMPT_OUT_N_T))
        else:
            xp, xs, next_big = _ffn_call(xp, xs, small, big[layer], layer, rows=PROMPT_FFN_ROWS, final_norm=False,
                                         next_f32=big_f32)
            big.append(next_big)
    y_sample = jnp.swapaxes(ys.reshape(t_s, bs, D_MODEL), 0, 1)

    def collect(states, n_seq):
        hs, rcs, scs = zip(*states)
        return jnp.stack(hs), _seq_major(list(rcs), n_seq), _seq_major(list(scs), n_seq)

    rnn_h_p, rnn_conv_p, sc_conv_p = collect(new_p, bp)
    rnn_h_s, rnn_conv_s, sc_conv_s = collect(new_s, bs)

    return (y_prompt, y_sample, rnn_h_p, rnn_conv_p, sc_conv_p, rnn_h_s, rnn_conv_s, sc_conv_s)
```

```python
import functools
import math

import jax
import jax.numpy as jnp
from jax import lax
from jax.experimental import pallas as pl
from jax.experimental.pallas import tpu as pltpu

D_MODEL = 1024
DEPTH = 4
N_META = 16
N_RNN_HEADS = 16
RNN_HEAD_DIM = D_MODEL // N_RNN_HEADS
RNN_CONV_W = 4
SC_CONV_W = 3
LRU_C = 8.0
EPS = 1e-6
N_SPLITS = 7
GATE_CHUNK = 256
N_GATE_CHUNKS = D_MODEL // GATE_CHUNK
SUBLANES = 8

V7X_VMEM_BYTES = 64 * 1024 * 1024
VMEM_LIMIT_BYTES = V7X_VMEM_BYTES - 2 * 1024 * 1024

F32 = jnp.float32
BF16 = jnp.bfloat16


def _sigmoid_of_half(half_x):
    return 0.5 * jnp.tanh(half_x) + 0.5


def _gelu_tanh(x):
    c = math.sqrt(2.0 / math.pi)
    half_x = 0.5 * x
    return half_x * jnp.tanh(x * (c + (c * 0.044715) * (x * x))) + half_x


def _softplus(x):
    return jnp.maximum(x, 0.0) + jnp.log1p(jnp.exp(-jnp.abs(x)))


def _rmsnorm(x, g):
    ms = jnp.mean(x * x, axis=-1, keepdims=True)
    return x * lax.rsqrt(ms + EPS) * g


def _dot(a, b):
    return jnp.dot(a, b, preferred_element_type=F32)


def _mixer_body(read_x, step, region_gate, wait_weights, h0_ref, rc0_ref, sc0_ref, g1_ref, win_ref, cw_ref, cb_ref, wgate_ref,
                ba_ref, bx_ref, lam_ref, wba_ref, scw_ref, wbb_ref, wout_ref,
                xo_ref, h_ref, rc_ref, sc_ref,
                u_bf, xrbuf, chbuf, a_buf, h_buf, ya_buf, yb_buf, ga_buf, gb_buf, *, n_seq, n_t, layer):
    rows = n_seq * n_t
    rc_rows = (RNN_CONV_W - 1) * n_seq
    sc_rows = (SC_CONV_W - 1) * n_seq
    row = slice(layer, layer + 1)

    def proj(k):
        return _dot(u_bf[...], win_ref[:, k * D_MODEL:(k + 1) * D_MODEL])

    wait_weights(1)

    @pl.when(step == 0)
    def _():
        h_ref[...] = h0_ref[...]
        xrbuf[0:rc_rows, :] = rc0_ref[...]
        chbuf[0:sc_rows, :] = sc0_ref[...]

    u_bf[...] = _rmsnorm(read_x(), g1_ref[row, :]).astype(BF16)
    xrbuf[rc_rows:rc_rows + rows, :] = proj(0)
    chbuf[sc_rows:sc_rows + rows, :] = proj(3) * proj(4)
    xc = cb_ref[row, :] + xrbuf[0:rows, :] * cw_ref[0:1, :]
    for k in range(1, RNN_CONV_W):
        xc = xc + xrbuf[k * n_seq:k * n_seq + rows, :] * cw_ref[k:k + 1, :]
    h_buf[...] = xc
    xc_b = xc.astype(BF16)
    for j in range(N_GATE_CHUNKS):
        cols = slice(j * GATE_CHUNK, (j + 1) * GATE_CHUNK)
        gates = _dot(xc_b[:, cols], wgate_ref[j])
        ga_buf[:, cols] = gates[:, :GATE_CHUNK]
        gb_buf[:, cols] = gates[:, GATE_CHUNK:]

    @pl.when(region_gate)
    def _():
        wait_weights(2)
        half_log_a_max = (-0.5 * LRU_C) * _softplus(-lam_ref[row, :])
        t_r = jnp.tanh(ga_buf[...] + 0.5 * ba_ref[row, :])
        log_a = t_r * half_log_a_max + half_log_a_max
        a = jnp.exp(log_a)
        one_minus_a2 = jnp.maximum(jnp.tanh(log_a) * (-1.0 - a * a), 1e-12)
        mult = one_minus_a2 * lax.rsqrt(one_minus_a2)
        gate_i = _sigmoid_of_half(gb_buf[...] + 0.5 * bx_ref[row, :])
        a_buf[...] = a
        h_buf[...] = mult * gate_i * h_buf[...]

        if n_seq == SUBLANES:
            def scan_step(t, h):
                sl = pl.ds(pl.multiple_of(t * n_seq, n_seq), n_seq)
                h = a_buf[sl, :] * h + h_buf[sl, :]
                h_buf[sl, :] = h
                return h
            h_ref[...] = lax.fori_loop(0, n_t, scan_step, h_ref[...], unroll=True)
        else:
            for t in range(n_t):
                sl = slice(t * n_seq, (t + 1) * n_seq)
                prev = h_ref[...] if t == 0 else h_buf[(t - 1) * n_seq:t * n_seq, :]
                h_buf[sl, :] = a_buf[sl, :] * prev + h_buf[sl, :]
            h_ref[...] = h_buf[(n_t - 1) * n_seq:n_t * n_seq, :]

        ya_buf[...] = (h_buf[...] * _gelu_tanh(proj(1))).astype(BF16)

        vc = chbuf[0:rows, :] * scw_ref[0:1, :]
        for k in range(1, SC_CONV_W):
            vc = vc + chbuf[k * n_seq:k * n_seq + rows, :] * scw_ref[k:k + 1, :]
        yb_buf[...] = (proj(2) * vc).astype(BF16)

        gb_buf[...] = _sigmoid_of_half(proj(6))
        ga_buf[...] = _sigmoid_of_half(proj(5))

        new_rc = xrbuf[rows:rows + rc_rows, :]
        new_sc = chbuf[rows:rows + sc_rows, :]
        xrbuf[0:rc_rows, :] = new_rc
        chbuf[0:sc_rows, :] = new_sc

        @pl.when(step == pl.num_programs(0) - 1)
        def _():
            rc_ref[...] = new_rc
            sc_ref[...] = new_sc

    @pl.when(region_gate)
    def _():
        wait_weights(3)
        m = ga_buf[...] * _dot(ya_buf[...], wba_ref[...])
        m = m + gb_buf[...] * _dot(yb_buf[...], wbb_ref[...])
        xo_ref[...] = read_x() + _dot(m.astype(BF16), wout_ref[...])


_REGION_W_IN_GROUPS = {1: (0, 3, 4), 2: (1, 2, 5, 6)}


def _mixer_kernel(x_ref, zero_ref, *refs, n_seq, n_t, layer):
    refs = list(refs)
    n_w, n_big = len(_MIXER_WEIGHTS), len(_MIXER_BIG_WEIGHTS)
    *refs, sem = refs
    big_vmem = dict(zip(_MIXER_BIG_WEIGHTS, refs[-n_big:]))
    refs = refs[:-n_big]
    states, weights, rest = refs[:3], refs[3:3 + n_w], refs[3 + n_w:]
    big_hbm = {name: ref for name, ref in zip(_MIXER_WEIGHTS, weights) if name in big_vmem}
    weights = [big_vmem.get(name, ref) for name, ref in zip(_MIXER_WEIGHTS, weights)]
    step = pl.program_id(0)

    def region_copies(region):
        if region in _REGION_W_IN_GROUPS:
            pairs = [(big_hbm["w_in"].at[:, pl.ds(k * D_MODEL, D_MODEL)], big_vmem["w_in"].at[:, pl.ds(k * D_MODEL, D_MODEL)])
                     for k in _REGION_W_IN_GROUPS[region]]
        else:
            pairs = [(big_hbm[name], big_vmem[name]) for name in _MIXER_BIG_WEIGHTS[1:]]
        return [pltpu.make_async_copy(src, dst, sem.at[region - 1]) for src, dst in pairs]

    @pl.when(step == 0)
    def _():
        for region in (1, 2, 3):
            for c in region_copies(region):
                c.start()

    def wait_weights(region):
        @pl.when(step == 0)
        def _():
            for c in region_copies(region):
                c.wait()

    _mixer_body(lambda: x_ref[...], step, step + zero_ref[0] >= 0, wait_weights, *states, *weights, *rest,
                n_seq=n_seq, n_t=n_t, layer=layer)


def _seq_major_tile_copies(seq_hbm, tile_buf, sem, tile, slot, *, n_seq, n_t):
    copies = []
    for b in range(n_seq):
        if isinstance(tile, int):
            assert tile == 0
            hbm = seq_hbm.at[b, pl.ds(0, n_t - N_META), :]
            vmem = tile_buf.at[slot, pl.ds(N_META, n_t - N_META), b, :]
        else:
            t0 = pl.multiple_of(tile * n_t - N_META, SUBLANES)
            hbm = seq_hbm.at[b, pl.ds(t0, n_t), :]
            vmem = tile_buf.at[slot, :, b, :]
        copies.append(pltpu.make_async_copy(hbm, vmem, sem.at[slot]))
    return copies


_CAST_CHUNKS = {1: (("w_in", 0), ("w_in", 3), ("w_in", 4)),
                2: (("w_in", 1), ("w_in", 2), ("w_in", 5), ("w_in", 6)),
                3: (("w_branch_a", None), ("w_branch_b", None), ("w_out", None))}
_CAST_ORDER = _CAST_CHUNKS[1] + _CAST_CHUNKS[2] + _CAST_CHUNKS[3]


def _cast_chunk_copy(c, f32_hbm, stage, sem, layer):
    name, group = _CAST_ORDER[c]
    src = f32_hbm[name].at[layer] if group is None else f32_hbm[name].at[layer, :, pl.ds(group * D_MODEL, D_MODEL)]
    return pltpu.make_async_copy(src, stage.at[c % 2], sem.at[c % 2])


def _cast_region_chunks(region, f32_hbm, bf16_vmem, stage, sem, layer):
    first = sum(len(_CAST_CHUNKS[r]) for r in range(1, region))
    for c in range(first, first + len(_CAST_CHUNKS[region])):
        name, group = _CAST_ORDER[c]
        _cast_chunk_copy(c, f32_hbm, stage, sem, layer).wait()
        value = stage[c % 2]
        if group is None:
            bf16_vmem[name][...] = value.astype(BF16)
        else:
            halve = group >= N_SPLITS - 2
            bf16_vmem[name][:, group * D_MODEL:(group + 1) * D_MODEL] = (0.5 * value if halve else value).astype(BF16)
        if c + 2 < len(_CAST_ORDER):
            _cast_chunk_copy(c + 2, f32_hbm, stage, sem, layer).start()


def _mixer_from_seq_major_kernel(xs_hbm, meta_ref, zero_ref, *refs, n_seq, n_t, layer):
    refs = list(refs)
    n_w, n_out, n_cast, n_big = len(_MIXER_WEIGHTS), 4, len(_FFN_BIG_WEIGHTS), len(_MIXER_BIG_WEIGHTS)
    states, weights, cast_src = refs[:3], refs[3:3 + n_w], refs[3 + n_w:3 + n_w + n_cast]
    o = 3 + n_w + n_cast
    outs, cast_dst, big_out = refs[o:o + n_out], refs[o + n_out:o + n_out + n_cast], refs[o + n_out + n_cast:o + n_out + n_cast + n_big]
    sc = o + n_out + n_cast + n_big
    n_body_scratch = len(refs) - sc - 2 - n_big - 3
    body_scratch, (xin, sem) = refs[sc:sc + n_body_scratch], refs[sc + n_body_scratch:sc + n_body_scratch + 2]
    big_vmem = dict(zip(_MIXER_BIG_WEIGHTS, refs[sc + n_body_scratch + 2:sc + n_body_scratch + 2 + n_big]))
    stage, load_sem, store_sem = refs[sc + n_body_scratch + 2 + n_big:]
    big_f32 = {name: ref for name, ref in zip(_MIXER_WEIGHTS, weights) if name in big_vmem}
    weights = [big_vmem.get(name, ref) for name, ref in zip(_MIXER_WEIGHTS, weights)]

    step = pl.program_id(0)
    n_steps = pl.num_programs(0)
    slot = step % 2
    copies = functools.partial(_seq_major_tile_copies, xs_hbm, xin, sem, n_seq=n_seq, n_t=n_t)
    write_back = [pltpu.make_async_copy(big_vmem[name], dst, store_sem.at[0])
                  for name, dst in zip(_MIXER_BIG_WEIGHTS, big_out)]

    @pl.when(step == 0)
    def _():
        for c in copies(0, 0):
            c.start()
        xin[0, 0:N_META, :, :] = jnp.broadcast_to(meta_ref[...][:, None, :], (N_META, n_seq, D_MODEL))

        @pl.when(n_steps > 1)
        def _():
            for c in copies(step + 1, 1):
                c.start()
        _cast_chunk_copy(0, big_f32, stage, load_sem, layer).start()
        _cast_chunk_copy(1, big_f32, stage, load_sem, layer).start()
        for c in copies(0, 0):
            c.wait()

    @pl.when(step > 0)
    def _():
        @pl.when(step + 1 < n_steps)
        def _():
            for c in copies(step + 1, 1 - slot):
                c.start()
        for c in copies(step, slot):
            c.wait()

    def wait_weights(region):
        @pl.when(step == 0)
        def _():
            _cast_region_chunks(region, big_f32, big_vmem, stage, load_sem, layer)
            if region == 3:
                for c in write_back:
                    c.start()

    _mixer_body(lambda: xin[slot].reshape(n_t * n_seq, D_MODEL), step, step + zero_ref[0] >= 0, wait_weights,
                *states, *weights, *outs, *body_scratch, n_seq=n_seq, n_t=n_t, layer=layer)
    _cast_weight_blocks(_FFN_BIG_WEIGHTS, cast_src, cast_dst)

    @pl.when(step == n_steps - 1)
    def _():
        for c in write_back:
            c.wait()


def _ffn_body(x, *, g2_ref, wg_ref, wu_ref, wd_ref, gf_ref, final_norm, layer):
    v = _rmsnorm(x, g2_ref[layer:layer + 1, :]).astype(BF16)
    half = _dot(v, wg_ref[...])
    hid = ((half * jnp.tanh(half) + half) * _dot(v, wu_ref[...])).astype(BF16)
    y = x + _dot(hid, wd_ref[...])
    if final_norm:
        y = _rmsnorm(y, gf_ref[...])
    return y


def _cast_weight_blocks(names, src_refs, dst_refs):
    @pl.when(pl.program_id(0) < N_CONVERT_STEPS)
    def _():
        for name, src, dst in zip(names, src_refs, dst_refs):
            if name == "w_in":
                plain = (N_SPLITS - 2) * D_MODEL
                dst[:, :plain] = src[:, :plain].astype(BF16)
                dst[:, plain:] = (0.5 * src[:, plain:]).astype(BF16)
            elif name == "w_ff_gate":
                dst[...] = (0.5 * src[...]).astype(BF16)
            else:
                dst[...] = src[...].astype(BF16)


def _ffn_kernel(xp_ref, xs_ref, g2_ref, wg_ref, wu_ref, wd_ref, gf_ref, *rest, final_norm, layer, n_prompt_steps):
    n_big = (len(rest) - 2) // 2
    next_f32, (op_ref, os_ref), next_bf16 = rest[:n_big], rest[n_big:n_big + 2], rest[n_big + 2:]
    body = functools.partial(_ffn_body, g2_ref=g2_ref, wg_ref=wg_ref, wu_ref=wu_ref, wd_ref=wd_ref, gf_ref=gf_ref,
                             final_norm=final_norm, layer=layer)
    step = pl.program_id(0)

    @pl.when(step < n_prompt_steps)
    def _():
        op_ref[...] = body(xp_ref[...])

    @pl.when(step == n_prompt_steps)
    def _():
        os_ref[...] = body(xs_ref[...])

    if n_big:
        _cast_weight_blocks(_BIG_WEIGHTS, next_f32, next_bf16)


def _ffn_to_seq_major_kernel(xp_hbm, xs_ref, g2_ref, wg_ref, wu_ref, wd_ref, gf_ref, yp_ref, os_ref, xin, sem,
                             *, n_t, layer, n_prompt_steps):
    step = pl.program_id(0)
    slot = step % 2
    tiles_per_seq = (xp_hbm.shape[0] - N_META) // n_t
    body = functools.partial(_ffn_body, g2_ref=g2_ref, wg_ref=wg_ref, wu_ref=wu_ref, wd_ref=wd_ref, gf_ref=gf_ref,
                             final_norm=True, layer=layer)

    def fetch(tile, into):
        t0 = N_META + (tile % tiles_per_seq) * n_t
        return pltpu.make_async_copy(xp_hbm.at[pl.ds(t0, n_t), tile // tiles_per_seq, :], xin.at[into], sem.at[into])

    @pl.when(step == 0)
    def _():
        fetch(0, 0).start()

    @pl.when(step < n_prompt_steps)
    def _():
        @pl.when(step + 1 < n_prompt_steps)
        def _():
            fetch(step + 1, 1 - slot).start()
        fetch(step, slot).wait()
        yp_ref[...] = body(xin[slot])

    @pl.when(step == n_prompt_steps)
    def _():
        os_ref[...] = body(xs_ref[...])


def _resident(shape, index):
    return pl.BlockSpec(shape, index, pipeline_mode=pl.Buffered(1))


def _weight_operands(names, small, big, layer):
    specs, arrays = [], []
    for name in names:
        if name not in big and name not in small:
            specs.append(None)
            arrays.append(None)
            continue
        arr = big[name] if name in big else small[name]
        if name in big or arr.ndim == 2:
            specs.append(_resident(arr.shape, lambda i, nd=arr.ndim: (0,) * nd))
        else:
            specs.append(_resident((None,) + arr.shape[1:], lambda i, nd=arr.ndim: (layer,) + (0,) * (nd - 1)))
        arrays.append(arr)
    return specs, arrays


_MIXER_WEIGHTS = ("norm1_g", "w_in", "rnn_conv_w", "rnn_conv_b", "w_gate", "gate_a_b", "gate_x_b",
                  "lru_lambda", "w_branch_a", "sc_conv_w", "w_branch_b", "w_out")
_FFN_WEIGHTS = ("norm2_g", "w_ff_gate", "w_ff_up", "w_ff_down")
_BIG_WEIGHTS = ("w_in", "w_branch_a", "w_branch_b", "w_out", "w_ff_gate", "w_ff_up", "w_ff_down")
_MIXER_BIG_WEIGHTS, _FFN_BIG_WEIGHTS = _BIG_WEIGHTS[:4], _BIG_WEIGHTS[4:]
N_CONVERT_STEPS = 16
_COMPILER_PARAMS = pltpu.CompilerParams(dimension_semantics=("arbitrary",), vmem_limit_bytes=VMEM_LIMIT_BYTES)


def _cast_operands(names, stacked_f32, layer, n_steps):
    assert n_steps >= N_CONVERT_STEPS
    block = lambda i: jnp.minimum(i, N_CONVERT_STEPS - 1)
    in_specs, out_specs, out_shapes = [], [], []
    for name in names:
        _, n_in, n_out = stacked_f32[name].shape
        blk = n_in // N_CONVERT_STEPS
        assert blk * N_CONVERT_STEPS == n_in and blk % (2 * SUBLANES) == 0
        in_specs.append(pl.BlockSpec((None, blk, n_out), lambda i: (layer, block(i), 0)))
        out_specs.append(pl.BlockSpec((blk, n_out), lambda i: (block(i), 0)))
        out_shapes.append(jax.ShapeDtypeStruct((n_in, n_out), BF16))
    return in_specs, out_specs, out_shapes, [stacked_f32[name] for name in names]


def _mixer_call(x, h0, rc0, sc0, small, big, layer, *, n_seq, n_t, meta=None, big_f32=None):
    rows = n_seq * n_t
    rc_rows = (RNN_CONV_W - 1) * n_seq
    sc_rows = (SC_CONV_W - 1) * n_seq
    row_spec = pl.BlockSpec((rows, D_MODEL), lambda i: (i, 0))
    state_layer = layer if h0.shape[0] > 1 else 0
    state_spec = lambda r: _resident((None, r, D_MODEL), lambda i: (state_layer, 0, 0))
    scratch = [pltpu.VMEM((rows, D_MODEL), BF16),
               pltpu.VMEM((rc_rows + rows, D_MODEL), F32),
               pltpu.VMEM((sc_rows + rows, D_MODEL), F32),
               pltpu.VMEM((rows, D_MODEL), F32),
               pltpu.VMEM((rows, D_MODEL), F32),
               pltpu.VMEM((rows, D_MODEL), BF16),
               pltpu.VMEM((rows, D_MODEL), BF16),
               pltpu.VMEM((rows, D_MODEL), F32),
               pltpu.VMEM((rows, D_MODEL), F32)]
    w_specs, w_arrays = _weight_operands(_MIXER_WEIGHTS, small, big, layer)
    zero, smem_spec = jnp.zeros((1,), jnp.int32), pl.BlockSpec(memory_space=pltpu.SMEM)
    if meta is None:
        body, x_args, n_rows = _mixer_kernel, (x, zero), x.shape[0]
        x_specs = [row_spec, smem_spec]
        for name in _MIXER_BIG_WEIGHTS:
            w_specs[_MIXER_WEIGHTS.index(name)] = pl.BlockSpec(memory_space=pl.ANY)
            scratch.append(pltpu.VMEM(big[name].shape, BF16))
        scratch.append(pltpu.SemaphoreType.DMA((3,)))
    else:
        assert n_t % SUBLANES == 0 and n_t > N_META and n_seq == SUBLANES
        body, x_args, n_rows = _mixer_from_seq_major_kernel, (x, meta, zero), (x.shape[1] + N_META) * n_seq
        x_specs = [pl.BlockSpec(memory_space=pl.ANY), _resident(meta.shape, lambda i: (0, 0)), smem_spec]
        scratch += [pltpu.VMEM((2, n_t, n_seq, D_MODEL), F32), pltpu.SemaphoreType.DMA((2,))]
    assert n_rows % rows == 0
    cast_in, cast_out, cast_shapes, cast_arrays = [], [], [], []
    if meta is not None:
        cast_in, cast_out, cast_shapes, cast_arrays = _cast_operands(_FFN_BIG_WEIGHTS, big_f32, layer, n_rows // rows)
        for name in _MIXER_BIG_WEIGHTS:
            i, shape = _MIXER_WEIGHTS.index(name), big_f32[name].shape[1:]
            w_specs[i], w_arrays[i] = pl.BlockSpec(memory_space=pl.ANY), big_f32[name]
            cast_out.append(pl.BlockSpec(memory_space=pl.ANY))
            cast_shapes.append(jax.ShapeDtypeStruct(shape, BF16))
            scratch.append(pltpu.VMEM(shape, BF16))
        scratch += [pltpu.VMEM((2, D_MODEL, D_MODEL), F32), pltpu.SemaphoreType.DMA((2,)), pltpu.SemaphoreType.DMA((1,))]
    outs = pl.pallas_call(
        functools.partial(body, n_seq=n_seq, n_t=n_t, layer=layer),
        grid=(n_rows // rows,),
        in_specs=x_specs + [state_spec(n_seq), state_spec(rc_rows), state_spec(sc_rows)] + w_specs + cast_in,
        out_specs=[row_spec,
                   pl.BlockSpec((n_seq, D_MODEL), lambda i: (0, 0)),
                   pl.BlockSpec((rc_rows, D_MODEL), lambda i: (0, 0)),
                   pl.BlockSpec((sc_rows, D_MODEL), lambda i: (0, 0))] + cast_out,
        out_shape=[jax.ShapeDtypeStruct((n_rows, D_MODEL), F32),
                   jax.ShapeDtypeStruct((n_seq, D_MODEL), F32),
                   jax.ShapeDtypeStruct((rc_rows, D_MODEL), F32),
                   jax.ShapeDtypeStruct((sc_rows, D_MODEL), F32)] + cast_shapes,
        scratch_shapes=scratch,
        compiler_params=_COMPILER_PARAMS,
        name=f"mixer_s{n_seq}" + ("_in" if meta is not None else ""),
    )(*x_args, h0, rc0, sc0, *w_arrays, *cast_arrays)
    if meta is None:
        return outs
    n_ffn = len(_FFN_BIG_WEIGHTS)
    return (*outs[:4], dict(zip(_FFN_BIG_WEIGHTS, outs[4:4 + n_ffn])), dict(zip(_MIXER_BIG_WEIGHTS, outs[4 + n_ffn:])))


def _ffn_call(xp, xs, small, big, layer, *, final_norm, rows=None, seq_major_out=None, next_f32=None):
    n_rows = xp.shape[0]
    sample_spec = pl.BlockSpec(xs.shape, lambda i: (0, 0))
    gf = small["final_norm_g"]
    w_specs, w_arrays = _weight_operands(_FFN_WEIGHTS, small, big, layer)
    in_specs = [None, _resident(xs.shape, lambda i: (0, 0))] + w_specs + [_resident(gf.shape, lambda i: (0, 0))]
    args = (xp, xs, *w_arrays, gf)
    sample_shape = jax.ShapeDtypeStruct(xs.shape, F32)
    if seq_major_out is not None:
        n_seq, n_t = seq_major_out
        n_time = n_rows // n_seq
        assert final_norm and next_f32 is None and (n_time - N_META) % n_t == 0
        tiles_per_seq = (n_time - N_META) // n_t
        n_prompt_steps = n_seq * tiles_per_seq
        last = n_prompt_steps - 1
        y_spec = pl.BlockSpec((None, n_t, D_MODEL), lambda i: (jnp.minimum(i, last) // tiles_per_seq,
                                                               jnp.minimum(i, last) % tiles_per_seq, 0))
        in_specs[0] = pl.BlockSpec(memory_space=pl.ANY)
        return pl.pallas_call(
            functools.partial(_ffn_to_seq_major_kernel, n_t=n_t, layer=layer, n_prompt_steps=n_prompt_steps),
            grid=(n_prompt_steps + 1,), in_specs=in_specs, out_specs=[y_spec, sample_spec],
            out_shape=[jax.ShapeDtypeStruct((n_seq, n_time - N_META, D_MODEL), F32), sample_shape],
            scratch_shapes=[pltpu.VMEM((2, n_t, D_MODEL), F32), pltpu.SemaphoreType.DMA((2,))],
            compiler_params=_COMPILER_PARAMS, name=f"ffn_t{n_t}_out",
        )(xp.reshape(n_time, n_seq, D_MODEL), *args[1:])
    assert n_rows % rows == 0
    n_prompt_steps = n_rows // rows
    prompt_spec = pl.BlockSpec((rows, D_MODEL), lambda i: (jnp.minimum(i, n_prompt_steps - 1), 0))
    in_specs[0] = prompt_spec
    out_specs, out_shape = [prompt_spec, sample_spec], [jax.ShapeDtypeStruct((n_rows, D_MODEL), F32), sample_shape]
    if next_f32 is not None:
        cast_in, cast_out, cast_shapes, cast_arrays = _cast_operands(_BIG_WEIGHTS, next_f32, layer + 1,
                                                                     n_prompt_steps)
        in_specs, out_specs, out_shape = in_specs + cast_in, out_specs + cast_out, out_shape + cast_shapes
        args += tuple(cast_arrays)
    outs = pl.pallas_call(
        functools.partial(_ffn_kernel, final_norm=final_norm, layer=layer, n_prompt_steps=n_prompt_steps),
        grid=(n_prompt_steps + 1,), in_specs=in_specs, out_specs=out_specs, out_shape=out_shape,
        compiler_params=_COMPILER_PARAMS,
        name=f"ffn_r{rows}" + ("_cast" if next_f32 is not None else ""),
    )(*args)
    if next_f32 is None:
        return outs
    return outs[0], outs[1], dict(zip(_BIG_WEIGHTS, outs[2:]))


def _block_diag_gates(gate_a_w, gate_x_w):
    hpc = GATE_CHUNK // RNN_HEAD_DIM

    def bd(wt):
        wt = wt.reshape(DEPTH, N_GATE_CHUNKS, hpc, RNN_HEAD_DIM, RNN_HEAD_DIM)
        bands = [jnp.pad(wt[:, :, p], ((0, 0), (0, 0), (0, 0), (p * RNN_HEAD_DIM, (hpc - 1 - p) * RNN_HEAD_DIM)))
                 for p in range(hpc)]
        return jnp.stack(bands, axis=2).reshape(DEPTH, N_GATE_CHUNKS, GATE_CHUNK, GATE_CHUNK)

    return jnp.concatenate([bd(gate_a_w), bd(gate_x_w)], axis=-1).astype(BF16)


def _time_major(states):
    return jnp.swapaxes(states, 1, 2).reshape(states.shape[0], -1, D_MODEL)


def _seq_major(states, n_seq):
    return jnp.swapaxes(jnp.stack(states).reshape(len(states), -1, n_seq, D_MODEL), 1, 2)


PROMPT_N_T = 86
PROMPT_FFN_ROWS = 688
PROMPT_IO_N_T = 48
PROMPT_OUT_N_T = 512


def kernel(x_prompt, x_sample, state_rnn_h, state_rnn_conv, state_sc_conv, meta_tokens, norm1_g, w_in, rnn_conv_w, rnn_conv_b, gate_a_w, gate_a_b, gate_x_w, gate_x_b, lru_lambda, w_branch_a, sc_conv_w, w_branch_b, w_out, norm2_g, w_ff_gate, w_ff_up, w_ff_down, final_norm_g):
    small = {
        "norm1_g": norm1_g, "rnn_conv_w": rnn_conv_w, "rnn_conv_b": rnn_conv_b,
        "w_gate": _block_diag_gates(0.5 * gate_a_w, 0.5 * gate_x_w),
        "gate_a_b": gate_a_b, "gate_x_b": gate_x_b, "lru_lambda": lru_lambda,
        "sc_conv_w": sc_conv_w, "norm2_g": norm2_g, "final_norm_g": final_norm_g.reshape(1, D_MODEL),
    }
    big_f32 = {"w_in": w_in, "w_branch_a": w_branch_a, "w_branch_b": w_branch_b, "w_out": w_out,
               "w_ff_gate": w_ff_gate, "w_ff_up": w_ff_up, "w_ff_down": w_ff_down}
    big = [{}]
    dt = x_prompt.dtype

    bp = x_prompt.shape[0]
    bs, t_s, _ = x_sample.shape
    zeros = lambda k: jnp.zeros((1, k * bp, D_MODEL), dt)
    prompt_states = (zeros(1), zeros(RNN_CONV_W - 1), zeros(SC_CONV_W - 1))
    sample_states = (state_rnn_h, _time_major(state_rnn_conv), _time_major(state_sc_conv))
    xp = x_prompt
    xs = jnp.swapaxes(x_sample, 0, 1).reshape(t_s * bs, D_MODEL)
    new_p, new_s = [], []
    for layer in range(DEPTH):
        if layer == 0:
            xp, *st_p, ffn_big, mixer_big = _mixer_call(xp, *prompt_states, small, big[layer], layer, n_seq=bp,
                                                        n_t=PROMPT_IO_N_T, meta=meta_tokens.astype(dt), big_f32=big_f32)
            big[layer] = {**mixer_big, **ffn_big}
        else:
            xp, *st_p = _mixer_call(xp, *prompt_states, small, big[layer], layer, n_seq=bp, n_t=PROMPT_N_T)
        xs, *st_s = _mixer_call(xs, *sample_states, small, big[layer], layer, n_seq=bs, n_t=t_s)
        new_p.append(st_p)
        new_s.append(st_s)
        if layer == DEPTH - 1:
            y_prompt, ys = _ffn_call(xp, xs, small, big[layer], layer, final_norm=True,
                                     seq_major_out=(bp, PROMPT_OUT_N_T))
        else:
            xp, xs, next_big = _ffn_call(xp, xs, small, big[layer], layer, rows=PROMPT_FFN_ROWS, final_norm=False,
                                         next_f32=big_f32)
            big.append(next_big)
    y_sample = jnp.swapaxes(ys.reshape(t_s, bs, D_MODEL), 0, 1)

    def collect(states, n_seq):
        hs, rcs, scs = zip(*states)
        return jnp.stack(hs), _seq_major(list(rcs), n_seq), _seq_major(list(scs), n_seq)

    rnn_h_p, rnn_conv_p, sc_conv_p = collect(new_p, bp)
    rnn_h_s, rnn_conv_s, sc_conv_s = collect(new_s, bs)

    return (y_prompt, y_sample, rnn_h_p, rnn_conv_p, sc_conv_p, rnn_h_s, rnn_conv_s, sc_conv_s)
```
